```python
import jax, jax.numpy as jnp
from jax import lax
import numpy as np

D_MODEL = 2048
BATCH = 8
SEQ = 2048
DEPTH = 4

HEAD_DIM = 128
D_ATT = D_MODEL // 2
N_HEADS_ATT = D_ATT // HEAD_DIM
Q_BLOCK = 128
D_SC = D_MODEL // 4
N_GROUPS_SC = 4
SC_WIDTH = 3
D_CF = D_MODEL // 4
N_GROUPS_CF = 4
CF_WIDTH = 31
N_BRANCH = 3
D_FF = 4 * D_MODEL
N_IN = 3 * D_ATT + 3 * D_SC + 2 * D_CF + N_BRANCH * D_MODEL
RMS_EPS = 1e-6
LN_EPS = 1e-5

kernel_name = 'gated_parallel_hybrid_sb_conv_block'


def rms_norm(x, g):
    xf = x.astype(jnp.float32)
    y = xf * lax.rsqrt(jnp.mean(jnp.square(xf), axis=-1, keepdims=True) + RMS_EPS)
    return (y * g.astype(jnp.float32)).astype(x.dtype)


def layer_norm(x, g, b):
    xf = x.astype(jnp.float32)
    mu = jnp.mean(xf, axis=-1, keepdims=True)
    var = jnp.mean(jnp.square(xf - mu), axis=-1, keepdims=True)
    y = (xf - mu) * lax.rsqrt(var + LN_EPS)
    return (y * g.astype(jnp.float32) + b.astype(jnp.float32)).astype(x.dtype)


def causal_dwconv(u, w):
    K, C = w.shape
    return lax.conv_general_dilated(
        u, w[:, None, :].astype(u.dtype), window_strides=(1,), padding=[(K - 1, 0)],
        dimension_numbers=('NWC', 'WIO', 'NWC'), feature_group_count=C)


def stick_breaking_attention(q, k, v):
    S = q.shape[1]
    scale = HEAD_DIM ** -0.5
    outs = []
    for blk in range(S // Q_BLOCK):
        q0 = blk * Q_BLOCK
        kend = q0 + Q_BLOCK
        qb = q[:, q0:kend]
        kb = k[:, :kend]
        vb = v[:, :kend]
        z = jnp.einsum('bthd,bshd->bhts', qb, kb).astype(jnp.float32) * scale
        t_idx = q0 + jnp.arange(Q_BLOCK)[:, None]
        s_idx = jnp.arange(kend)[None, :]
        mask = s_idx < t_idx
        log_fail = jnp.where(mask, jax.nn.log_sigmoid(-z), 0.0)
        suffix = lax.cumsum(log_fail, axis=3, reverse=True) - log_fail
        log_a = jax.nn.log_sigmoid(z) + suffix
        a = jnp.where(mask, jnp.exp(log_a), 0.0)
        outs.append(jnp.einsum('bhts,bshd->bthd', a.astype(v.dtype), vb))
    return jnp.concatenate(outs, axis=1)


def _fwd_setup_inputs(seed: int = 0) -> dict:
    key = jax.random.key(seed)
    ks = jax.random.split(key, 20)
    f32 = jnp.float32

    def nrm(k, shape, fan_in):
        return jax.random.normal(k, shape, f32) * (fan_in ** -0.5)

    def gain(k, shape):
        return jnp.ones(shape, f32) + 0.02 * jax.random.normal(k, shape, f32)

    return {
        'x': jax.random.normal(ks[0], (BATCH, SEQ, D_MODEL), f32),
        'ln_mix_pre': gain(ks[1], (DEPTH, D_MODEL)),
        'ln_mix_post': gain(ks[2], (DEPTH, D_MODEL)),
        'ln_mlp_pre': gain(ks[3], (DEPTH, D_MODEL)),
        'ln_mlp_post': gain(ks[4], (DEPTH, D_MODEL)),
        'w_in': nrm(ks[5], (DEPTH, D_MODEL, N_IN), D_MODEL),
        'conv_a_w': nrm(ks[6], (DEPTH, SC_WIDTH, D_SC), SC_WIDTH),
        'proj_a': nrm(ks[7], (DEPTH, D_SC, D_MODEL), D_SC),
        'proj_b': nrm(ks[8], (DEPTH, D_ATT, D_MODEL), D_ATT),
        'conv_c_w': nrm(ks[9], (DEPTH, CF_WIDTH, D_CF), CF_WIDTH),
        'conv_c_b': 0.02 * jax.random.normal(ks[10], (DEPTH, D_CF), f32),
        'norm_c_g': gain(ks[11], (DEPTH, D_CF)),
        'norm_c_b': 0.02 * jax.random.normal(ks[12], (DEPTH, D_CF), f32),
        'proj_c': nrm(ks[13], (DEPTH, D_CF, D_MODEL), D_CF),
        'w_o': nrm(ks[14], (DEPTH, D_MODEL, D_MODEL), D_MODEL),
        'w_up': nrm(ks[15], (DEPTH, D_MODEL, D_FF), D_MODEL),
        'w_down': nrm(ks[16], (DEPTH, D_FF, D_MODEL), D_FF),
    }


def _fwd_reference(x, ln_mix_pre, ln_mix_post, ln_mlp_pre, ln_mlp_post, w_in, conv_a_w, proj_a,
              proj_b, conv_c_w, conv_c_b, norm_c_g, norm_c_b, proj_c, w_o, w_up, w_down):
    Bsz, S, _ = x.shape
    sizes = [D_ATT, D_ATT, D_ATT, D_SC, D_SC, D_SC, 2 * D_CF, N_BRANCH * D_MODEL]
    splits = [int(c) for c in np.cumsum(sizes)[:-1]]
    for l in range(DEPTH):
        h = rms_norm(x, ln_mix_pre[l])
        proj = h @ w_in[l]
        q, k, v, sc_b, sc_c, sc_u, cf_in, gate_logits = jnp.split(proj, splits, axis=-1)
        ya = (sc_b * causal_dwconv(sc_c * sc_u, conv_a_w[l])) @ proj_a[l]
        q = q.reshape(Bsz, S, N_HEADS_ATT, HEAD_DIM)
        k = k.reshape(Bsz, S, N_HEADS_ATT, HEAD_DIM)
        v = v.reshape(Bsz, S, N_HEADS_ATT, HEAD_DIM)
        yb = stick_breaking_attention(q, k, v).reshape(Bsz, S, D_ATT) @ proj_b[l]
        cf_a, cf_g = jnp.split(cf_in, 2, axis=-1)
        u = cf_a * jax.nn.sigmoid(cf_g)
        u = causal_dwconv(u, conv_c_w[l]) + conv_c_b[l]
        u = jax.nn.silu(layer_norm(u, norm_c_g[l], norm_c_b[l]))
        yc = u @ proj_c[l]
        g_a, g_b, g_c = jnp.split(jax.nn.sigmoid(gate_logits), N_BRANCH, axis=-1)
        mixed = (g_a * ya + g_b * yb + g_c * yc) @ w_o[l]
        x = x + rms_norm(mixed, ln_mix_post[l])
        h = rms_norm(x, ln_mlp_pre[l])
        f = jnp.square(jax.nn.relu(h @ w_up[l])) @ w_down[l]
        x = x + rms_norm(f, ln_mlp_post[l])
    return x


import jax as _jax
import jax.numpy as _jnp

TWIN_FORMAT = 'train_step'
FWD_PARAMS = ['x', 'ln_mix_pre', 'ln_mix_post', 'ln_mlp_pre', 'ln_mlp_post', 'w_in', 'conv_a_w', 'proj_a', 'proj_b', 'conv_c_w', 'conv_c_b', 'norm_c_g', 'norm_c_b', 'proj_c', 'w_o', 'w_up', 'w_down']
TWIN_WEIGHTS = ['ln_mix_pre', 'ln_mix_post', 'ln_mlp_pre', 'ln_mlp_post', 'w_in', 'conv_a_w', 'proj_a', 'proj_b', 'conv_c_w', 'conv_c_b', 'norm_c_g', 'norm_c_b', 'proj_c', 'w_o', 'w_up', 'w_down']
TWIN_DIFF_INPUT = 'x'
TWIN_INPUTS = ['x', 'ln_mix_pre', 'ln_mix_post', 'ln_mlp_pre', 'ln_mlp_post', 'w_in', 'conv_a_w', 'proj_a', 'proj_b', 'conv_c_w', 'conv_c_b', 'norm_c_g', 'norm_c_b', 'proj_c', 'w_o', 'w_up', 'w_down', 'loss_target', 'm_ln_mix_pre', 'm_ln_mix_post', 'm_ln_mlp_pre', 'm_ln_mlp_post', 'm_w_in', 'm_conv_a_w', 'm_proj_a', 'm_proj_b', 'm_conv_c_w', 'm_conv_c_b', 'm_norm_c_g', 'm_norm_c_b', 'm_proj_c', 'm_w_o', 'm_w_up', 'm_w_down', 'v_ln_mix_pre', 'v_ln_mix_post', 'v_ln_mlp_pre', 'v_ln_mlp_post', 'v_w_in', 'v_conv_a_w', 'v_proj_a', 'v_proj_b', 'v_conv_c_w', 'v_conv_c_b', 'v_norm_c_g', 'v_norm_c_b', 'v_proj_c', 'v_w_o', 'v_w_up', 'v_w_down']
TWIN_OUTPUTS = ['loss', 'grad_x', 'grad_ln_mix_pre', 'grad_ln_mix_post', 'grad_ln_mlp_pre', 'grad_ln_mlp_post', 'grad_w_in', 'grad_conv_a_w', 'grad_proj_a', 'grad_proj_b', 'grad_conv_c_w', 'grad_conv_c_b', 'grad_norm_c_g', 'grad_norm_c_b', 'grad_proj_c', 'grad_w_o', 'grad_w_up', 'grad_w_down', 'delta_ln_mix_pre', 'delta_ln_mix_post', 'delta_ln_mlp_pre', 'delta_ln_mlp_post', 'delta_w_in', 'delta_conv_a_w', 'delta_proj_a', 'delta_proj_b', 'delta_conv_c_w', 'delta_conv_c_b', 'delta_norm_c_g', 'delta_norm_c_b', 'delta_proj_c', 'delta_w_o', 'delta_w_up', 'delta_w_down', 'new_m_ln_mix_pre', 'new_m_ln_mix_post', 'new_m_ln_mlp_pre', 'new_m_ln_mlp_post', 'new_m_w_in', 'new_m_conv_a_w', 'new_m_proj_a', 'new_m_proj_b', 'new_m_conv_c_w', 'new_m_conv_c_b', 'new_m_norm_c_g', 'new_m_norm_c_b', 'new_m_proj_c', 'new_m_w_o', 'new_m_w_up', 'new_m_w_down', 'new_v_ln_mix_pre', 'new_v_ln_mix_post', 'new_v_ln_mlp_pre', 'new_v_ln_mlp_post', 'new_v_w_in', 'new_v_conv_a_w', 'new_v_proj_a', 'new_v_proj_b', 'new_v_conv_c_w', 'new_v_conv_c_b', 'new_v_norm_c_g', 'new_v_norm_c_b', 'new_v_proj_c', 'new_v_w_o', 'new_v_w_up', 'new_v_w_down']
TWIN_LEAF_KINDS = {'loss': 'loss', 'grad_x': 'grad_x', 'grad_ln_mix_pre': 'grad_w', 'grad_ln_mix_post': 'grad_w', 'grad_ln_mlp_pre': 'grad_w', 'grad_ln_mlp_post': 'grad_w', 'grad_w_in': 'grad_w', 'grad_conv_a_w': 'grad_w', 'grad_proj_a': 'grad_w', 'grad_proj_b': 'grad_w', 'grad_conv_c_w': 'grad_w', 'grad_conv_c_b': 'grad_w', 'grad_norm_c_g': 'grad_w', 'grad_norm_c_b': 'grad_w', 'grad_proj_c': 'grad_w', 'grad_w_o': 'grad_w', 'grad_w_up': 'grad_w', 'grad_w_down': 'grad_w', 'delta_ln_mix_pre': 'delta_w', 'delta_ln_mix_post': 'delta_w', 'delta_ln_mlp_pre': 'delta_w', 'delta_ln_mlp_post': 'delta_w', 'delta_w_in': 'delta_w', 'delta_conv_a_w': 'delta_w', 'delta_proj_a': 'delta_w', 'delta_proj_b': 'delta_w', 'delta_conv_c_w': 'delta_w', 'delta_conv_c_b': 'delta_w', 'delta_norm_c_g': 'delta_w', 'delta_norm_c_b': 'delta_w', 'delta_proj_c': 'delta_w', 'delta_w_o': 'delta_w', 'delta_w_up': 'delta_w', 'delta_w_down': 'delta_w', 'new_m_ln_mix_pre': 'new_m', 'new_m_ln_mix_post': 'new_m', 'new_m_ln_mlp_pre': 'new_m', 'new_m_ln_mlp_post': 'new_m', 'new_m_w_in': 'new_m', 'new_m_conv_a_w': 'new_m', 'new_m_proj_a': 'new_m', 'new_m_proj_b': 'new_m', 'new_m_conv_c_w': 'new_m', 'new_m_conv_c_b': 'new_m', 'new_m_norm_c_g': 'new_m', 'new_m_norm_c_b': 'new_m', 'new_m_proj_c': 'new_m', 'new_m_w_o': 'new_m', 'new_m_w_up': 'new_m', 'new_m_w_down': 'new_m', 'new_v_ln_mix_pre': 'new_v', 'new_v_ln_mix_post': 'new_v', 'new_v_ln_mlp_pre': 'new_v', 'new_v_ln_mlp_post': 'new_v', 'new_v_w_in': 'new_v', 'new_v_conv_a_w': 'new_v', 'new_v_proj_a': 'new_v', 'new_v_proj_b': 'new_v', 'new_v_conv_c_w': 'new_v', 'new_v_conv_c_b': 'new_v', 'new_v_norm_c_g': 'new_v', 'new_v_norm_c_b': 'new_v', 'new_v_proj_c': 'new_v', 'new_v_w_o': 'new_v', 'new_v_w_up': 'new_v', 'new_v_w_down': 'new_v'}


def _forward(args):
    return _fwd_reference(*[args[k] for k in FWD_PARAMS])


def _output_shape():
    out = _jax.eval_shape(lambda: _forward(_fwd_setup_inputs(0)))
    return out.shape, out.dtype

N_MICROBATCH = 1
ADAM_LR = 0.001
ADAM_B1 = 0.9
ADAM_B2 = 0.999
ADAM_EPS = 1e-08
ADAM_WD = 0.01
ADAM_STEP = 10
PER_EXAMPLE_BATCH_AXIS = {'x': 0, 'loss_target': 0}
SHARED_INPUTS = []
_WEIGHT_DTYPES = {'ln_mix_pre': _jnp.float32, 'ln_mix_post': _jnp.float32, 'ln_mlp_pre': _jnp.float32, 'ln_mlp_post': _jnp.float32, 'w_in': _jnp.float32, 'conv_a_w': _jnp.float32, 'proj_a': _jnp.float32, 'proj_b': _jnp.float32, 'conv_c_w': _jnp.float32, 'conv_c_b': _jnp.float32, 'norm_c_g': _jnp.float32, 'norm_c_b': _jnp.float32, 'proj_c': _jnp.float32, 'w_o': _jnp.float32, 'w_up': _jnp.float32, 'w_down': _jnp.float32}
MOMENT_SCALE = {'ln_mix_pre': 1.505996e+00, 'ln_mix_post': 8.301540e+00, 'ln_mlp_pre': 1.357733e+00, 'ln_mlp_post': 9.130779e+00, 'w_in': 6.380864e-01, 'conv_a_w': 6.335852e-01, 'proj_a': 3.408502e-01, 'proj_b': 1.447092e+00, 'conv_c_w': 1.534677e+00, 'conv_c_b': 1.218236e+01, 'norm_c_g': 5.031409e+00, 'norm_c_b': 6.941200e+00, 'proj_c': 1.566743e+00, 'w_o': 2.058710e+00, 'w_up': 6.717624e-01, 'w_down': 3.777524e+00}


def _to_microbatches(a, axis):
    t = _jnp.moveaxis(a, axis, 0)
    t = t.reshape((N_MICROBATCH, t.shape[0] // N_MICROBATCH) + t.shape[1:])
    return _jnp.moveaxis(t, 1, axis + 1)


def setup_inputs(seed: int = 0) -> dict:
    inp = _fwd_setup_inputs(seed)
    key = _jax.random.fold_in(_jax.random.key(seed), 7919)
    shape, _ = _output_shape()
    out = dict(inp)
    out["loss_target"] = _jax.random.normal(_jax.random.fold_in(key, 0), shape, _jnp.float32)
    for i, name in enumerate(TWIN_WEIGHTS):
        w = inp[name].astype(_jnp.float32)
        if MOMENT_SCALE is None:
            s = _jnp.sqrt(_jnp.mean(_jnp.square(w)) + 1e-30)
        else:
            s = MOMENT_SCALE[name]
        km, kv = _jax.random.split(_jax.random.fold_in(key, i + 1))
        out[name] = w
        out["m_" + name] = s * _jax.random.normal(km, w.shape, _jnp.float32)
        out["v_" + name] = (s * s) * _jax.random.uniform(kv, w.shape, _jnp.float32, 0.5, 1.5)
    if N_MICROBATCH > 1:
        for name, axis in PER_EXAMPLE_BATCH_AXIS.items():
            out[name] = _to_microbatches(out[name], axis)
    return {'x': out['x'], 'ln_mix_pre': out['ln_mix_pre'], 'ln_mix_post': out['ln_mix_post'], 'ln_mlp_pre': out['ln_mlp_pre'], 'ln_mlp_post': out['ln_mlp_post'], 'w_in': out['w_in'], 'conv_a_w': out['conv_a_w'], 'proj_a': out['proj_a'], 'proj_b': out['proj_b'], 'conv_c_w': out['conv_c_w'], 'conv_c_b': out['conv_c_b'], 'norm_c_g': out['norm_c_g'], 'norm_c_b': out['norm_c_b'], 'proj_c': out['proj_c'], 'w_o': out['w_o'], 'w_up': out['w_up'], 'w_down': out['w_down'], 'loss_target': out['loss_target'], 'm_ln_mix_pre': out['m_ln_mix_pre'], 'm_ln_mix_post': out['m_ln_mix_post'], 'm_ln_mlp_pre': out['m_ln_mlp_pre'], 'm_ln_mlp_post': out['m_ln_mlp_post'], 'm_w_in': out['m_w_in'], 'm_conv_a_w': out['m_conv_a_w'], 'm_proj_a': out['m_proj_a'], 'm_proj_b': out['m_proj_b'], 'm_conv_c_w': out['m_conv_c_w'], 'm_conv_c_b': out['m_conv_c_b'], 'm_norm_c_g': out['m_norm_c_g'], 'm_norm_c_b': out['m_norm_c_b'], 'm_proj_c': out['m_proj_c'], 'm_w_o': out['m_w_o'], 'm_w_up': out['m_w_up'], 'm_w_down': out['m_w_down'], 'v_ln_mix_pre': out['v_ln_mix_pre'], 'v_ln_mix_post': out['v_ln_mix_post'], 'v_ln_mlp_pre': out['v_ln_mlp_pre'], 'v_ln_mlp_post': out['v_ln_mlp_post'], 'v_w_in': out['v_w_in'], 'v_conv_a_w': out['v_conv_a_w'], 'v_proj_a': out['v_proj_a'], 'v_proj_b': out['v_proj_b'], 'v_conv_c_w': out['v_conv_c_w'], 'v_conv_c_b': out['v_conv_c_b'], 'v_norm_c_g': out['v_norm_c_g'], 'v_norm_c_b': out['v_norm_c_b'], 'v_proj_c': out['v_proj_c'], 'v_w_o': out['v_w_o'], 'v_w_up': out['v_w_up'], 'v_w_down': out['v_w_down']}


def _loss(weights, diff, rest, loss_target):
    with _jax.named_scope("forward"):
        args = {**rest, TWIN_DIFF_INPUT: diff, **{k: w.astype(_WEIGHT_DTYPES[k]) for k, w in weights.items()}}
        y = _forward(args)
    with _jax.named_scope("loss_head"):
        err = _jnp.square(y.astype(_jnp.float32) - loss_target)
        return 0.5 * _jnp.sum(_jnp.mean(err, axis=-1)) if err.ndim else 0.5 * err


def _adamw(w, g, m, v):
    m = ADAM_B1 * m + (1.0 - ADAM_B1) * g
    v = ADAM_B2 * v + (1.0 - ADAM_B2) * _jnp.square(g)
    m_hat = m / (1.0 - ADAM_B1 ** ADAM_STEP)
    v_hat = v / (1.0 - ADAM_B2 ** ADAM_STEP)
    delta = -ADAM_LR * (m_hat / (_jnp.sqrt(v_hat) + ADAM_EPS) + ADAM_WD * w)
    return delta, m, v


def reference(x, ln_mix_pre, ln_mix_post, ln_mlp_pre, ln_mlp_post, w_in, conv_a_w, proj_a, proj_b, conv_c_w, conv_c_b, norm_c_g, norm_c_b, proj_c, w_o, w_up, w_down, loss_target, m_ln_mix_pre, m_ln_mix_post, m_ln_mlp_pre, m_ln_mlp_post, m_w_in, m_conv_a_w, m_proj_a, m_proj_b, m_conv_c_w, m_conv_c_b, m_norm_c_g, m_norm_c_b, m_proj_c, m_w_o, m_w_up, m_w_down, v_ln_mix_pre, v_ln_mix_post, v_ln_mlp_pre, v_ln_mlp_post, v_w_in, v_conv_a_w, v_proj_a, v_proj_b, v_conv_c_w, v_conv_c_b, v_norm_c_g, v_norm_c_b, v_proj_c, v_w_o, v_w_up, v_w_down):
    given = dict(x=x, ln_mix_pre=ln_mix_pre, ln_mix_post=ln_mix_post, ln_mlp_pre=ln_mlp_pre, ln_mlp_post=ln_mlp_post, w_in=w_in, conv_a_w=conv_a_w, proj_a=proj_a, proj_b=proj_b, conv_c_w=conv_c_w, conv_c_b=conv_c_b, norm_c_g=norm_c_g, norm_c_b=norm_c_b, proj_c=proj_c, w_o=w_o, w_up=w_up, w_down=w_down, loss_target=loss_target, m_ln_mix_pre=m_ln_mix_pre, m_ln_mix_post=m_ln_mix_post, m_ln_mlp_pre=m_ln_mlp_pre, m_ln_mlp_post=m_ln_mlp_post, m_w_in=m_w_in, m_conv_a_w=m_conv_a_w, m_proj_a=m_proj_a, m_proj_b=m_proj_b, m_conv_c_w=m_conv_c_w, m_conv_c_b=m_conv_c_b, m_norm_c_g=m_norm_c_g, m_norm_c_b=m_norm_c_b, m_proj_c=m_proj_c, m_w_o=m_w_o, m_w_up=m_w_up, m_w_down=m_w_down, v_ln_mix_pre=v_ln_mix_pre, v_ln_mix_post=v_ln_mix_post, v_ln_mlp_pre=v_ln_mlp_pre, v_ln_mlp_post=v_ln_mlp_post, v_w_in=v_w_in, v_conv_a_w=v_conv_a_w, v_proj_a=v_proj_a, v_proj_b=v_proj_b, v_conv_c_w=v_conv_c_w, v_conv_c_b=v_conv_c_b, v_norm_c_g=v_norm_c_g, v_norm_c_b=v_norm_c_b, v_proj_c=v_proj_c, v_w_o=v_w_o, v_w_up=v_w_up, v_w_down=v_w_down)
    weights = {n: given[n] for n in TWIN_WEIGHTS}
    shared = {n: given[n] for n in SHARED_INPUTS}
    per_example = {n: given[n] for n in ['x']}
    grad_fn = _jax.value_and_grad(_loss, argnums=(0, 1))

    def one_microbatch(ex, loss_target):
        ex = dict(ex)
        diff = ex.pop(TWIN_DIFF_INPUT)
        return grad_fn(weights, diff, {**shared, **ex}, loss_target)

    if N_MICROBATCH == 1:
        loss, (grad_w, grad_x) = one_microbatch(per_example, given["loss_target"])
    else:
        def body(carry, xs):
            loss_sum, grad_sum = carry
            l_k, (gw_k, gx_k) = one_microbatch(xs[0], xs[1])
            with _jax.named_scope("update"):
                return (loss_sum + l_k, _jax.tree.map(_jnp.add, grad_sum, gw_k)), gx_k

        init = (_jnp.zeros((), _jnp.float32), _jax.tree.map(_jnp.zeros_like, weights))
        (loss, grad_w), grad_x = _jax.lax.scan(body, init, (per_example, given["loss_target"]))
    with _jax.named_scope("update"):
        delta_w, new_m, new_v = {}, {}, {}
        for n in TWIN_WEIGHTS:
            delta_w[n], new_m[n], new_v[n] = _adamw(weights[n], grad_w[n], given["m_" + n], given["v_" + n])
    return (loss, grad_x, *[grad_w[n] for n in TWIN_WEIGHTS], *[delta_w[n] for n in TWIN_WEIGHTS],
            *[new_m[n] for n in TWIN_WEIGHTS], *[new_v[n] for n in TWIN_WEIGHTS])
```

```python
import functools

import jax
import jax.numpy as jnp
from jax import lax
from jax.experimental import pallas as pl
from jax.experimental.pallas import tpu as pltpu

F32 = jnp.float32
BF16 = jnp.bfloat16
MESH = pl.DeviceIdType.MESH

HEAD_DIM = 128
QB = 128
RMS_EPS = 1e-6
LN_EPS = 1e-5
ADAM_LR = 0.001
ADAM_B1 = 0.9
ADAM_B2 = 0.999
ADAM_EPS = 1e-08
ADAM_WD = 0.01
ADAM_STEP = 10
LANE = 128
VMEM_LIMIT = 56 * 1024 * 1024
CONV_PAD = 32
CONV_ROWS = 256


def _tile(n, cap, mult=LANE):
    best = None
    t = mult
    while t <= min(n, cap):
        if n % t == 0:
            best = t
        t += mult
    return best if best is not None else n


def _params(*sem):
    return pltpu.CompilerParams(dimension_semantics=sem if sem else None, vmem_limit_bytes=VMEM_LIMIT)


def _sigmoid(x):
    return 1.0 / (1.0 + jnp.exp(-x))


def _mm(a, b, *, name, ta=False, tb=False, b_layer=None, out_dtypes=(F32,), epilogue=None, extras=(),
        tm_cap=1024, tn_cap=1024, tk_cap=512):
    if ta:
        K, M = a.shape
    else:
        M, K = a.shape
    N = b.shape[-2] if tb else b.shape[-1]
    tm, tn, tk = _tile(M, tm_cap), _tile(N, tn_cap), _tile(K, tk_cap)
    nk = K // tk
    a_spec = pl.BlockSpec((tk, tm), lambda i, j, k: (k, i)) if ta else pl.BlockSpec((tm, tk), lambda i, j, k: (i, k))
    if tb:
        b_blk, b_idx = (tn, tk), (lambda i, j, k: (j, k))
    else:
        b_blk, b_idx = (tk, tn), (lambda i, j, k: (k, j))
    if b_layer is None:
        b_spec = pl.BlockSpec(b_blk, b_idx)
    else:
        b_spec = pl.BlockSpec((None,) + b_blk, lambda i, j, k: (b_layer,) + b_idx(i, j, k))
    e_specs = [pl.BlockSpec((tm, tn), lambda i, j, k: (i, j)) for _ in extras]
    dims = (((0 if ta else 1,), (1 if tb else 0,)), ((), ()))
    n_e, n_o = len(extras), len(out_dtypes)

    def body(a_ref, b_ref, *rest):
        e_refs, o_refs, acc_ref = rest[:n_e], rest[n_e:n_e + n_o], rest[n_e + n_o]
        k = pl.program_id(2)

        @pl.when(k == 0)
        def _():
            acc_ref[...] = jnp.zeros_like(acc_ref)

        acc_ref[...] += lax.dot_general(a_ref[...].astype(BF16), b_ref[...].astype(BF16), dims,
                                        preferred_element_type=F32)

        @pl.when(k == nk - 1)
        def _():
            acc = acc_ref[...]
            outs = (acc,) if epilogue is None else epilogue(acc, *[e[...] for e in e_refs])
            for o_ref, o in zip(o_refs, outs):
                o_ref[...] = o.astype(o_ref.dtype)

    outs = pl.pallas_call(
        body, name=name, grid=(M // tm, N // tn, nk),
        in_specs=[a_spec, b_spec] + e_specs,
        out_specs=[pl.BlockSpec((tm, tn), lambda i, j, k: (i, j)) for _ in out_dtypes],
        out_shape=[jax.ShapeDtypeStruct((M, N), dt) for dt in out_dtypes],
        scratch_shapes=[pltpu.VMEM((tm, tn), F32)],
        compiler_params=_params("parallel", "parallel", "arbitrary"),
    )(a, b, *extras)
    return outs[0] if n_o == 1 else outs


def _row_tile(S):
    return _tile(S, 256, 8)


def _gain_spec(D, l):
    return pl.BlockSpec((None, 1, D), lambda i: (l, 0, 0))


def _rms(x, g):
    r = lax.rsqrt(jnp.mean(x * x, axis=-1, keepdims=True) + RMS_EPS)
    return x * r * g


def _rms_fwd(x, g3, l, name):
    S, D = x.shape
    tr = _row_tile(S)

    def body(x_ref, g_ref, h_ref):
        h_ref[...] = _rms(x_ref[...], g_ref[...]).astype(BF16)

    return pl.pallas_call(
        body, name=name, grid=(S // tr,),
        in_specs=[pl.BlockSpec((tr, D), lambda i: (i, 0)), _gain_spec(D, l)],
        out_specs=pl.BlockSpec((tr, D), lambda i: (i, 0)),
        out_shape=jax.ShapeDtypeStruct((S, D), BF16),
        compiler_params=_params("parallel"),
    )(x, g3)


def _post_res_fwd(x_in, f, gpost3, l, gnext3, lnext, name):
    S, D = x_in.shape
    tr = _row_tile(S)

    def body(x_ref, f_ref, gp_ref, gn_ref, xo_ref, h_ref):
        xo = x_ref[...] + _rms(f_ref[...], gp_ref[...])
        xo_ref[...] = xo
        h_ref[...] = _rms(xo, gn_ref[...]).astype(BF16)

    row = pl.BlockSpec((tr, D), lambda i: (i, 0))
    return pl.pallas_call(
        body, name=name, grid=(S // tr,),
        in_specs=[row, row, _gain_spec(D, l), _gain_spec(D, lnext)],
        out_specs=[row, row],
        out_shape=[jax.ShapeDtypeStruct((S, D), F32), jax.ShapeDtypeStruct((S, D), BF16)],
        compiler_params=_params("parallel"),
    )(x_in, f, gpost3, gnext3)


def _final_fwd_loss(x_in, f, gpost3, l, target, name):
    S, D = x_in.shape
    tr = _row_tile(S)

    def body(x_ref, f_ref, gp_ref, t_ref, dx_ref, loss_ref):
        @pl.when(pl.program_id(0) == 0)
        def _():
            loss_ref[...] = jnp.zeros_like(loss_ref)

        err = x_ref[...] + _rms(f_ref[...], gp_ref[...]) - t_ref[...]
        dx_ref[...] = err * (1.0 / D)
        loss_ref[...] += 0.5 * jnp.sum(jnp.mean(err * err, axis=-1, keepdims=True))

    row = pl.BlockSpec((tr, D), lambda i: (i, 0))
    return pl.pallas_call(
        body, name=name, grid=(S // tr,),
        in_specs=[row, row, _gain_spec(D, l), row],
        out_specs=[row, pl.BlockSpec((8, LANE), lambda i: (0, 0))],
        out_shape=[jax.ShapeDtypeStruct((S, D), F32), jax.ShapeDtypeStruct((8, LANE), F32)],
        compiler_params=_params("arbitrary"),
    )(x_in, f, gpost3, target)


def _rms_bwd_rows(dy, x, g):
    r = lax.rsqrt(jnp.mean(x * x, axis=-1, keepdims=True) + RMS_EPS)
    t = dy * g
    dx = r * t - x * (r * r * r) * jnp.mean(t * x, axis=-1, keepdims=True)
    return dx, dy * x * r


def _post_bwd(dxo, f, gpost3, l, name):
    S, D = f.shape
    tr = _row_tile(S)

    def body(d_ref, f_ref, g_ref, df_ref, dg_ref):
        @pl.when(pl.program_id(0) == 0)
        def _():
            dg_ref[...] = jnp.zeros_like(dg_ref)

        df, dg = _rms_bwd_rows(d_ref[...], f_ref[...], g_ref[...])
        df_ref[...] = df.astype(BF16)
        dg_ref[...] += jnp.sum(dg, axis=0, keepdims=True)

    row = pl.BlockSpec((tr, D), lambda i: (i, 0))
    return pl.pallas_call(
        body, name=name, grid=(S // tr,),
        in_specs=[row, row, _gain_spec(D, l)],
        out_specs=[row, pl.BlockSpec((1, D), lambda i: (0, 0))],
        out_shape=[jax.ShapeDtypeStruct((S, D), BF16), jax.ShapeDtypeStruct((1, D), F32)],
        compiler_params=_params("arbitrary"),
    )(dxo, f, gpost3)


def _pre_bwd(dxo, dh, x_in, gpre3, l, name):
    S, D = x_in.shape
    tr = _row_tile(S)

    def body(d_ref, dh_ref, x_ref, g_ref, dx_ref, dg_ref):
        @pl.when(pl.program_id(0) == 0)
        def _():
            dg_ref[...] = jnp.zeros_like(dg_ref)

        dx, dg = _rms_bwd_rows(dh_ref[...], x_ref[...], g_ref[...])
        dx_ref[...] = d_ref[...] + dx
        dg_ref[...] += jnp.sum(dg, axis=0, keepdims=True)

    row = pl.BlockSpec((tr, D), lambda i: (i, 0))
    return pl.pallas_call(
        body, name=name, grid=(S // tr,),
        in_specs=[row, row, row, _gain_spec(D, l)],
        out_specs=[row, pl.BlockSpec((1, D), lambda i: (0, 0))],
        out_shape=[jax.ShapeDtypeStruct((S, D), F32), jax.ShapeDtypeStruct((1, D), F32)],
        compiler_params=_params("arbitrary"),
    )(dxo, dh, x_in, gpre3)


def _zero_pads(pad_ref, S):
    z = jnp.zeros((CONV_PAD, pad_ref.shape[1]), F32)
    pad_ref[pl.ds(0, CONV_PAD), :] = z
    pad_ref[pl.ds(CONV_PAD + S, CONV_PAD), :] = z


def _conv_fwd_chunk(pad_ref, w_ref, K, r0, rows):
    acc = None
    for k in range(K):
        term = w_ref[pl.ds(k, 1), :] * pad_ref[pl.ds(CONV_PAD + r0 - (K - 1) + k, rows), :]
        acc = term if acc is None else acc + term
    return acc


def _conv_bwd_chunk(pad_ref, w_ref, K, r0, rows):
    acc = None
    for k in range(K):
        term = w_ref[pl.ds(k, 1), :] * pad_ref[pl.ds(CONV_PAD + r0 + (K - 1) - k, rows), :]
        acc = term if acc is None else acc + term
    return acc


def _conv_dw(upad_ref, dy_ref_or_pad, dy_off, K, S, dw_ref):
    rows = min(CONV_ROWS, S)
    for k in range(K):
        acc = None
        for r0 in range(0, S, rows):
            term = jnp.sum(dy_ref_or_pad[pl.ds(dy_off + r0, rows), :]
                           * upad_ref[pl.ds(CONV_PAD + r0 - (K - 1) + k, rows), :], axis=0, keepdims=True)
            acc = term if acc is None else acc + term
        dw_ref[pl.ds(k, 1), :] = acc


def _col_spec(S, off):
    return pl.BlockSpec((S, LANE), lambda j: (0, off // LANE + j))


def _sc_fwd(proj, conv_w, l, offs, DS, name):
    S = proj.shape[0]
    K = conv_w.shape[1]
    rows = min(CONV_ROWS, S)

    def body(b_ref, c_ref, u_ref, w_ref, o_ref, pad_ref):
        _zero_pads(pad_ref, S)
        pad_ref[pl.ds(CONV_PAD, S), :] = c_ref[...].astype(F32) * u_ref[...].astype(F32)
        for r0 in range(0, S, rows):
            cv = _conv_fwd_chunk(pad_ref, w_ref, K, r0, rows)
            o_ref[pl.ds(r0, rows), :] = (b_ref[pl.ds(r0, rows), :].astype(F32) * cv).astype(BF16)

    return pl.pallas_call(
        body, name=name, grid=(DS // LANE,),
        in_specs=[_col_spec(S, offs["sc_b"]), _col_spec(S, offs["sc_c"]), _col_spec(S, offs["sc_u"]),
                  pl.BlockSpec((None, K, LANE), lambda j: (l, 0, j))],
        out_specs=pl.BlockSpec((S, LANE), lambda j: (0, j)),
        out_shape=jax.ShapeDtypeStruct((S, DS), BF16),
        scratch_shapes=[pltpu.VMEM((S + 2 * CONV_PAD, LANE), F32)],
        compiler_params=_params("parallel"),
    )(proj, proj, proj, conv_w)


def _sc_bwd(dga, proj, conv_w, l, offs, DS, name):
    S = proj.shape[0]
    K = conv_w.shape[1]
    rows = min(CONV_ROWS, S)

    def body(d_ref, b_ref, c_ref, u_ref, w_ref, db_ref, dc_ref, du_ref, dw_ref, tpad_ref, gpad_ref):
        _zero_pads(tpad_ref, S)
        _zero_pads(gpad_ref, S)
        tpad_ref[pl.ds(CONV_PAD, S), :] = c_ref[...].astype(F32) * u_ref[...].astype(F32)
        for r0 in range(0, S, rows):
            sl = pl.ds(r0, rows)
            cv = _conv_fwd_chunk(tpad_ref, w_ref, K, r0, rows)
            d = d_ref[sl, :]
            db_ref[sl, :] = (d * cv).astype(BF16)
            gpad_ref[pl.ds(CONV_PAD + r0, rows), :] = d * b_ref[sl, :].astype(F32)
        for r0 in range(0, S, rows):
            sl = pl.ds(r0, rows)
            dt = _conv_bwd_chunk(gpad_ref, w_ref, K, r0, rows)
            dc_ref[sl, :] = (dt * u_ref[sl, :].astype(F32)).astype(BF16)
            du_ref[sl, :] = (dt * c_ref[sl, :].astype(F32)).astype(BF16)
        _conv_dw(tpad_ref, gpad_ref, CONV_PAD, K, S, dw_ref)

    blk = pl.BlockSpec((S, LANE), lambda j: (0, j))
    act = jax.ShapeDtypeStruct((S, DS), BF16)
    return pl.pallas_call(
        body, name=name, grid=(DS // LANE,),
        in_specs=[blk, _col_spec(S, offs["sc_b"]), _col_spec(S, offs["sc_c"]), _col_spec(S, offs["sc_u"]),
                  pl.BlockSpec((None, K, LANE), lambda j: (l, 0, j))],
        out_specs=[blk, blk, blk, pl.BlockSpec((K, LANE), lambda j: (0, j))],
        out_shape=[act, act, act, jax.ShapeDtypeStruct((K, DS), F32)],
        scratch_shapes=[pltpu.VMEM((S + 2 * CONV_PAD, LANE), F32), pltpu.VMEM((S + 2 * CONV_PAD, LANE), F32)],
        compiler_params=_params("parallel"),
    )(dga, proj, proj, proj, conv_w)


def _cf_conv_fwd(proj, conv_w, conv_b3, l, offs, DC, name):
    S = proj.shape[0]
    K = conv_w.shape[1]
    rows = min(CONV_ROWS, S)

    def body(a_ref, g_ref, w_ref, bias_ref, o_ref, pad_ref):
        _zero_pads(pad_ref, S)
        pad_ref[pl.ds(CONV_PAD, S), :] = a_ref[...].astype(F32) * _sigmoid(g_ref[...].astype(F32))
        for r0 in range(0, S, rows):
            o_ref[pl.ds(r0, rows), :] = _conv_fwd_chunk(pad_ref, w_ref, K, r0, rows) + bias_ref[...]

    return pl.pallas_call(
        body, name=name, grid=(DC // LANE,),
        in_specs=[_col_spec(S, offs["cf_a"]), _col_spec(S, offs["cf_g"]),
                  pl.BlockSpec((None, K, LANE), lambda j: (l, 0, j)),
                  pl.BlockSpec((None, 1, LANE), lambda j: (l, 0, j))],
        out_specs=pl.BlockSpec((S, LANE), lambda j: (0, j)),
        out_shape=jax.ShapeDtypeStruct((S, DC), F32),
        scratch_shapes=[pltpu.VMEM((S + 2 * CONV_PAD, LANE), F32)],
        compiler_params=_params("parallel"),
    )(proj, proj, conv_w, conv_b3)


def _layer_norm_hat(u):
    mu = jnp.mean(u, axis=-1, keepdims=True)
    xc = u - mu
    rstd = lax.rsqrt(jnp.mean(xc * xc, axis=-1, keepdims=True) + LN_EPS)
    return xc * rstd, rstd


def _cf_norm_fwd(u1, gam3, bet3, l, name):
    S, DC = u1.shape
    tr = _row_tile(S)

    def body(u_ref, g_ref, b_ref, o_ref):
        xhat, _ = _layer_norm_hat(u_ref[...])
        s = xhat * g_ref[...] + b_ref[...]
        o_ref[...] = (s * _sigmoid(s)).astype(BF16)

    row = pl.BlockSpec((tr, DC), lambda i: (i, 0))
    vec = pl.BlockSpec((None, 1, DC), lambda i: (l, 0, 0))
    return pl.pallas_call(
        body, name=name, grid=(S // tr,),
        in_specs=[row, vec, vec], out_specs=row,
        out_shape=jax.ShapeDtypeStruct((S, DC), BF16),
        compiler_params=_params("parallel"),
    )(u1, gam3, bet3)


def _cf_norm_bwd(du2, u1, gam3, bet3, l, name):
    S, DC = u1.shape
    tr = _row_tile(S)

    def body(d_ref, u_ref, g_ref, b_ref, du_ref, dg_ref, db_ref):
        @pl.when(pl.program_id(0) == 0)
        def _():
            dg_ref[...] = jnp.zeros_like(dg_ref)
            db_ref[...] = jnp.zeros_like(db_ref)

        xhat, rstd = _layer_norm_hat(u_ref[...])
        s = xhat * g_ref[...] + b_ref[...]
        sg = _sigmoid(s)
        ds = d_ref[...] * (sg * (1.0 + s * (1.0 - sg)))
        dg_ref[...] += jnp.sum(ds * xhat, axis=0, keepdims=True)
        db_ref[...] += jnp.sum(ds, axis=0, keepdims=True)
        dxh = ds * g_ref[...]
        du_ref[...] = rstd * (dxh - jnp.mean(dxh, axis=-1, keepdims=True)
                              - xhat * jnp.mean(dxh * xhat, axis=-1, keepdims=True))

    row = pl.BlockSpec((tr, DC), lambda i: (i, 0))
    vec = pl.BlockSpec((None, 1, DC), lambda i: (l, 0, 0))
    acc = pl.BlockSpec((1, DC), lambda i: (0, 0))
    return pl.pallas_call(
        body, name=name, grid=(S // tr,),
        in_specs=[row, row, vec, vec], out_specs=[row, acc, acc],
        out_shape=[jax.ShapeDtypeStruct((S, DC), F32), jax.ShapeDtypeStruct((1, DC), F32),
                   jax.ShapeDtypeStruct((1, DC), F32)],
        compiler_params=_params("arbitrary"),
    )(du2, u1, gam3, bet3)


def _cf_conv_bwd(du1, proj, conv_w, l, offs, DC, name):
    S = proj.shape[0]
    K = conv_w.shape[1]
    rows = min(CONV_ROWS, S)

    def body(d_ref, a_ref, g_ref, w_ref, da_ref, dgl_ref, dw_ref, dbias_ref, upad_ref, dpad_ref):
        _zero_pads(upad_ref, S)
        _zero_pads(dpad_ref, S)
        upad_ref[pl.ds(CONV_PAD, S), :] = a_ref[...].astype(F32) * _sigmoid(g_ref[...].astype(F32))
        dpad_ref[pl.ds(CONV_PAD, S), :] = d_ref[...]
        dbias_ref[...] = jnp.sum(d_ref[...], axis=0, keepdims=True)
        for r0 in range(0, S, rows):
            sl = pl.ds(r0, rows)
            du0 = _conv_bwd_chunk(dpad_ref, w_ref, K, r0, rows)
            a = a_ref[sl, :].astype(F32)
            sg = _sigmoid(g_ref[sl, :].astype(F32))
            da_ref[sl, :] = (du0 * sg).astype(BF16)
            dgl_ref[sl, :] = (du0 * a * sg * (1.0 - sg)).astype(BF16)
        _conv_dw(upad_ref, dpad_ref, CONV_PAD, K, S, dw_ref)

    blk = pl.BlockSpec((S, LANE), lambda j: (0, j))
    act = jax.ShapeDtypeStruct((S, DC), BF16)
    return pl.pallas_call(
        body, name=name, grid=(DC // LANE,),
        in_specs=[blk, _col_spec(S, offs["cf_a"]), _col_spec(S, offs["cf_g"]),
                  pl.BlockSpec((None, K, LANE), lambda j: (l, 0, j))],
        out_specs=[blk, blk, pl.BlockSpec((K, LANE), lambda j: (0, j)), pl.BlockSpec((1, LANE), lambda j: (0, j))],
        out_shape=[act, act, jax.ShapeDtypeStruct((K, DC), F32), jax.ShapeDtypeStruct((1, DC), F32)],
        scratch_shapes=[pltpu.VMEM((S + 2 * CONV_PAD, LANE), F32), pltpu.VMEM((S + 2 * CONV_PAD, LANE), F32)],
        compiler_params=_params("parallel"),
    )(du1, proj, proj, conv_w)


def _dot_nt(a, b):
    return lax.dot_general(a, b, (((1,), (1,)), ((), ())), preferred_element_type=F32)


def _dot_nn(a, b):
    return lax.dot_general(a, b, (((1,), (0,)), ((), ())), preferred_element_type=F32)


def _dot_tn(a, b):
    return lax.dot_general(a, b, (((0,), (0,)), ((), ())), preferred_element_type=F32)


def _dot_split(x, u):
    hi = x.astype(BF16)
    lo = (x - hi.astype(F32)).astype(BF16)
    return _dot_nn(hi, u) + _dot_nn(lo, u)


def _sb_scores(q, k, valid):
    z = _dot_nt(q, k) * (HEAD_DIM ** -0.5)
    lf = -(jnp.maximum(z, 0.0) + jnp.log(1.0 + jnp.exp(-jnp.abs(z))))
    return z, jnp.where(valid, lf, 0.0)


def _attn_fwd(proj, offs, DA, name):
    S = proj.shape[0]
    H = DA // HEAD_DIM
    nb = S // QB

    def body(q_ref, k_ref, v_ref, o_ref, tot_ref):
        row = lax.broadcasted_iota(jnp.int32, (QB, QB), 0)
        col = lax.broadcasted_iota(jnp.int32, (QB, QB), 1)
        u_after = (row > col).astype(BF16)

        def q_block(i, _):
            qs = pl.ds(pl.multiple_of(i * QB, QB), QB)
            q = q_ref[qs, :]

            def k_block(jj, carry):
                acc, c = carry
                ks = pl.ds(pl.multiple_of((i - jj) * QB, QB), QB)
                valid = col < row + jj * QB
                z, lf = _sb_scores(q, k_ref[ks, :], valid)
                log_a = lf + z + _dot_split(lf, u_after) + c
                a = jnp.where(valid, jnp.exp(log_a), 0.0)
                acc = acc + _dot_nn(a.astype(BF16), v_ref[ks, :])
                return acc, c + jnp.sum(lf, axis=1, keepdims=True)

            acc, c = lax.fori_loop(0, i + 1, k_block, (jnp.zeros((QB, HEAD_DIM), F32), jnp.zeros((QB, 1), F32)))
            o_ref[qs, :] = acc.astype(BF16)
            tot_ref[qs, :] = c
            return 0

        lax.fori_loop(0, nb, q_block, 0)

    def hs(off):
        return pl.BlockSpec((S, HEAD_DIM), lambda h: (0, off // HEAD_DIM + h))

    return pl.pallas_call(
        body, name=name, grid=(H,),
        in_specs=[hs(offs["q"]), hs(offs["k"]), hs(offs["v"])],
        out_specs=[pl.BlockSpec((S, HEAD_DIM), lambda h: (0, h)), pl.BlockSpec((None, S, 1), lambda h: (h, 0, 0))],
        out_shape=[jax.ShapeDtypeStruct((S, DA), BF16), jax.ShapeDtypeStruct((H, S, 1), F32)],
        compiler_params=_params("parallel"),
    )(proj, proj, proj)


def _attn_bwd(dout, tot, proj, offs, DA, name):
    S = proj.shape[0]
    H = DA // HEAD_DIM
    nb = S // QB
    scale = HEAD_DIM ** -0.5

    def body(q_ref, k_ref, v_ref, tot_ref, do_ref, dq_ref, dk_ref, dv_ref, dk_acc, dv_acc):
        row = lax.broadcasted_iota(jnp.int32, (QB, QB), 0)
        col = lax.broadcasted_iota(jnp.int32, (QB, QB), 1)
        u_after = (row > col).astype(BF16)
        u_before = (row < col).astype(BF16)
        dk_acc[...] = jnp.zeros_like(dk_acc)
        dv_acc[...] = jnp.zeros_like(dv_acc)

        def q_block(i, _):
            qs = pl.ds(pl.multiple_of(i * QB, QB), QB)
            q = q_ref[qs, :]
            do = do_ref[qs, :]
            tot_q = tot_ref[qs, :]

            def k_block(j, carry):
                dq, seen, gsum = carry
                ks = pl.ds(pl.multiple_of(j * QB, QB), QB)
                k = k_ref[ks, :]
                v = v_ref[ks, :]
                valid = col < row + (i - j) * QB
                z, lf = _sb_scores(q, k, valid)
                seen = seen + jnp.sum(lf, axis=1, keepdims=True)
                log_a = lf + z + _dot_split(lf, u_after) + (tot_q - seen)
                a = jnp.where(valid, jnp.exp(log_a), 0.0)
                g = a * _dot_nt(do, v)
                before = gsum + _dot_split(g, u_before)
                dz = jnp.where(valid, g * jnp.exp(lf) - before * jnp.exp(lf + z), 0.0) * scale
                dzb = dz.astype(BF16)
                dk_acc[ks, :] += _dot_tn(dzb, q)
                dv_acc[ks, :] += _dot_tn(a.astype(BF16), do)
                return dq + _dot_nn(dzb, k), seen, gsum + jnp.sum(g, axis=1, keepdims=True)

            zero = jnp.zeros((QB, 1), F32)
            dq, _, _ = lax.fori_loop(0, i + 1, k_block, (jnp.zeros((QB, HEAD_DIM), F32), zero, zero))
            dq_ref[qs, :] = dq.astype(BF16)
            return 0

        lax.fori_loop(0, nb, q_block, 0)
        dk_ref[...] = dk_acc[...].astype(BF16)
        dv_ref[...] = dv_acc[...].astype(BF16)

    def hs(off):
        return pl.BlockSpec((S, HEAD_DIM), lambda h: (0, off // HEAD_DIM + h))

    head = pl.BlockSpec((S, HEAD_DIM), lambda h: (0, h))
    act = jax.ShapeDtypeStruct((S, DA), BF16)
    return pl.pallas_call(
        body, name=name, grid=(H,),
        in_specs=[hs(offs["q"]), hs(offs["k"]), hs(offs["v"]), pl.BlockSpec((None, S, 1), lambda h: (h, 0, 0)), head],
        out_specs=[head, head, head], out_shape=[act, act, act],
        scratch_shapes=[pltpu.VMEM((S, HEAD_DIM), F32), pltpu.VMEM((S, HEAD_DIM), F32)],
        compiler_params=_params("parallel"),
    )(proj, proj, proj, tot, dout)


def _merge_tiles(S, D, goff):
    tn = LANE
    for t in range(LANE, 513, LANE):
        if D % t == 0 and goff % t == 0:
            tn = t
    return _tile(S, 512, 8), tn


def _merge_fwd(ga, attn, u2, pa, pb, pc, l, proj, goff, name):
    S = ga.shape[0]
    D = pa.shape[-1]
    tm, tn = _merge_tiles(S, D, goff)

    def body(ga_ref, at_ref, u2_ref, pa_ref, pb_ref, pc_ref, la_ref, lb_ref, lc_ref, ya_ref, yb_ref, yc_ref, m_ref):
        ya = _dot_nn(ga_ref[...], pa_ref[...])
        yb = _dot_nn(at_ref[...], pb_ref[...])
        yc = _dot_nn(u2_ref[...], pc_ref[...])
        ya_ref[...] = ya.astype(BF16)
        yb_ref[...] = yb.astype(BF16)
        yc_ref[...] = yc.astype(BF16)
        m_ref[...] = (_sigmoid(la_ref[...].astype(F32)) * ya + _sigmoid(lb_ref[...].astype(F32)) * yb
                      + _sigmoid(lc_ref[...].astype(F32)) * yc).astype(BF16)

    def lhs(a):
        return pl.BlockSpec((tm, a.shape[1]), lambda i, j: (i, 0))

    def rhs(p):
        return pl.BlockSpec((None, p.shape[1], tn), lambda i, j: (l, 0, j))

    def gate(r):
        return pl.BlockSpec((tm, tn), lambda i, j: (i, (goff + r * D) // tn + j))

    out = pl.BlockSpec((tm, tn), lambda i, j: (i, j))
    act = jax.ShapeDtypeStruct((S, D), BF16)
    return pl.pallas_call(
        body, name=name, grid=(S // tm, D // tn),
        in_specs=[lhs(ga), lhs(attn), lhs(u2), rhs(pa), rhs(pb), rhs(pc), gate(0), gate(1), gate(2)],
        out_specs=[out, out, out, out], out_shape=[act, act, act, act],
        compiler_params=_params("parallel", "parallel"),
    )(ga, attn, u2, pa, pb, pc, proj, proj, proj)


def _merge_bwd(dm, ya, yb, yc, proj, goff, name):
    S, D = dm.shape
    tm, tn = _merge_tiles(S, D, goff)

    def body(dm_ref, ya_ref, yb_ref, yc_ref, la_ref, lb_ref, lc_ref, *o_refs):
        dmv = dm_ref[...]
        for y_ref, l_ref, dy_ref, dl_ref in zip((ya_ref, yb_ref, yc_ref), (la_ref, lb_ref, lc_ref),
                                                o_refs[:3], o_refs[3:]):
            sg = _sigmoid(l_ref[...].astype(F32))
            dy_ref[...] = (dmv * sg).astype(BF16)
            dl_ref[...] = (dmv * y_ref[...].astype(F32) * sg * (1.0 - sg)).astype(BF16)

    def gate(r):
        return pl.BlockSpec((tm, tn), lambda i, j: (i, (goff + r * D) // tn + j))

    blk = pl.BlockSpec((tm, tn), lambda i, j: (i, j))
    act = jax.ShapeDtypeStruct((S, D), BF16)
    return pl.pallas_call(
        body, name=name, grid=(S // tm, D // tn),
        in_specs=[blk, blk, blk, blk, gate(0), gate(1), gate(2)],
        out_specs=[blk] * 6, out_shape=[act] * 6,
        compiler_params=_params("parallel", "parallel"),
    )(dm, ya, yb, yc, proj, proj, proj)


def _ew_tiles(rows, cols):
    tc = cols if cols <= 4096 else _tile(cols, 2048)
    tr = _tile(rows, max(8, (1 << 19) // tc), 8)
    return tr, tc


def _add2(a, b, out_dtype, name):
    shape = a.shape
    a2, b2 = a.reshape(-1, shape[-1]), b.reshape(-1, shape[-1])
    rows, cols = a2.shape
    tr, tc = _ew_tiles(rows, cols)

    def body(a_ref, b_ref, o_ref):
        o_ref[...] = (a_ref[...].astype(F32) + b_ref[...].astype(F32)).astype(out_dtype)

    blk = pl.BlockSpec((tr, tc), lambda i, j: (i, j))
    out = pl.pallas_call(
        body, name=name, grid=(rows // tr, cols // tc), in_specs=[blk, blk], out_specs=blk,
        out_shape=jax.ShapeDtypeStruct((rows, cols), out_dtype),
        compiler_params=_params("parallel", "parallel"),
    )(a2, b2)
    return out.reshape(shape)


def _sum_own_and_received(own, recv, name):
    shape = own.shape
    n = recv.shape[0]
    o2, r2 = own.reshape(-1, shape[-1]), recv.reshape(n, -1, shape[-1])
    rows, cols = o2.shape
    tr, tc = _ew_tiles(rows, cols)

    def body(o_ref, r_ref, out_ref):
        acc = o_ref[...].astype(F32)
        for s in range(n):
            acc = acc + r_ref[s].astype(F32)
        out_ref[...] = acc

    out = pl.pallas_call(
        body, name=name, grid=(rows // tr, cols // tc),
        in_specs=[pl.BlockSpec((tr, tc), lambda i, j: (i, j)), pl.BlockSpec((n, tr, tc), lambda i, j: (0, i, j))],
        out_specs=pl.BlockSpec((tr, tc), lambda i, j: (i, j)),
        out_shape=jax.ShapeDtypeStruct((rows, cols), F32),
        compiler_params=_params("parallel", "parallel"),
    )(o2, r2)
    return out.reshape(shape)


def _adamw(w, g, m, v, name):
    shape = w.shape
    args = [t.reshape(-1, shape[-1]) for t in (w, g, m, v)]
    rows, cols = args[0].shape
    tr, tc = _ew_tiles(rows, cols)
    c1 = 1.0 - ADAM_B1 ** ADAM_STEP
    c2 = 1.0 - ADAM_B2 ** ADAM_STEP

    def body(w_ref, g_ref, m_ref, v_ref, d_ref, nm_ref, nv_ref):
        gv = g_ref[...]
        nm = ADAM_B1 * m_ref[...] + (1.0 - ADAM_B1) * gv
        nv = ADAM_B2 * v_ref[...] + (1.0 - ADAM_B2) * (gv * gv)
        nm_ref[...] = nm
        nv_ref[...] = nv
        d_ref[...] = -ADAM_LR * ((nm / c1) / (jnp.sqrt(nv / c2) + ADAM_EPS) + ADAM_WD * w_ref[...])

    blk = pl.BlockSpec((tr, tc), lambda i, j: (i, j))
    shp = jax.ShapeDtypeStruct((rows, cols), F32)
    outs = pl.pallas_call(
        body, name=name, grid=(rows // tr, cols // tc), in_specs=[blk] * 4, out_specs=[blk] * 3,
        out_shape=[shp] * 3, compiler_params=_params("parallel", "parallel"),
    )(*args)
    return [o.reshape(shape) for o in outs]


ANY = pl.BlockSpec(memory_space=pl.ANY)


def _place():
    x, y, c = lax.axis_index("x"), lax.axis_index("y"), lax.axis_index("c")
    chips = [(1 - x, y), (x, 1 - y), (1 - x, 1 - y)]
    return x, y, c, chips


def _al(v, unit):
    return pl.multiple_of(v, unit) if unit % LANE == 0 else v


def _half_of_full(ref, kind, p, half, Rs, Cs):
    Rh = (ref.shape[-2] // 2) if kind == "col" else Rs // 2
    lead = (slice(None),) * (len(ref.shape) - 2)
    if kind == "col":
        return ref.at[lead + (pl.ds(_al(half * Rh, Rh), Rh), pl.ds(_al(p * Cs, Cs), Cs))]
    return ref.at[lead + (pl.ds(_al(p * Rs + half * Rh, Rh), Rh), slice(None))]


def _gather_weights(shards, kinds):
    n = len(shards)
    fulls = []
    for s, kind in zip(shards, kinds):
        L, Rs, Cs = s.shape
        fulls.append(jax.ShapeDtypeStruct((L, Rs, 4 * Cs) if kind == "col" else (L, 4 * Rs, Cs), s.dtype))

    def body(*refs):
        ins, outs = refs[:n], refs[n:2 * n]
        ici_send, ici_recv, d2d_send, d2d_recv, own_sem = refs[2 * n:]
        x, y, c, chips = _place()
        me = 2 * x + y
        sib = (x, y, 1 - c)

        def piece(w, p, half):
            _, Rs, Cs = ins[w].shape
            return _half_of_full(outs[w], kinds[w], p, half, Rs, Cs)

        def my_half(w):
            Rh = ins[w].shape[1] // 2
            return ins[w].at[:, pl.ds(_al(c * Rh, Rh), Rh), :]

        own = []
        for w in range(n):
            for half in range(2):
                Rh = ins[w].shape[1] // 2
                cp = pltpu.make_async_copy(ins[w].at[:, pl.ds(half * Rh, Rh), :], piece(w, me, half),
                                           own_sem.at[2 * w + half])
                cp.start()
                own.append(cp)
        sends = []
        for j, chip in enumerate(chips):
            for w in range(n):
                cp = pltpu.make_async_remote_copy(my_half(w), piece(w, me, c), ici_send.at[j * n + w],
                                                  ici_recv.at[j * n + w], device_id=(*chip, c), device_id_type=MESH)
                cp.start()
                sends.append(cp)
        for j, (px, py) in enumerate(chips):
            for w in range(n):
                got = piece(w, 2 * px + py, c)
                pltpu.make_async_remote_copy(got, got, ici_send.at[j * n + w], ici_recv.at[j * n + w],
                                             device_id=(px, py, c), device_id_type=MESH).wait_recv()
                cp = pltpu.make_async_remote_copy(got, got, d2d_send.at[j * n + w], d2d_recv.at[j * n + w],
                                                  device_id=sib, device_id_type=MESH)
                cp.start()
                sends.append(cp)
        for j, (px, py) in enumerate(chips):
            for w in range(n):
                got = piece(w, 2 * px + py, 1 - c)
                pltpu.make_async_remote_copy(got, got, d2d_send.at[j * n + w], d2d_recv.at[j * n + w],
                                             device_id=sib, device_id_type=MESH).wait_recv()
        for cp in sends:
            cp.wait_send()
        for cp in own:
            cp.wait()

    return pl.pallas_call(
        body, name="gather_weights", in_specs=[ANY] * n, out_specs=[ANY] * n, out_shape=fulls,
        scratch_shapes=[pltpu.SemaphoreType.DMA((3 * n,))] * 4 + [pltpu.SemaphoreType.DMA((2 * n,))],
    )(*shards)


def _reduce_to_sibling_halves(grads, kinds):
    n = len(grads)
    L = len(grads[0])
    shapes = []
    for g, kind in zip(grads, kinds):
        R, C = g[0].shape
        shapes.append((L, 4, R // 2, C // 4) if kind == "col" else (L, 4, R // 8, C))
    flat = [g[l] for g in grads for l in range(L)]
    out_shape = [jax.ShapeDtypeStruct(s, BF16) for s in shapes] * 2

    def body(*refs):
        ins = refs[:n * L]
        owns, gots = refs[n * L:n * L + n], refs[n * L + n:n * L + 2 * n]
        own_sem, send_sem, recv_sem = refs[n * L + 2 * n:]
        x, y, c, _ = _place()
        sib = (x, y, 1 - c)
        for w in range(n):
            _, _, Rh, Cs = shapes[w]
            Rs = 2 * Rh
            for l in range(L):
                for p in range(4):
                    g = ins[w * L + l]
                    pltpu.make_async_copy(_half_of_full(g, kinds[w], p, c, Rs, Cs), owns[w].at[l, p],
                                          own_sem.at[w]).start()
                    pltpu.make_async_remote_copy(_half_of_full(g, kinds[w], p, 1 - c, Rs, Cs), gots[w].at[l, p],
                                                 send_sem.at[w], recv_sem.at[w], device_id=sib,
                                                 device_id_type=MESH).start()
        for w in range(n):
            pltpu.make_async_copy(owns[w], owns[w], own_sem.at[w]).wait()
            pltpu.make_async_remote_copy(gots[w], gots[w], send_sem.at[w], recv_sem.at[w], device_id=sib,
                                         device_id_type=MESH).wait()

    outs = pl.pallas_call(
        body, name="reduce_d2d", in_specs=[ANY] * (n * L), out_specs=[ANY] * (2 * n), out_shape=out_shape,
        scratch_shapes=[pltpu.SemaphoreType.DMA((n,))] * 3,
    )(*flat)
    return outs[:n], outs[n:]


def _exchange_over_ici(parts):
    n = len(parts)
    out_shape = ([jax.ShapeDtypeStruct((p.shape[0],) + p.shape[2:], p.dtype) for p in parts]
                 + [jax.ShapeDtypeStruct((3, p.shape[0]) + p.shape[2:], p.dtype) for p in parts])

    def body(*refs):
        ins, owns, gots = refs[:n], refs[n:2 * n], refs[2 * n:3 * n]
        own_sem, send_sem, recv_sem = refs[3 * n:]
        x, y, c, chips = _place()
        me = 2 * x + y
        copies = []
        for w in range(n):
            cp = pltpu.make_async_copy(ins[w].at[:, me], owns[w], own_sem.at[w])
            cp.start()
            copies.append(cp)
        for j, (px, py) in enumerate(chips):
            for w in range(n):
                cp = pltpu.make_async_remote_copy(ins[w].at[:, 2 * px + py], gots[w].at[j], send_sem.at[j * n + w],
                                                  recv_sem.at[j * n + w], device_id=(px, py, c), device_id_type=MESH)
                cp.start()
                copies.append(cp)
        for cp in copies:
            cp.wait()

    outs = pl.pallas_call(
        body, name="reduce_ici", in_specs=[ANY] * n, out_specs=[ANY] * (2 * n), out_shape=out_shape,
        scratch_shapes=[pltpu.SemaphoreType.DMA((n,)), pltpu.SemaphoreType.DMA((3 * n,)),
                        pltpu.SemaphoreType.DMA((3 * n,))],
    )(*parts)
    return outs[:n], outs[n:]


def _share_with_sibling(halves):
    n = len(halves)
    out_shape = [jax.ShapeDtypeStruct((h.shape[0], 2 * h.shape[1], h.shape[2]), h.dtype) for h in halves]

    def body(*refs):
        ins, outs = refs[:n], refs[n:2 * n]
        own_sem, send_sem, recv_sem = refs[2 * n:]
        x, y, c, _ = _place()
        sib = (x, y, 1 - c)
        copies = []
        for w in range(n):
            Rh = ins[w].shape[1]
            mine = outs[w].at[:, pl.ds(_al(c * Rh, Rh), Rh), :]
            cp = pltpu.make_async_copy(ins[w], mine, own_sem.at[w])
            cp.start()
            copies.append(cp)
            cp = pltpu.make_async_remote_copy(ins[w], mine, send_sem.at[w], recv_sem.at[w], device_id=sib,
                                              device_id_type=MESH)
            cp.start()
            copies.append(cp)
        for cp in copies:
            cp.wait()

    return pl.pallas_call(
        body, name="reduce_share", in_specs=[ANY] * n, out_specs=[ANY] * n, out_shape=out_shape,
        scratch_shapes=[pltpu.SemaphoreType.DMA((n,))] * 3,
    )(*halves)


def _all_gather_small(v):
    r = v.shape[0]

    def body(v_ref, o_ref, send_sem, recv_sem):
        x, y, c, _ = _place()
        me = 4 * x + 2 * y + c
        o_ref[me] = v_ref[...]
        copies = []
        for k in range(1, 8):
            peer = (x ^ (k >> 2), y ^ ((k >> 1) & 1), c ^ (k & 1))
            cp = pltpu.make_async_remote_copy(v_ref, o_ref.at[me], send_sem.at[k - 1], recv_sem.at[k - 1],
                                              device_id=peer, device_id_type=MESH)
            cp.start()
            copies.append(cp)
        for cp in copies:
            cp.wait()

    vmem = pl.BlockSpec(memory_space=pltpu.VMEM)
    return pl.pallas_call(
        body, name="all_gather_small", in_specs=[vmem], out_specs=vmem,
        out_shape=jax.ShapeDtypeStruct((8, r, LANE), F32),
        scratch_shapes=[pltpu.SemaphoreType.DMA((7,)), pltpu.SemaphoreType.DMA((7,))],
        compiler_params=pltpu.CompilerParams(vmem_limit_bytes=VMEM_LIMIT),
    )(v)


def _sum_slots(g):
    n, r, _ = g.shape

    def body(g_ref, o_ref):
        acc = g_ref[0]
        for s in range(1, n):
            acc = acc + g_ref[s]
        o_ref[...] = acc

    vmem = pl.BlockSpec(memory_space=pltpu.VMEM)
    return pl.pallas_call(
        body, name="sum_slots", in_specs=[vmem], out_specs=vmem, out_shape=jax.ShapeDtypeStruct((r, LANE), F32),
        compiler_params=pltpu.CompilerParams(vmem_limit_bytes=VMEM_LIMIT),
    )(g)


def _pack(arrays):
    flat = jnp.concatenate([a.reshape(-1) for a in arrays])
    pad = (-flat.shape[0]) % (8 * LANE)
    return jnp.pad(flat, (0, pad)).reshape(-1, LANE)


def _unpack(packed, like):
    flat = packed.reshape(-1)
    out, off = [], 0
    for a in like:
        out.append(flat[off:off + a.size].reshape(a.shape))
        off += a.size
    return out


def _offsets(D):
    DA, DS, DC = D // 2, D // 4, D // 4
    names = ["q", "k", "v", "sc_b", "sc_c", "sc_u", "cf_a", "cf_g", "gate"]
    sizes = [DA, DA, DA, DS, DS, DS, DC, DC, 3 * D]
    offs, o = {}, 0
    for nm, sz in zip(names, sizes):
        offs[nm] = o
        o += sz
    return offs, DA, DS, DC


def _relu2_epilogue(acc):
    r = jnp.maximum(acc, 0.0)
    return acc, r * r


def _drelu2_epilogue(acc, up):
    return (acc * (2.0 * jnp.maximum(up.astype(F32), 0.0)),)


def _local_step(x, target, gains, conv_a_w, conv_c_w, conv_c_b, norm_c_g, norm_c_b, big):
    S, D = x.shape
    L = gains[0].shape[0]
    offs, DA, DS, DC = _offsets(D)
    goff = offs["gate"]
    g_mix_pre, g_mix_post, g_mlp_pre, g_mlp_post = gains
    cb3, ng3, nb3 = (t.reshape(L, 1, DC) for t in (conv_c_b, norm_c_g, norm_c_b))

    saved = []
    h = _rms_fwd(x, g_mix_pre, 0, "rms_first")
    xin = x
    for l in range(L):
        proj = _mm(h, big["w_in"], b_layer=l, out_dtypes=(BF16,), name=f"fwd_w_in_{l}", tn_cap=512)
        ga = _sc_fwd(proj, conv_a_w, l, offs, DS, f"sc_fwd_{l}")
        attn, attn_tot = _attn_fwd(proj, offs, DA, f"attn_fwd_{l}")
        u1 = _cf_conv_fwd(proj, conv_c_w, cb3, l, offs, DC, f"cf_conv_fwd_{l}")
        u2 = _cf_norm_fwd(u1, ng3, nb3, l, f"cf_norm_fwd_{l}")
        ya, yb, yc, merged = _merge_fwd(ga, attn, u2, big["proj_a"], big["proj_b"], big["proj_c"], l, proj, goff,
                                        f"merge_fwd_{l}")
        mixed = _mm(merged, big["w_o"], b_layer=l, name=f"fwd_w_o_{l}")
        x1, h2 = _post_res_fwd(xin, mixed, g_mix_post, l, g_mlp_pre, l, f"mix_residual_{l}")
        up, act = _mm(h2, big["w_up"], b_layer=l, out_dtypes=(BF16, BF16), epilogue=_relu2_epilogue,
                      name=f"fwd_w_up_{l}")
        f = _mm(act, big["w_down"], b_layer=l, name=f"fwd_w_down_{l}")
        saved.append(dict(xin=xin, h=h, proj=proj, ga=ga, attn=attn, attn_tot=attn_tot, u1=u1, u2=u2, ya=ya, yb=yb, yc=yc,
                          merged=merged, mixed=mixed, x1=x1, h2=h2, up=up, act=act, f=f))
        if l + 1 < L:
            xin, h = _post_res_fwd(x1, f, g_mlp_post, l, g_mix_pre, l + 1, f"mlp_residual_{l}")
        else:
            dx, loss = _final_fwd_loss(x1, f, g_mlp_post, l, target, "loss_head")

    small = {k: [None] * L for k in ("mix_pre", "mix_post", "mlp_pre", "mlp_post", "conv_a_w", "conv_c_w",
                                       "conv_c_b", "norm_c_g", "norm_c_b")}
    bigg = {k: [None] * L for k in ("w_in", "proj_a", "proj_b", "proj_c", "w_o", "w_up", "w_down")}
    for l in reversed(range(L)):
        s = saved[l]
        df, small["mlp_post"][l] = _post_bwd(dx, s["f"], g_mlp_post, l, f"mlp_post_bwd_{l}")
        bigg["w_down"][l] = _mm(s["act"], df, ta=True, out_dtypes=(BF16,), name=f"dw_down_{l}")
        dup = _mm(df, big["w_down"], tb=True, b_layer=l, out_dtypes=(BF16,), epilogue=_drelu2_epilogue,
                  extras=(s["up"],), name=f"d_up_{l}")
        bigg["w_up"][l] = _mm(s["h2"], dup, ta=True, out_dtypes=(BF16,), name=f"dw_up_{l}")
        dh2 = _mm(dup, big["w_up"], tb=True, b_layer=l, name=f"d_h2_{l}")
        dx1, small["mlp_pre"][l] = _pre_bwd(dx, dh2, s["x1"], g_mlp_pre, l, f"mlp_pre_bwd_{l}")
        dmixed, small["mix_post"][l] = _post_bwd(dx1, s["mixed"], g_mix_post, l, f"mix_post_bwd_{l}")
        bigg["w_o"][l] = _mm(s["merged"], dmixed, ta=True, out_dtypes=(BF16,), name=f"dw_o_{l}")
        dmerged = _mm(dmixed, big["w_o"], tb=True, b_layer=l, name=f"d_merged_{l}")
        dya, dyb, dyc, dla, dlb, dlc = _merge_bwd(dmerged, s["ya"], s["yb"], s["yc"], s["proj"], goff,
                                                  f"merge_bwd_{l}")
        bigg["proj_a"][l] = _mm(s["ga"], dya, ta=True, out_dtypes=(BF16,), name=f"dproj_a_{l}")
        bigg["proj_b"][l] = _mm(s["attn"], dyb, ta=True, out_dtypes=(BF16,), name=f"dproj_b_{l}")
        bigg["proj_c"][l] = _mm(s["u2"], dyc, ta=True, out_dtypes=(BF16,), name=f"dproj_c_{l}")
        dga = _mm(dya, big["proj_a"], tb=True, b_layer=l, name=f"d_ga_{l}")
        dattn = _mm(dyb, big["proj_b"], tb=True, b_layer=l, out_dtypes=(BF16,), name=f"d_attn_{l}")
        du2 = _mm(dyc, big["proj_c"], tb=True, b_layer=l, name=f"d_u2_{l}")
        dsb, dsc, dsu, small["conv_a_w"][l] = _sc_bwd(dga, s["proj"], conv_a_w, l, offs, DS, f"sc_bwd_{l}")
        du1, small["norm_c_g"][l], small["norm_c_b"][l] = _cf_norm_bwd(du2, s["u1"], ng3, nb3, l, f"cf_norm_bwd_{l}")
        dca, dcg, small["conv_c_w"][l], small["conv_c_b"][l] = _cf_conv_bwd(du1, s["proj"], conv_c_w, l, offs, DC,
                                                                          f"cf_conv_bwd_{l}")
        dq, dk, dv = _attn_bwd(dattn, s["attn_tot"], s["proj"], offs, DA, f"attn_bwd_{l}")
        dproj = jnp.concatenate([dq, dk, dv, dsb, dsc, dsu, dca, dcg, dla, dlb, dlc], axis=1)
        bigg["w_in"][l] = _mm(s["h"], dproj, ta=True, out_dtypes=(BF16,), name=f"dw_in_{l}", tn_cap=512)
        dh = _mm(dproj, big["w_in"], tb=True, b_layer=l, name=f"d_h_{l}")
        dx, small["mix_pre"][l] = _pre_bwd(dx1, dh, s["xin"], g_mix_pre, l, f"mix_pre_bwd_{l}")
    return loss, dx, small, bigg


BIG = ("w_in", "proj_a", "proj_b", "proj_c", "w_o", "w_up", "w_down")
BIG_KIND = {"w_in": "col", "proj_a": "col", "proj_b": "col", "proj_c": "col", "w_o": "row", "w_up": "col",
            "w_down": "row"}


def kernel(x, ln_mix_pre, ln_mix_post, ln_mlp_pre, ln_mlp_post, w_in, conv_a_w, proj_a, proj_b, conv_c_w, conv_c_b, norm_c_g, norm_c_b, proj_c, w_o, w_up, w_down, loss_target, m_ln_mix_pre, m_ln_mix_post, m_ln_mlp_pre, m_ln_mlp_post, m_w_in, m_conv_a_w, m_proj_a, m_proj_b, m_conv_c_w, m_conv_c_b, m_norm_c_g, m_norm_c_b, m_proj_c, m_w_o, m_w_up, m_w_down, v_ln_mix_pre, v_ln_mix_post, v_ln_mlp_pre, v_ln_mlp_post, v_w_in, v_conv_a_w, v_proj_a, v_proj_b, v_conv_c_w, v_conv_c_b, v_norm_c_g, v_norm_c_b, v_proj_c, v_w_o, v_w_up, v_w_down):
    weights = dict(ln_mix_pre=ln_mix_pre, ln_mix_post=ln_mix_post, ln_mlp_pre=ln_mlp_pre, ln_mlp_post=ln_mlp_post,
                   w_in=w_in, conv_a_w=conv_a_w, proj_a=proj_a, proj_b=proj_b, conv_c_w=conv_c_w, conv_c_b=conv_c_b,
                   norm_c_g=norm_c_g, norm_c_b=norm_c_b, proj_c=proj_c, w_o=w_o, w_up=w_up, w_down=w_down)
    m_in = dict(ln_mix_pre=m_ln_mix_pre, ln_mix_post=m_ln_mix_post, ln_mlp_pre=m_ln_mlp_pre, ln_mlp_post=m_ln_mlp_post,
                w_in=m_w_in, conv_a_w=m_conv_a_w, proj_a=m_proj_a, proj_b=m_proj_b, conv_c_w=m_conv_c_w,
                conv_c_b=m_conv_c_b, norm_c_g=m_norm_c_g, norm_c_b=m_norm_c_b, proj_c=m_proj_c, w_o=m_w_o,
                w_up=m_w_up, w_down=m_w_down)
    v_in = dict(ln_mix_pre=v_ln_mix_pre, ln_mix_post=v_ln_mix_post, ln_mlp_pre=v_ln_mlp_pre, ln_mlp_post=v_ln_mlp_post,
                w_in=v_w_in, conv_a_w=v_conv_a_w, proj_a=v_proj_a, proj_b=v_proj_b, conv_c_w=v_conv_c_w,
                conv_c_b=v_conv_c_b, norm_c_g=v_norm_c_g, norm_c_b=v_norm_c_b, proj_c=v_proj_c, w_o=v_w_o,
                w_up=v_w_up, w_down=v_w_down)
    order = list(weights)
    L, D = ln_mix_pre.shape
    chip = 2 * lax.axis_index("x") + lax.axis_index("y")

    kinds = [BIG_KIND[k] for k in BIG]
    full = dict(zip(BIG, _gather_weights([weights[k].astype(BF16) for k in BIG], kinds)))
    conv_local = [conv_a_w, conv_c_w]
    slots = _all_gather_small(_pack(conv_local))
    per_chip = [_unpack(slots[4 * px + 2 * py], conv_local) for px in range(2) for py in range(2)]
    conv_a_full = jnp.concatenate([pc[0] for pc in per_chip], axis=-1)
    conv_c_full = jnp.concatenate([pc[1] for pc in per_chip], axis=-1)

    gains = [weights[k].reshape(L, 1, D) for k in ("ln_mix_pre", "ln_mix_post", "ln_mlp_pre", "ln_mlp_post")]
    loss, dx, small, bigg = _local_step(x[0], loss_target[0], gains, conv_a_full, conv_c_full, conv_c_b, norm_c_g,
                                        norm_c_b, full)

    glist = [bigg[k] for k in BIG]
    own1, got1 = _reduce_to_sibling_halves(glist, kinds)
    pair = [_add2(a, b, BF16, f"pair_sum_{k}") for k, a, b in zip(BIG, own1, got1)]
    own2, got2 = _exchange_over_ici(pair)
    halves = [_sum_own_and_received(a, b, f"chip_sum_{k}") for k, a, b in zip(BIG, own2, got2)]
    grads = dict(zip(BIG, _share_with_sibling(halves)))

    small_names = ["ln_mix_pre", "ln_mix_post", "ln_mlp_pre", "ln_mlp_post", "conv_a_w", "conv_c_w", "conv_c_b",
                   "norm_c_g", "norm_c_b"]
    small_key = dict(ln_mix_pre="mix_pre", ln_mix_post="mix_post", ln_mlp_pre="mlp_pre", ln_mlp_post="mlp_post")
    small_local = []
    for k in small_names:
        per_layer = small[small_key.get(k, k)]
        stacked = jnp.stack(per_layer)
        small_local.append(stacked.reshape(L, -1) if stacked.shape[1] == 1 else stacked)
    small_sum = _unpack(_sum_slots(_all_gather_small(_pack(small_local))), small_local)
    for k, g in zip(small_names, small_sum):
        if k in ("conv_a_w", "conv_c_w"):
            width = weights[k].shape[-1]
            g = lax.dynamic_slice_in_dim(g, chip * width, width, axis=2)
        grads[k] = g

    delta, new_m, new_v = {}, {}, {}
    for k in BIG:
        delta[k], new_m[k], new_v[k] = _adamw(weights[k], grads[k], m_in[k], v_in[k], f"adamw_{k}")
    packed = [_pack([t[k] for k in small_names]) for t in (weights, grads, m_in, v_in)]
    like = [weights[k] for k in small_names]
    for dst, res in zip((delta, new_m, new_v), _adamw(*packed, "adamw_small")):
        dst.update(zip(small_names, _unpack(res, like)))

    total = lax.psum(loss[0, 0], ("x", "y", "c"))
    return (total, dx[None], *[grads[k] for k in order], *[delta[k] for k in order],
            *[new_m[k] for k in order], *[new_v[k] for k in order])
```

```python
import functools

import jax
import jax.numpy as jnp
from jax import lax
from jax.experimental import pallas as pl
from jax.experimental.pallas import tpu as pltpu

F32 = jnp.float32
BF16 = jnp.bfloat16
MESH = pl.DeviceIdType.MESH

HEAD_DIM = 128
QB = 128
RMS_EPS = 1e-6
LN_EPS = 1e-5
ADAM_LR = 0.001
ADAM_B1 = 0.9
ADAM_B2 = 0.999
ADAM_EPS = 1e-08
ADAM_WD = 0.01
ADAM_STEP = 10
LANE = 128
VMEM_LIMIT = 56 * 1024 * 1024
CONV_PAD = 32
CONV_ROWS = 256
ANY = pl.BlockSpec(memory_space=pl.ANY)


def _tile(n, cap, mult=LANE):
    best = None
    t = mult
    while t <= min(n, cap):
        if n % t == 0:
            best = t
        t += mult
    return best if best is not None else n


def _params(*sem):
    return pltpu.CompilerParams(dimension_semantics=sem if sem else None, vmem_limit_bytes=VMEM_LIMIT)


def _sigmoid(x):
    return 1.0 / (1.0 + jnp.exp(-x))


def _mm(a, b, *, name, ta=False, tb=False, b_layer=None, out_dtypes=(F32,), epilogue=None, extras=(),
        tm_cap=1024, tn_cap=1024, tk_cap=512, stack=None):
    if ta:
        K, M = a.shape
    else:
        M, K = a.shape
    N = b.shape[-2] if tb else b.shape[-1]
    tm, tn, tk = _tile(M, tm_cap), _tile(N, tn_cap), _tile(K, tk_cap)
    nk = K // tk
    a_spec = pl.BlockSpec((tk, tm), lambda i, j, k: (k, i)) if ta else pl.BlockSpec((tm, tk), lambda i, j, k: (i, k))
    if tb:
        b_blk, b_idx = (tn, tk), (lambda i, j, k: (j, k))
    else:
        b_blk, b_idx = (tk, tn), (lambda i, j, k: (k, j))
    if b_layer is None:
        b_spec = pl.BlockSpec(b_blk, b_idx)
    else:
        b_spec = pl.BlockSpec((None,) + b_blk, lambda i, j, k: (b_layer,) + b_idx(i, j, k))
    e_specs = [pl.BlockSpec((tm, tn), lambda i, j, k: (i, j)) for _ in extras]
    dims = (((0 if ta else 1,), (1 if tb else 0,)), ((), ()))
    n_e, n_o = len(extras), len(out_dtypes)

    n_p = 1 if stack is not None and stack[2] is not None else 0

    def body(a_ref, b_ref, *rest):
        e_refs, o_refs, acc_ref = rest[:n_e], rest[n_e + n_p:n_e + n_p + n_o], rest[n_e + n_p + n_o]
        k = pl.program_id(2)

        @pl.when(k == 0)
        def _():
            acc_ref[...] = jnp.zeros_like(acc_ref)

        acc_ref[...] += lax.dot_general(a_ref[...].astype(BF16), b_ref[...].astype(BF16), dims,
                                        preferred_element_type=F32)

        @pl.when(k == nk - 1)
        def _():
            acc = acc_ref[...]
            outs = (acc,) if epilogue is None else epilogue(acc, *[e[...] for e in e_refs])
            for o_ref, o in zip(o_refs, outs):
                o_ref[...] = o.astype(o_ref.dtype)

    if stack is None:
        out_specs = [pl.BlockSpec((tm, tn), lambda i, j, k: (i, j)) for _ in out_dtypes]
        out_shape = [jax.ShapeDtypeStruct((M, N), dt) for dt in out_dtypes]
        prev, aliases = (), {}
    else:
        layer, n_layers, buf = stack
        out_specs = [pl.BlockSpec((None, tm, tn), lambda i, j, k: (layer, i, j))]
        out_shape = [jax.ShapeDtypeStruct((n_layers, M, N), out_dtypes[0])]
        prev, aliases = ((buf,), {2 + n_e: 0}) if n_p else ((), {})
    outs = pl.pallas_call(
        body, name=name, grid=(M // tm, N // tn, nk),
        in_specs=[a_spec, b_spec] + e_specs + [ANY] * n_p,
        out_specs=out_specs, out_shape=out_shape, input_output_aliases=aliases,
        scratch_shapes=[pltpu.VMEM((tm, tn), F32)],
        compiler_params=_params("parallel", "parallel", "arbitrary"),
    )(a, b, *extras, *prev)
    return outs[0] if n_o == 1 else outs


def _row_tile(S):
    return _tile(S, 256, 8)


def _gain_spec(D, l):
    return pl.BlockSpec((None, 1, D), lambda i: (l, 0, 0))


def _rms(x, g):
    r = lax.rsqrt(jnp.mean(x * x, axis=-1, keepdims=True) + RMS_EPS)
    return x * r * g


def _rms_fwd(x, g3, l, name):
    S, D = x.shape
    tr = _row_tile(S)

    def body(x_ref, g_ref, h_ref):
        h_ref[...] = _rms(x_ref[...], g_ref[...]).astype(BF16)

    return pl.pallas_call(
        body, name=name, grid=(S // tr,),
        in_specs=[pl.BlockSpec((tr, D), lambda i: (i, 0)), _gain_spec(D, l)],
        out_specs=pl.BlockSpec((tr, D), lambda i: (i, 0)),
        out_shape=jax.ShapeDtypeStruct((S, D), BF16),
        compiler_params=_params("parallel"),
    )(x, g3)


def _post_res_fwd(x_in, f, gpost3, l, gnext3, lnext, name):
    S, D = x_in.shape
    tr = _row_tile(S)

    def body(x_ref, f_ref, gp_ref, gn_ref, xo_ref, h_ref):
        xo = x_ref[...] + _rms(f_ref[...], gp_ref[...])
        xo_ref[...] = xo
        h_ref[...] = _rms(xo, gn_ref[...]).astype(BF16)

    row = pl.BlockSpec((tr, D), lambda i: (i, 0))
    return pl.pallas_call(
        body, name=name, grid=(S // tr,),
        in_specs=[row, row, _gain_spec(D, l), _gain_spec(D, lnext)],
        out_specs=[row, row],
        out_shape=[jax.ShapeDtypeStruct((S, D), F32), jax.ShapeDtypeStruct((S, D), BF16)],
        compiler_params=_params("parallel"),
    )(x_in, f, gpost3, gnext3)


def _final_fwd_loss(x_in, f, gpost3, l, target, name):
    S, D = x_in.shape
    tr = _row_tile(S)

    def body(x_ref, f_ref, gp_ref, t_ref, dx_ref, loss_ref):
        @pl.when(pl.program_id(0) == 0)
        def _():
            loss_ref[...] = jnp.zeros_like(loss_ref)

        err = x_ref[...] + _rms(f_ref[...], gp_ref[...]) - t_ref[...]
        dx_ref[...] = err * (1.0 / D)
        loss_ref[...] += 0.5 * jnp.sum(jnp.mean(err * err, axis=-1, keepdims=True))

    row = pl.BlockSpec((tr, D), lambda i: (i, 0))
    return pl.pallas_call(
        body, name=name, grid=(S // tr,),
        in_specs=[row, row, _gain_spec(D, l), row],
        out_specs=[row, pl.BlockSpec((8, LANE), lambda i: (0, 0))],
        out_shape=[jax.ShapeDtypeStruct((S, D), F32), jax.ShapeDtypeStruct((8, LANE), F32)],
        compiler_params=_params("arbitrary"),
    )(x_in, f, gpost3, target)


def _rms_bwd_rows(dy, x, g):
    r = lax.rsqrt(jnp.mean(x * x, axis=-1, keepdims=True) + RMS_EPS)
    t = dy * g
    dx = r * t - x * (r * r * r) * jnp.mean(t * x, axis=-1, keepdims=True)
    return dx, dy * x * r


def _post_bwd(dxo, f, gpost3, l, name):
    S, D = f.shape
    tr = _row_tile(S)

    def body(d_ref, f_ref, g_ref, df_ref, dg_ref):
        @pl.when(pl.program_id(0) == 0)
        def _():
            dg_ref[...] = jnp.zeros_like(dg_ref)

        df, dg = _rms_bwd_rows(d_ref[...], f_ref[...], g_ref[...])
        df_ref[...] = df.astype(BF16)
        dg_ref[...] += jnp.sum(dg, axis=0, keepdims=True)

    row = pl.BlockSpec((tr, D), lambda i: (i, 0))
    return pl.pallas_call(
        body, name=name, grid=(S // tr,),
        in_specs=[row, row, _gain_spec(D, l)],
        out_specs=[row, pl.BlockSpec((1, D), lambda i: (0, 0))],
        out_shape=[jax.ShapeDtypeStruct((S, D), BF16), jax.ShapeDtypeStruct((1, D), F32)],
        compiler_params=_params("arbitrary"),
    )(dxo, f, gpost3)


def _pre_bwd(dxo, dh, x_in, gpre3, l, name):
    S, D = x_in.shape
    tr = _row_tile(S)

    def body(d_ref, dh_ref, x_ref, g_ref, dx_ref, dg_ref):
        @pl.when(pl.program_id(0) == 0)
        def _():
            dg_ref[...] = jnp.zeros_like(dg_ref)

        dx, dg = _rms_bwd_rows(dh_ref[...], x_ref[...], g_ref[...])
        dx_ref[...] = d_ref[...] + dx
        dg_ref[...] += jnp.sum(dg, axis=0, keepdims=True)

    row = pl.BlockSpec((tr, D), lambda i: (i, 0))
    return pl.pallas_call(
        body, name=name, grid=(S // tr,),
        in_specs=[row, row, row, _gain_spec(D, l)],
        out_specs=[row, pl.BlockSpec((1, D), lambda i: (0, 0))],
        out_shape=[jax.ShapeDtypeStruct((S, D), F32), jax.ShapeDtypeStruct((1, D), F32)],
        compiler_params=_params("arbitrary"),
    )(dxo, dh, x_in, gpre3)


def _zero_pads(pad_ref, S):
    z = jnp.zeros((CONV_PAD, pad_ref.shape[1]), F32)
    pad_ref[pl.ds(0, CONV_PAD), :] = z
    pad_ref[pl.ds(CONV_PAD + S, CONV_PAD), :] = z


def _conv_fwd_chunk(pad_ref, w_ref, K, r0, rows):
    acc = None
    for k in range(K):
        term = w_ref[pl.ds(k, 1), :] * pad_ref[pl.ds(CONV_PAD + r0 - (K - 1) + k, rows), :]
        acc = term if acc is None else acc + term
    return acc


def _conv_bwd_chunk(pad_ref, w_ref, K, r0, rows):
    acc = None
    for k in range(K):
        term = w_ref[pl.ds(k, 1), :] * pad_ref[pl.ds(CONV_PAD + r0 + (K - 1) - k, rows), :]
        acc = term if acc is None else acc + term
    return acc


def _conv_dw(upad_ref, dy_ref_or_pad, dy_off, K, S, dw_ref):
    rows = min(CONV_ROWS, S)
    for k in range(K):
        acc = None
        for r0 in range(0, S, rows):
            term = jnp.sum(dy_ref_or_pad[pl.ds(dy_off + r0, rows), :]
                           * upad_ref[pl.ds(CONV_PAD + r0 - (K - 1) + k, rows), :], axis=0, keepdims=True)
            acc = term if acc is None else acc + term
        dw_ref[pl.ds(k, 1), :] = acc


def _col_spec(S, off):
    return pl.BlockSpec((S, LANE), lambda j: (0, off // LANE + j))


def _sc_fwd(proj, conv_w, l, offs, DS, name):
    S = proj.shape[0]
    K = conv_w.shape[1]
    rows = min(CONV_ROWS, S)

    def body(b_ref, c_ref, u_ref, w_ref, o_ref, pad_ref):
        _zero_pads(pad_ref, S)
        pad_ref[pl.ds(CONV_PAD, S), :] = c_ref[...].astype(F32) * u_ref[...].astype(F32)
        for r0 in range(0, S, rows):
            cv = _conv_fwd_chunk(pad_ref, w_ref, K, r0, rows)
            o_ref[pl.ds(r0, rows), :] = (b_ref[pl.ds(r0, rows), :].astype(F32) * cv).astype(BF16)

    return pl.pallas_call(
        body, name=name, grid=(DS // LANE,),
        in_specs=[_col_spec(S, offs["sc_b"]), _col_spec(S, offs["sc_c"]), _col_spec(S, offs["sc_u"]),
                  pl.BlockSpec((None, K, LANE), lambda j: (l, 0, j))],
        out_specs=pl.BlockSpec((S, LANE), lambda j: (0, j)),
        out_shape=jax.ShapeDtypeStruct((S, DS), BF16),
        scratch_shapes=[pltpu.VMEM((S + 2 * CONV_PAD, LANE), F32)],
        compiler_params=_params("parallel"),
    )(proj, proj, proj, conv_w)


def _sc_bwd(dga, proj, conv_w, l, offs, DS, name):
    S = proj.shape[0]
    K = conv_w.shape[1]
    rows = min(CONV_ROWS, S)

    def body(d_ref, b_ref, c_ref, u_ref, w_ref, db_ref, dc_ref, du_ref, dw_ref, tpad_ref, gpad_ref):
        _zero_pads(tpad_ref, S)
        _zero_pads(gpad_ref, S)
        tpad_ref[pl.ds(CONV_PAD, S), :] = c_ref[...].astype(F32) * u_ref[...].astype(F32)
        for r0 in range(0, S, rows):
            sl = pl.ds(r0, rows)
            cv = _conv_fwd_chunk(tpad_ref, w_ref, K, r0, rows)
            d = d_ref[sl, :]
            db_ref[sl, :] = (d * cv).astype(BF16)
            gpad_ref[pl.ds(CONV_PAD + r0, rows), :] = d * b_ref[sl, :].astype(F32)
        for r0 in range(0, S, rows):
            sl = pl.ds(r0, rows)
            dt = _conv_bwd_chunk(gpad_ref, w_ref, K, r0, rows)
            dc_ref[sl, :] = (dt * u_ref[sl, :].astype(F32)).astype(BF16)
            du_ref[sl, :] = (dt * c_ref[sl, :].astype(F32)).astype(BF16)
        _conv_dw(tpad_ref, gpad_ref, CONV_PAD, K, S, dw_ref)

    blk = pl.BlockSpec((S, LANE), lambda j: (0, j))
    act = jax.ShapeDtypeStruct((S, DS), BF16)
    return pl.pallas_call(
        body, name=name, grid=(DS // LANE,),
        in_specs=[blk, _col_spec(S, offs["sc_b"]), _col_spec(S, offs["sc_c"]), _col_spec(S, offs["sc_u"]),
                  pl.BlockSpec((None, K, LANE), lambda j: (l, 0, j))],
        out_specs=[blk, blk, blk, pl.BlockSpec((K, LANE), lambda j: (0, j))],
        out_shape=[act, act, act, jax.ShapeDtypeStruct((K, DS), F32)],
        scratch_shapes=[pltpu.VMEM((S + 2 * CONV_PAD, LANE), F32), pltpu.VMEM((S + 2 * CONV_PAD, LANE), F32)],
        compiler_params=_params("parallel"),
    )(dga, proj, proj, proj, conv_w)


def _cf_conv_fwd(proj, conv_w, conv_b3, l, offs, DC, name):
    S = proj.shape[0]
    K = conv_w.shape[1]
    rows = min(CONV_ROWS, S)

    def body(a_ref, g_ref, w_ref, bias_ref, o_ref, pad_ref):
        _zero_pads(pad_ref, S)
        pad_ref[pl.ds(CONV_PAD, S), :] = a_ref[...].astype(F32) * _sigmoid(g_ref[...].astype(F32))
        for r0 in range(0, S, rows):
            o_ref[pl.ds(r0, rows), :] = _conv_fwd_chunk(pad_ref, w_ref, K, r0, rows) + bias_ref[...]

    return pl.pallas_call(
        body, name=name, grid=(DC // LANE,),
        in_specs=[_col_spec(S, offs["cf_a"]), _col_spec(S, offs["cf_g"]),
                  pl.BlockSpec((None, K, LANE), lambda j: (l, 0, j)),
                  pl.BlockSpec((None, 1, LANE), lambda j: (l, 0, j))],
        out_specs=pl.BlockSpec((S, LANE), lambda j: (0, j)),
        out_shape=jax.ShapeDtypeStruct((S, DC), F32),
        scratch_shapes=[pltpu.VMEM((S + 2 * CONV_PAD, LANE), F32)],
        compiler_params=_params("parallel"),
    )(proj, proj, conv_w, conv_b3)


def _layer_norm_hat(u):
    mu = jnp.mean(u, axis=-1, keepdims=True)
    xc = u - mu
    rstd = lax.rsqrt(jnp.mean(xc * xc, axis=-1, keepdims=True) + LN_EPS)
    return xc * rstd, rstd


def _cf_norm_fwd(u1, gam3, bet3, l, name):
    S, DC = u1.shape
    tr = _row_tile(S)

    def body(u_ref, g_ref, b_ref, o_ref):
        xhat, _ = _layer_norm_hat(u_ref[...])
        s = xhat * g_ref[...] + b_ref[...]
        o_ref[...] = (s * _sigmoid(s)).astype(BF16)

    row = pl.BlockSpec((tr, DC), lambda i: (i, 0))
    vec = pl.BlockSpec((None, 1, DC), lambda i: (l, 0, 0))
    return pl.pallas_call(
        body, name=name, grid=(S // tr,),
        in_specs=[row, vec, vec], out_specs=row,
        out_shape=jax.ShapeDtypeStruct((S, DC), BF16),
        compiler_params=_params("parallel"),
    )(u1, gam3, bet3)


def _cf_norm_bwd(du2, u1, gam3, bet3, l, name):
    S, DC = u1.shape
    tr = _row_tile(S)

    def body(d_ref, u_ref, g_ref, b_ref, du_ref, dg_ref, db_ref):
        @pl.when(pl.program_id(0) == 0)
        def _():
            dg_ref[...] = jnp.zeros_like(dg_ref)
            db_ref[...] = jnp.zeros_like(db_ref)

        xhat, rstd = _layer_norm_hat(u_ref[...])
        s = xhat * g_ref[...] + b_ref[...]
        sg = _sigmoid(s)
        ds = d_ref[...] * (sg * (1.0 + s * (1.0 - sg)))
        dg_ref[...] += jnp.sum(ds * xhat, axis=0, keepdims=True)
        db_ref[...] += jnp.sum(ds, axis=0, keepdims=True)
        dxh = ds * g_ref[...]
        du_ref[...] = rstd * (dxh - jnp.mean(dxh, axis=-1, keepdims=True)
                              - xhat * jnp.mean(dxh * xhat, axis=-1, keepdims=True))

    row = pl.BlockSpec((tr, DC), lambda i: (i, 0))
    vec = pl.BlockSpec((None, 1, DC), lambda i: (l, 0, 0))
    acc = pl.BlockSpec((1, DC), lambda i: (0, 0))
    return pl.pallas_call(
        body, name=name, grid=(S // tr,),
        in_specs=[row, row, vec, vec], out_specs=[row, acc, acc],
        out_shape=[jax.ShapeDtypeStruct((S, DC), F32), jax.ShapeDtypeStruct((1, DC), F32),
                   jax.ShapeDtypeStruct((1, DC), F32)],
        compiler_params=_params("arbitrary"),
    )(du2, u1, gam3, bet3)


def _cf_conv_bwd(du1, proj, conv_w, l, offs, DC, name):
    S = proj.shape[0]
    K = conv_w.shape[1]
    rows = min(CONV_ROWS, S)

    def body(d_ref, a_ref, g_ref, w_ref, da_ref, dgl_ref, dw_ref, dbias_ref, upad_ref, dpad_ref):
        _zero_pads(upad_ref, S)
        _zero_pads(dpad_ref, S)
        upad_ref[pl.ds(CONV_PAD, S), :] = a_ref[...].astype(F32) * _sigmoid(g_ref[...].astype(F32))
        dpad_ref[pl.ds(CONV_PAD, S), :] = d_ref[...]
        dbias_ref[...] = jnp.sum(d_ref[...], axis=0, keepdims=True)
        for r0 in range(0, S, rows):
            sl = pl.ds(r0, rows)
            du0 = _conv_bwd_chunk(dpad_ref, w_ref, K, r0, rows)
            a = a_ref[sl, :].astype(F32)
            sg = _sigmoid(g_ref[sl, :].astype(F32))
            da_ref[sl, :] = (du0 * sg).astype(BF16)
            dgl_ref[sl, :] = (du0 * a * sg * (1.0 - sg)).astype(BF16)
        _conv_dw(upad_ref, dpad_ref, CONV_PAD, K, S, dw_ref)

    blk = pl.BlockSpec((S, LANE), lambda j: (0, j))
    act = jax.ShapeDtypeStruct((S, DC), BF16)
    return pl.pallas_call(
        body, name=name, grid=(DC // LANE,),
        in_specs=[blk, _col_spec(S, offs["cf_a"]), _col_spec(S, offs["cf_g"]),
                  pl.BlockSpec((None, K, LANE), lambda j: (l, 0, j))],
        out_specs=[blk, blk, pl.BlockSpec((K, LANE), lambda j: (0, j)), pl.BlockSpec((1, LANE), lambda j: (0, j))],
        out_shape=[act, act, jax.ShapeDtypeStruct((K, DC), F32), jax.ShapeDtypeStruct((1, DC), F32)],
        scratch_shapes=[pltpu.VMEM((S + 2 * CONV_PAD, LANE), F32), pltpu.VMEM((S + 2 * CONV_PAD, LANE), F32)],
        compiler_params=_params("parallel"),
    )(du1, proj, proj, conv_w)


def _dot_nt(a, b):
    return lax.dot_general(a, b, (((1,), (1,)), ((), ())), preferred_element_type=F32)


def _dot_nn(a, b):
    return lax.dot_general(a, b, (((1,), (0,)), ((), ())), preferred_element_type=F32)


def _dot_tn(a, b):
    return lax.dot_general(a, b, (((0,), (0,)), ((), ())), preferred_element_type=F32)


def _dot_split(x, u):
    hi = x.astype(BF16)
    lo = (x - hi.astype(F32)).astype(BF16)
    return _dot_nn(hi, u) + _dot_nn(lo, u)


def _sb_scores(q, k, valid):
    z = _dot_nt(q, k) * (HEAD_DIM ** -0.5)
    lf = -(jnp.maximum(z, 0.0) + jnp.log(1.0 + jnp.exp(-jnp.abs(z))))
    return z, jnp.where(valid, lf, 0.0)


def _attn_fwd(proj, offs, DA, name):
    S = proj.shape[0]
    H = DA // HEAD_DIM
    nb = S // QB

    def body(q_ref, k_ref, v_ref, o_ref, tot_ref):
        row = lax.broadcasted_iota(jnp.int32, (QB, QB), 0)
        col = lax.broadcasted_iota(jnp.int32, (QB, QB), 1)
        u_after = (row > col).astype(BF16)

        def q_block(i, _):
            qs = pl.ds(pl.multiple_of(i * QB, QB), QB)
            q = q_ref[qs, :]

            def k_block(jj, carry):
                acc, c = carry
                ks = pl.ds(pl.multiple_of((i - jj) * QB, QB), QB)
                valid = col < row + jj * QB
                z, lf = _sb_scores(q, k_ref[ks, :], valid)
                log_a = lf + z + _dot_split(lf, u_after) + c
                a = jnp.where(valid, jnp.exp(log_a), 0.0)
                acc = acc + _dot_nn(a.astype(BF16), v_ref[ks, :])
                return acc, c + jnp.sum(lf, axis=1, keepdims=True)

            acc, c = lax.fori_loop(0, i + 1, k_block, (jnp.zeros((QB, HEAD_DIM), F32), jnp.zeros((QB, 1), F32)))
            o_ref[qs, :] = acc.astype(BF16)
            tot_ref[qs, :] = c
            return 0

        lax.fori_loop(0, nb, q_block, 0)

    def hs(off):
        return pl.BlockSpec((S, HEAD_DIM), lambda h: (0, off // HEAD_DIM + h))

    return pl.pallas_call(
        body, name=name, grid=(H,),
        in_specs=[hs(offs["q"]), hs(offs["k"]), hs(offs["v"])],
        out_specs=[pl.BlockSpec((S, HEAD_DIM), lambda h: (0, h)), pl.BlockSpec((None, S, 1), lambda h: (h, 0, 0))],
        out_shape=[jax.ShapeDtypeStruct((S, DA), BF16), jax.ShapeDtypeStruct((H, S, 1), F32)],
        compiler_params=_params("parallel"),
    )(proj, proj, proj)


def _attn_bwd(dout, tot, proj, offs, DA, name):
    S = proj.shape[0]
    H = DA // HEAD_DIM
    nb = S // QB
    scale = HEAD_DIM ** -0.5

    def body(q_ref, k_ref, v_ref, tot_ref, do_ref, dq_ref, dk_ref, dv_ref, dk_acc, dv_acc):
        row = lax.broadcasted_iota(jnp.int32, (QB, QB), 0)
        col = lax.broadcasted_iota(jnp.int32, (QB, QB), 1)
        u_after = (row > col).astype(BF16)
        u_before = (row < col).astype(BF16)
        dk_acc[...] = jnp.zeros_like(dk_acc)
        dv_acc[...] = jnp.zeros_like(dv_acc)

        def q_block(i, _):
            qs = pl.ds(pl.multiple_of(i * QB, QB), QB)
            q = q_ref[qs, :]
            do = do_ref[qs, :]
            tot_q = tot_ref[qs, :]

            def k_block(j, carry):
                dq, seen, gsum = carry
                ks = pl.ds(pl.multiple_of(j * QB, QB), QB)
                k = k_ref[ks, :]
                v = v_ref[ks, :]
                valid = col < row + (i - j) * QB
                z, lf = _sb_scores(q, k, valid)
                seen = seen + jnp.sum(lf, axis=1, keepdims=True)
                log_a = lf + z + _dot_split(lf, u_after) + (tot_q - seen)
                a = jnp.where(valid, jnp.exp(log_a), 0.0)
                g = a * _dot_nt(do, v)
                before = gsum + _dot_split(g, u_before)
                dz = jnp.where(valid, g * jnp.exp(lf) - before * jnp.exp(lf + z), 0.0) * scale
                dzb = dz.astype(BF16)
                dk_acc[ks, :] += _dot_tn(dzb, q)
                dv_acc[ks, :] += _dot_tn(a.astype(BF16), do)
                return dq + _dot_nn(dzb, k), seen, gsum + jnp.sum(g, axis=1, keepdims=True)

            zero = jnp.zeros((QB, 1), F32)
            dq, _, _ = lax.fori_loop(0, i + 1, k_block, (jnp.zeros((QB, HEAD_DIM), F32), zero, zero))
            dq_ref[qs, :] = dq.astype(BF16)
            return 0

        lax.fori_loop(0, nb, q_block, 0)
        dk_ref[...] = dk_acc[...].astype(BF16)
        dv_ref[...] = dv_acc[...].astype(BF16)

    def hs(off):
        return pl.BlockSpec((S, HEAD_DIM), lambda h: (0, off // HEAD_DIM + h))

    head = pl.BlockSpec((S, HEAD_DIM), lambda h: (0, h))
    act = jax.ShapeDtypeStruct((S, DA), BF16)
    return pl.pallas_call(
        body, name=name, grid=(H,),
        in_specs=[hs(offs["q"]), hs(offs["k"]), hs(offs["v"]), pl.BlockSpec((None, S, 1), lambda h: (h, 0, 0)), head],
        out_specs=[head, head, head], out_shape=[act, act, act],
        scratch_shapes=[pltpu.VMEM((S, HEAD_DIM), F32), pltpu.VMEM((S, HEAD_DIM), F32)],
        compiler_params=_params("parallel"),
    )(proj, proj, proj, tot, dout)


def _merge_tiles(S, D, goff):
    tn = LANE
    for t in range(LANE, 513, LANE):
        if D % t == 0 and goff % t == 0:
            tn = t
    return _tile(S, 512, 8), tn


def _merge_fwd(ga, attn, u2, pa, pb, pc, l, proj, goff, name):
    S = ga.shape[0]
    D = pa.shape[-1]
    tm, tn = _merge_tiles(S, D, goff)

    def body(ga_ref, at_ref, u2_ref, pa_ref, pb_ref, pc_ref, la_ref, lb_ref, lc_ref, ya_ref, yb_ref, yc_ref, m_ref):
        ya = _dot_nn(ga_ref[...], pa_ref[...])
        yb = _dot_nn(at_ref[...], pb_ref[...])
        yc = _dot_nn(u2_ref[...], pc_ref[...])
        ya_ref[...] = ya.astype(BF16)
        yb_ref[...] = yb.astype(BF16)
        yc_ref[...] = yc.astype(BF16)
        m_ref[...] = (_sigmoid(la_ref[...].astype(F32)) * ya + _sigmoid(lb_ref[...].astype(F32)) * yb
                      + _sigmoid(lc_ref[...].astype(F32)) * yc).astype(BF16)

    def lhs(a):
        return pl.BlockSpec((tm, a.shape[1]), lambda i, j: (i, 0))

    def rhs(p):
        return pl.BlockSpec((None, p.shape[1], tn), lambda i, j: (l, 0, j))

    def gate(r):
        return pl.BlockSpec((tm, tn), lambda i, j: (i, (goff + r * D) // tn + j))

    out = pl.BlockSpec((tm, tn), lambda i, j: (i, j))
    act = jax.ShapeDtypeStruct((S, D), BF16)
    return pl.pallas_call(
        body, name=name, grid=(S // tm, D // tn),
        in_specs=[lhs(ga), lhs(attn), lhs(u2), rhs(pa), rhs(pb), rhs(pc), gate(0), gate(1), gate(2)],
        out_specs=[out, out, out, out], out_shape=[act, act, act, act],
        compiler_params=_params("parallel", "parallel"),
    )(ga, attn, u2, pa, pb, pc, proj, proj, proj)


def _merge_bwd(dm, ya, yb, yc, proj, goff, name):
    S, D = dm.shape
    tm, tn = _merge_tiles(S, D, goff)

    def body(dm_ref, ya_ref, yb_ref, yc_ref, la_ref, lb_ref, lc_ref, *o_refs):
        dmv = dm_ref[...]
        for y_ref, l_ref, dy_ref, dl_ref in zip((ya_ref, yb_ref, yc_ref), (la_ref, lb_ref, lc_ref),
                                                o_refs[:3], o_refs[3:]):
            sg = _sigmoid(l_ref[...].astype(F32))
            dy_ref[...] = (dmv * sg).astype(BF16)
            dl_ref[...] = (dmv * y_ref[...].astype(F32) * sg * (1.0 - sg)).astype(BF16)

    def gate(r):
        return pl.BlockSpec((tm, tn), lambda i, j: (i, (goff + r * D) // tn + j))

    blk = pl.BlockSpec((tm, tn), lambda i, j: (i, j))
    act = jax.ShapeDtypeStruct((S, D), BF16)
    return pl.pallas_call(
        body, name=name, grid=(S // tm, D // tn),
        in_specs=[blk, blk, blk, blk, gate(0), gate(1), gate(2)],
        out_specs=[blk] * 6, out_shape=[act] * 6,
        compiler_params=_params("parallel", "parallel"),
    )(dm, ya, yb, yc, proj, proj, proj)


def _ew_tiles(rows, cols):
    tc = cols if cols <= 4096 else _tile(cols, 2048)
    tr = _tile(rows, max(8, (1 << 19) // tc), 8)
    return tr, tc


def _half_rows_tile(Rh, Cs):
    return _tile(Rh, max(16, (1 << 19) // Cs), 16)


def _place_shard(w, kind, place, name):
    L, Rs, Cs = w.shape
    tr = _half_rows_tile(Rs, Cs)

    def body(pr_ref, w_ref, o_ref):
        o_ref[...] = w_ref[...].astype(BF16)

    if kind == "col":
        shape = (L, Rs, 4 * Cs)
        o_spec = pl.BlockSpec((None, tr, Cs), lambda l, i, pr: (l, i, pr[1]))
    else:
        shape = (L, 4, Rs, Cs)
        o_spec = pl.BlockSpec((None, None, tr, Cs), lambda l, i, pr: (l, pr[1], i, 0))
    out = pl.pallas_call(
        body, name=name,
        grid_spec=pltpu.PrefetchScalarGridSpec(
            num_scalar_prefetch=1, grid=(L, Rs // tr),
            in_specs=[pl.BlockSpec((None, tr, Cs), lambda l, i, pr: (l, i, 0))], out_specs=o_spec),
        out_shape=jax.ShapeDtypeStruct(shape, BF16), compiler_params=_params("parallel", "parallel"),
    )(place, w)
    return out if kind == "col" else out.reshape(L, 4 * Rs, Cs)


def _pair_sum(g, got, kind, place, name):
    L, _, Rh, Cs = got.shape
    tr = _half_rows_tile(Rh, Cs)
    if kind == "col":
        gv = g.reshape(L, 2, Rh, 4 * Cs)
        g_spec = pl.BlockSpec((None, None, tr, Cs), lambda l, p, i, pr: (l, pr[0], i, p))
    else:
        gv = g.reshape(L, 4, 2, Rh, Cs)
        g_spec = pl.BlockSpec((None, None, None, tr, Cs), lambda l, p, i, pr: (l, p, pr[0], i, 0))
    blk = pl.BlockSpec((None, None, tr, Cs), lambda l, p, i, pr: (l, p, i, 0))

    def body(pr_ref, g_ref, r_ref, o_ref):
        o_ref[...] = (g_ref[...].astype(F32) + r_ref[...].astype(F32)).astype(BF16)

    return pl.pallas_call(
        body, name=name,
        grid_spec=pltpu.PrefetchScalarGridSpec(num_scalar_prefetch=1, grid=(L, 4, Rh // tr),
                                               in_specs=[g_spec, blk], out_specs=blk),
        out_shape=jax.ShapeDtypeStruct(got.shape, BF16), compiler_params=_params("parallel", "parallel", "parallel"),
    )(place, gv, got)


def _chip_sum(part, got, place, name):
    n, L, Rh, Cs = got.shape
    tr = _half_rows_tile(Rh, Cs)

    def body(pr_ref, p_ref, r_ref, o_ref):
        acc = p_ref[...].astype(F32)
        for s in range(n):
            acc = acc + r_ref[s].astype(F32)
        o_ref[...] = acc

    out = pl.pallas_call(
        body, name=name,
        grid_spec=pltpu.PrefetchScalarGridSpec(
            num_scalar_prefetch=1, grid=(L, Rh // tr),
            in_specs=[pl.BlockSpec((None, None, tr, Cs), lambda l, i, pr: (l, pr[1], i, 0)),
                      pl.BlockSpec((n, None, tr, Cs), lambda l, i, pr: (0, l, i, 0))],
            out_specs=pl.BlockSpec((None, None, tr, Cs), lambda l, i, pr: (l, pr[0], i, 0))),
        out_shape=jax.ShapeDtypeStruct((L, 2, Rh, Cs), F32), compiler_params=_params("parallel", "parallel"),
    )(place, part, got)
    return out.reshape(L, 2 * Rh, Cs)


def _adamw(w, g, m, v, name):
    shape = w.shape
    args = [t.reshape(-1, shape[-1]) for t in (w, g, m, v)]
    rows, cols = args[0].shape
    tr, tc = _ew_tiles(rows, cols)
    c1 = 1.0 - ADAM_B1 ** ADAM_STEP
    c2 = 1.0 - ADAM_B2 ** ADAM_STEP

    def body(w_ref, g_ref, m_ref, v_ref, d_ref, nm_ref, nv_ref):
        gv = g_ref[...]
        nm = ADAM_B1 * m_ref[...] + (1.0 - ADAM_B1) * gv
        nv = ADAM_B2 * v_ref[...] + (1.0 - ADAM_B2) * (gv * gv)
        nm_ref[...] = nm
        nv_ref[...] = nv
        d_ref[...] = -ADAM_LR * ((nm / c1) / (jnp.sqrt(nv / c2) + ADAM_EPS) + ADAM_WD * w_ref[...])

    blk = pl.BlockSpec((tr, tc), lambda i, j: (i, j))
    shp = jax.ShapeDtypeStruct((rows, cols), F32)
    outs = pl.pallas_call(
        body, name=name, grid=(rows // tr, cols // tc), in_specs=[blk] * 4, out_specs=[blk] * 3,
        out_shape=[shp] * 3, compiler_params=_params("parallel", "parallel"),
    )(*args)
    return [o.reshape(shape) for o in outs]


def _place():
    x, y, c = lax.axis_index("x"), lax.axis_index("y"), lax.axis_index("c")
    chips = [(1 - x, y), (x, 1 - y), (1 - x, 1 - y)]
    return x, y, c, chips


def _al(v, unit):
    return pl.multiple_of(v, unit) if unit % LANE == 0 else v


def _half_of_full(ref, kind, p, half, Rs, Cs):
    Rh = (ref.shape[-2] // 2) if kind == "col" else Rs // 2
    lead = (slice(None),) * (len(ref.shape) - 2)
    if kind == "col":
        return ref.at[lead + (pl.ds(_al(half * Rh, Rh), Rh), pl.ds(_al(p * Cs, Cs), Cs))]
    return ref.at[lead + (pl.ds(_al(p * Rs + half * Rh, Rh), Rh), slice(None))]


def _gather_weights(fulls, kinds):
    n = len(fulls)

    def body(*refs):
        outs = refs[n:2 * n]
        ici_send, ici_recv, d2d_send, d2d_recv = refs[2 * n:]
        x, y, c, chips = _place()
        me = 2 * x + y
        sib = (x, y, 1 - c)

        def piece(w, p, half):
            _, R, C = outs[w].shape
            Rs, Cs = (R, C // 4) if kinds[w] == "col" else (R // 4, C)
            return _half_of_full(outs[w], kinds[w], p, half, Rs, Cs)

        sends = []
        for j, chip in enumerate(chips):
            for w in range(n):
                cp = pltpu.make_async_remote_copy(piece(w, me, c), piece(w, me, c), ici_send.at[j * n + w],
                                                  ici_recv.at[j * n + w], device_id=(*chip, c), device_id_type=MESH)
                cp.start()
                sends.append(cp)
        for j, (px, py) in enumerate(chips):
            for w in range(n):
                got = piece(w, 2 * px + py, c)
                pltpu.make_async_remote_copy(got, got, ici_send.at[j * n + w], ici_recv.at[j * n + w],
                                             device_id=(px, py, c), device_id_type=MESH).wait_recv()
                cp = pltpu.make_async_remote_copy(got, got, d2d_send.at[j * n + w], d2d_recv.at[j * n + w],
                                                  device_id=sib, device_id_type=MESH)
                cp.start()
                sends.append(cp)
        for j, (px, py) in enumerate(chips):
            for w in range(n):
                got = piece(w, 2 * px + py, 1 - c)
                pltpu.make_async_remote_copy(got, got, d2d_send.at[j * n + w], d2d_recv.at[j * n + w],
                                             device_id=sib, device_id_type=MESH).wait_recv()
        for cp in sends:
            cp.wait_send()

    return pl.pallas_call(
        body, name="gather_weights", in_specs=[ANY] * n, out_specs=[ANY] * n,
        out_shape=[jax.ShapeDtypeStruct(f.shape, f.dtype) for f in fulls],
        input_output_aliases={w: w for w in range(n)},
        scratch_shapes=[pltpu.SemaphoreType.DMA((3 * n,))] * 4,
    )(*fulls)


def _reduce_to_sibling_halves(grads, kinds):
    n = len(grads)
    shapes = []
    for g, kind in zip(grads, kinds):
        L, R, C = g.shape
        shapes.append((L, 4, R // 2, C // 4) if kind == "col" else (L, 4, R // 8, C))

    def body(*refs):
        ins, gots = refs[:n], refs[n:2 * n]
        send_sem, recv_sem = refs[2 * n:]
        x, y, c, _ = _place()
        sib = (x, y, 1 - c)
        copies = []
        for w in range(n):
            _, _, Rh, Cs = shapes[w]
            for p in range(4):
                cp = pltpu.make_async_remote_copy(_half_of_full(ins[w], kinds[w], p, 1 - c, 2 * Rh, Cs),
                                                  gots[w].at[:, p], send_sem.at[4 * w + p], recv_sem.at[4 * w + p],
                                                  device_id=sib, device_id_type=MESH)
                cp.start()
                copies.append(cp)
        for cp in copies:
            cp.wait()

    return pl.pallas_call(
        body, name="reduce_d2d", in_specs=[ANY] * n, out_specs=[ANY] * n,
        out_shape=[jax.ShapeDtypeStruct(s, BF16) for s in shapes],
        scratch_shapes=[pltpu.SemaphoreType.DMA((4 * n,))] * 2,
    )(*grads)


def _exchange_over_ici(parts):
    n = len(parts)

    def body(*refs):
        ins, gots = refs[:n], refs[n:2 * n]
        send_sem, recv_sem = refs[2 * n:]
        x, y, c, chips = _place()
        copies = []
        for j, (px, py) in enumerate(chips):
            for w in range(n):
                cp = pltpu.make_async_remote_copy(ins[w].at[:, 2 * px + py], gots[w].at[j], send_sem.at[j * n + w],
                                                  recv_sem.at[j * n + w], device_id=(px, py, c), device_id_type=MESH)
                cp.start()
                copies.append(cp)
        for cp in copies:
            cp.wait()

    return pl.pallas_call(
        body, name="reduce_ici", in_specs=[ANY] * n, out_specs=[ANY] * n,
        out_shape=[jax.ShapeDtypeStruct((3, p.shape[0]) + p.shape[2:], p.dtype) for p in parts],
        scratch_shapes=[pltpu.SemaphoreType.DMA((3 * n,))] * 2,
    )(*parts)


def _share_with_sibling(reduced):
    n = len(reduced)

    def body(*refs):
        outs = refs[n:2 * n]
        send_sem, recv_sem = refs[2 * n:]
        x, y, c, _ = _place()
        sib = (x, y, 1 - c)
        copies = []
        for w in range(n):
            Rh = outs[w].shape[1] // 2
            mine = outs[w].at[:, pl.ds(_al(c * Rh, Rh), Rh), :]
            cp = pltpu.make_async_remote_copy(mine, mine, send_sem.at[w], recv_sem.at[w], device_id=sib,
                                              device_id_type=MESH)
            cp.start()
            copies.append(cp)
        for cp in copies:
            cp.wait()

    return pl.pallas_call(
        body, name="reduce_share", in_specs=[ANY] * n, out_specs=[ANY] * n,
        out_shape=[jax.ShapeDtypeStruct(r.shape, r.dtype) for r in reduced],
        input_output_aliases={w: w for w in range(n)},
        scratch_shapes=[pltpu.SemaphoreType.DMA((n,))] * 2,
    )(*reduced)


def _all_gather_small(v):
    r = v.shape[0]

    def body(v_ref, o_ref, send_sem, recv_sem):
        x, y, c, _ = _place()
        me = 4 * x + 2 * y + c
        o_ref[me] = v_ref[...]
        copies = []
        for k in range(1, 8):
            peer = (x ^ (k >> 2), y ^ ((k >> 1) & 1), c ^ (k & 1))
            cp = pltpu.make_async_remote_copy(v_ref, o_ref.at[me], send_sem.at[k - 1], recv_sem.at[k - 1],
                                              device_id=peer, device_id_type=MESH)
            cp.start()
            copies.append(cp)
        for cp in copies:
            cp.wait()

    vmem = pl.BlockSpec(memory_space=pltpu.VMEM)
    return pl.pallas_call(
        body, name="all_gather_small", in_specs=[vmem], out_specs=vmem,
        out_shape=jax.ShapeDtypeStruct((8, r, LANE), F32),
        scratch_shapes=[pltpu.SemaphoreType.DMA((7,)), pltpu.SemaphoreType.DMA((7,))],
        compiler_params=pltpu.CompilerParams(vmem_limit_bytes=VMEM_LIMIT),
    )(v)


def _sum_slots(g):
    n, r, _ = g.shape

    def body(g_ref, o_ref):
        acc = g_ref[0]
        for s in range(1, n):
            acc = acc + g_ref[s]
        o_ref[...] = acc

    vmem = pl.BlockSpec(memory_space=pltpu.VMEM)
    return pl.pallas_call(
        body, name="sum_slots", in_specs=[vmem], out_specs=vmem, out_shape=jax.ShapeDtypeStruct((r, LANE), F32),
        compiler_params=pltpu.CompilerParams(vmem_limit_bytes=VMEM_LIMIT),
    )(g)


def _pack(arrays):
    flat = jnp.concatenate([a.reshape(-1) for a in arrays])
    pad = (-flat.shape[0]) % (8 * LANE)
    return jnp.pad(flat, (0, pad)).reshape(-1, LANE)


def _unpack(packed, like):
    flat = packed.reshape(-1)
    out, off = [], 0
    for a in like:
        out.append(flat[off:off + a.size].reshape(a.shape))
        off += a.size
    return out


def _offsets(D):
    DA, DS, DC = D // 2, D // 4, D // 4
    names = ["q", "k", "v", "sc_b", "sc_c", "sc_u", "cf_a", "cf_g", "gate"]
    sizes = [DA, DA, DA, DS, DS, DS, DC, DC, 3 * D]
    offs, o = {}, 0
    for nm, sz in zip(names, sizes):
        offs[nm] = o
        o += sz
    return offs, DA, DS, DC


def _relu2_epilogue(acc):
    r = jnp.maximum(acc, 0.0)
    return acc, r * r


def _drelu2_epilogue(acc, up):
    return (acc * (2.0 * jnp.maximum(up.astype(F32), 0.0)),)


def _local_step(x, target, gains, conv_a_w, conv_c_w, conv_c_b, norm_c_g, norm_c_b, big):
    S, D = x.shape
    L = gains[0].shape[0]
    offs, DA, DS, DC = _offsets(D)
    goff = offs["gate"]
    g_mix_pre, g_mix_post, g_mlp_pre, g_mlp_post = gains
    cb3, ng3, nb3 = (t.reshape(L, 1, DC) for t in (conv_c_b, norm_c_g, norm_c_b))

    saved = []
    h = _rms_fwd(x, g_mix_pre, 0, "rms_first")
    xin = x
    for l in range(L):
        proj = _mm(h, big["w_in"], b_layer=l, out_dtypes=(BF16,), name=f"fwd_w_in_{l}", tn_cap=512)
        ga = _sc_fwd(proj, conv_a_w, l, offs, DS, f"sc_fwd_{l}")
        attn, attn_tot = _attn_fwd(proj, offs, DA, f"attn_fwd_{l}")
        u1 = _cf_conv_fwd(proj, conv_c_w, cb3, l, offs, DC, f"cf_conv_fwd_{l}")
        u2 = _cf_norm_fwd(u1, ng3, nb3, l, f"cf_norm_fwd_{l}")
        ya, yb, yc, merged = _merge_fwd(ga, attn, u2, big["proj_a"], big["proj_b"], big["proj_c"], l, proj, goff,
                                        f"merge_fwd_{l}")
        mixed = _mm(merged, big["w_o"], b_layer=l, name=f"fwd_w_o_{l}")
        x1, h2 = _post_res_fwd(xin, mixed, g_mix_post, l, g_mlp_pre, l, f"mix_residual_{l}")
        up, act = _mm(h2, big["w_up"], b_layer=l, out_dtypes=(BF16, BF16), epilogue=_relu2_epilogue,
                      name=f"fwd_w_up_{l}")
        f = _mm(act, big["w_down"], b_layer=l, name=f"fwd_w_down_{l}")
        saved.append(dict(xin=xin, h=h, proj=proj, ga=ga, attn=attn, attn_tot=attn_tot, u1=u1, u2=u2, ya=ya, yb=yb, yc=yc,
                          merged=merged, mixed=mixed, x1=x1, h2=h2, up=up, act=act, f=f))
        if l + 1 < L:
            xin, h = _post_res_fwd(x1, f, g_mlp_post, l, g_mix_pre, l + 1, f"mlp_residual_{l}")
        else:
            dx, loss = _final_fwd_loss(x1, f, g_mlp_post, l, target, "loss_head")

    small = {k: [None] * L for k in ("mix_pre", "mix_post", "mlp_pre", "mlp_post", "conv_a_w", "conv_c_w",
                                       "conv_c_b", "norm_c_g", "norm_c_b")}
    bigg = {k: None for k in BIG}

    def dw(key, a, b, l, **kw):
        bigg[key] = _mm(a, b, ta=True, out_dtypes=(BF16,), stack=(l, L, bigg[key]), name=f"d{key}_{l}", **kw)

    for l in reversed(range(L)):
        s = saved[l]
        df, small["mlp_post"][l] = _post_bwd(dx, s["f"], g_mlp_post, l, f"mlp_post_bwd_{l}")
        dw("w_down", s["act"], df, l)
        dup = _mm(df, big["w_down"], tb=True, b_layer=l, out_dtypes=(BF16,), epilogue=_drelu2_epilogue,
                  extras=(s["up"],), name=f"d_up_{l}")
        dw("w_up", s["h2"], dup, l)
        dh2 = _mm(dup, big["w_up"], tb=True, b_layer=l, name=f"d_h2_{l}")
        dx1, small["mlp_pre"][l] = _pre_bwd(dx, dh2, s["x1"], g_mlp_pre, l, f"mlp_pre_bwd_{l}")
        dmixed, small["mix_post"][l] = _post_bwd(dx1, s["mixed"], g_mix_post, l, f"mix_post_bwd_{l}")
        dw("w_o", s["merged"], dmixed, l)
        dmerged = _mm(dmixed, big["w_o"], tb=True, b_layer=l, name=f"d_merged_{l}")
        dya, dyb, dyc, dla, dlb, dlc = _merge_bwd(dmerged, s["ya"], s["yb"], s["yc"], s["proj"], goff,
                                                  f"merge_bwd_{l}")
        dw("proj_a", s["ga"], dya, l)
        dw("proj_b", s["attn"], dyb, l)
        dw("proj_c", s["u2"], dyc, l)
        dga = _mm(dya, big["proj_a"], tb=True, b_layer=l, name=f"d_ga_{l}")
        dattn = _mm(dyb, big["proj_b"], tb=True, b_layer=l, out_dtypes=(BF16,), name=f"d_attn_{l}")
        du2 = _mm(dyc, big["proj_c"], tb=True, b_layer=l, name=f"d_u2_{l}")
        dsb, dsc, dsu, small["conv_a_w"][l] = _sc_bwd(dga, s["proj"], conv_a_w, l, offs, DS, f"sc_bwd_{l}")
        du1, small["norm_c_g"][l], small["norm_c_b"][l] = _cf_norm_bwd(du2, s["u1"], ng3, nb3, l, f"cf_norm_bwd_{l}")
        dca, dcg, small["conv_c_w"][l], small["conv_c_b"][l] = _cf_conv_bwd(du1, s["proj"], conv_c_w, l, offs, DC,
                                                                          f"cf_conv_bwd_{l}")
        dq, dk, dv = _attn_bwd(dattn, s["attn_tot"], s["proj"], offs, DA, f"attn_bwd_{l}")
        dproj = jnp.concatenate([dq, dk, dv, dsb, dsc, dsu, dca, dcg, dla, dlb, dlc], axis=1)
        dw("w_in", s["h"], dproj, l, tn_cap=512)
        dh = _mm(dproj, big["w_in"], tb=True, b_layer=l, name=f"d_h_{l}")
        dx, small["mix_pre"][l] = _pre_bwd(dx1, dh, s["xin"], g_mix_pre, l, f"mix_pre_bwd_{l}")
    return loss, dx, small, bigg


BIG = ("w_in", "proj_a", "proj_b", "proj_c", "w_o", "w_up", "w_down")
BIG_KIND = {"w_in": "col", "proj_a": "col", "proj_b": "col", "proj_c": "col", "w_o": "row", "w_up": "col",
            "w_down": "row"}


def kernel(x, ln_mix_pre, ln_mix_post, ln_mlp_pre, ln_mlp_post, w_in, conv_a_w, proj_a, proj_b, conv_c_w, conv_c_b, norm_c_g, norm_c_b, proj_c, w_o, w_up, w_down, loss_target, m_ln_mix_pre, m_ln_mix_post, m_ln_mlp_pre, m_ln_mlp_post, m_w_in, m_conv_a_w, m_proj_a, m_proj_b, m_conv_c_w, m_conv_c_b, m_norm_c_g, m_norm_c_b, m_proj_c, m_w_o, m_w_up, m_w_down, v_ln_mix_pre, v_ln_mix_post, v_ln_mlp_pre, v_ln_mlp_post, v_w_in, v_conv_a_w, v_proj_a, v_proj_b, v_conv_c_w, v_conv_c_b, v_norm_c_g, v_norm_c_b, v_proj_c, v_w_o, v_w_up, v_w_down):
    weights = dict(ln_mix_pre=ln_mix_pre, ln_mix_post=ln_mix_post, ln_mlp_pre=ln_mlp_pre, ln_mlp_post=ln_mlp_post,
                   w_in=w_in, conv_a_w=conv_a_w, proj_a=proj_a, proj_b=proj_b, conv_c_w=conv_c_w, conv_c_b=conv_c_b,
                   norm_c_g=norm_c_g, norm_c_b=norm_c_b, proj_c=proj_c, w_o=w_o, w_up=w_up, w_down=w_down)
    m_in = dict(ln_mix_pre=m_ln_mix_pre, ln_mix_post=m_ln_mix_post, ln_mlp_pre=m_ln_mlp_pre, ln_mlp_post=m_ln_mlp_post,
                w_in=m_w_in, conv_a_w=m_conv_a_w, proj_a=m_proj_a, proj_b=m_proj_b, conv_c_w=m_conv_c_w,
                conv_c_b=m_conv_c_b, norm_c_g=m_norm_c_g, norm_c_b=m_norm_c_b, proj_c=m_proj_c, w_o=m_w_o,
                w_up=m_w_up, w_down=m_w_down)
    v_in = dict(ln_mix_pre=v_ln_mix_pre, ln_mix_post=v_ln_mix_post, ln_mlp_pre=v_ln_mlp_pre, ln_mlp_post=v_ln_mlp_post,
                w_in=v_w_in, conv_a_w=v_conv_a_w, proj_a=v_proj_a, proj_b=v_proj_b, conv_c_w=v_conv_c_w,
                conv_c_b=v_conv_c_b, norm_c_g=v_norm_c_g, norm_c_b=v_norm_c_b, proj_c=v_proj_c, w_o=v_w_o,
                w_up=v_w_up, w_down=v_w_down)
    order = list(weights)
    L, D = ln_mix_pre.shape
    chip = 2 * lax.axis_index("x") + lax.axis_index("y")
    place = jnp.stack([lax.axis_index("c"), chip]).astype(jnp.int32)

    kinds = [BIG_KIND[k] for k in BIG]
    placed = [_place_shard(weights[k], BIG_KIND[k], place, f"place_{k}") for k in BIG]
    full = dict(zip(BIG, _gather_weights(placed, kinds)))
    conv_local = [conv_a_w, conv_c_w]
    slots = _all_gather_small(_pack(conv_local))
    per_chip = [_unpack(slots[4 * px + 2 * py], conv_local) for px in range(2) for py in range(2)]
    conv_a_full = jnp.concatenate([pc[0] for pc in per_chip], axis=-1)
    conv_c_full = jnp.concatenate([pc[1] for pc in per_chip], axis=-1)

    gains = [weights[k].reshape(L, 1, D) for k in ("ln_mix_pre", "ln_mix_post", "ln_mlp_pre", "ln_mlp_post")]
    loss, dx, small, bigg = _local_step(x[0], loss_target[0], gains, conv_a_full, conv_c_full, conv_c_b, norm_c_g,
                                        norm_c_b, full)

    glist = [bigg[k] for k in BIG]
    got1 = _reduce_to_sibling_halves(glist, kinds)
    pair = [_pair_sum(g, r, BIG_KIND[k], place, f"pair_sum_{k}") for k, g, r in zip(BIG, glist, got1)]
    got2 = _exchange_over_ici(pair)
    reduced = [_chip_sum(p, r, place, f"chip_sum_{k}") for k, p, r in zip(BIG, pair, got2)]
    grads = dict(zip(BIG, _share_with_sibling(reduced)))

    small_names = ["ln_mix_pre", "ln_mix_post", "ln_mlp_pre", "ln_mlp_post", "conv_a_w", "conv_c_w", "conv_c_b",
                   "norm_c_g", "norm_c_b"]
    small_key = dict(ln_mix_pre="mix_pre", ln_mix_post="mix_post", ln_mlp_pre="mlp_pre", ln_mlp_post="mlp_post")
    small_local = []
    for k in small_names:
        per_layer = small[small_key.get(k, k)]
        stacked = jnp.stack(per_layer)
        small_local.append(stacked.reshape(L, -1) if stacked.shape[1] == 1 else stacked)
    small_sum = _unpack(_sum_slots(_all_gather_small(_pack(small_local))), small_local)
    for k, g in zip(small_names, small_sum):
        if k in ("conv_a_w", "conv_c_w"):
            width = weights[k].shape[-1]
            g = lax.dynamic_slice_in_dim(g, chip * width, width, axis=2)
        grads[k] = g

    delta, new_m, new_v = {}, {}, {}
    for k in BIG:
        delta[k], new_m[k], new_v[k] = _adamw(weights[k], grads[k], m_in[k], v_in[k], f"adamw_{k}")
    packed = [_pack([t[k] for k in small_names]) for t in (weights, grads, m_in, v_in)]
    like = [weights[k] for k in small_names]
    for dst, res in zip((delta, new_m, new_v), _adamw(*packed, "adamw_small")):
        dst.update(zip(small_names, _unpack(res, like)))

    total = lax.psum(loss[0, 0], ("x", "y", "c"))
    return (total, dx[None], *[grads[k] for k in order], *[delta[k] for k in order],
            *[new_m[k] for k in order], *[new_v[k] for k in order])
```

```python
import functools

import jax
import jax.numpy as jnp
from jax import lax
from jax.experimental import pallas as pl
from jax.experimental.pallas import tpu as pltpu

F32 = jnp.float32
BF16 = jnp.bfloat16
MESH = pl.DeviceIdType.MESH

HEAD_DIM = 128
QB = 128
ATTN_FWD_HEADS = 4
ATTN_BWD_HEADS = 2
RMS_EPS = 1e-6
LN_EPS = 1e-5
ADAM_LR = 0.001
ADAM_B1 = 0.9
ADAM_B2 = 0.999
ADAM_EPS = 1e-08
ADAM_WD = 0.01
ADAM_STEP = 10
LANE = 128
VMEM_LIMIT = 56 * 1024 * 1024
CONV_PAD = 32
CONV_ROWS = 256
ANY = pl.BlockSpec(memory_space=pl.ANY)


def _tile(n, cap, mult=LANE):
    best = None
    t = mult
    while t <= min(n, cap):
        if n % t == 0:
            best = t
        t += mult
    return best if best is not None else n


def _params(*sem):
    return pltpu.CompilerParams(dimension_semantics=sem if sem else None, vmem_limit_bytes=VMEM_LIMIT)


def _sigmoid(x):
    return 1.0 / (1.0 + jnp.exp(-x))


def _mm(a, b, *, name, ta=False, tb=False, out_dtypes=(F32,), epilogue=None, extras=(),
        tm_cap=1024, tn_cap=1024, tk_cap=2048):
    if ta:
        K, M = a.shape
    else:
        M, K = a.shape
    N = b.shape[0] if tb else b.shape[1]
    tm, tn, tk = _tile(M, tm_cap), _tile(N, tn_cap), _tile(K, tk_cap)
    nk = K // tk
    a_spec = pl.BlockSpec((tk, tm), lambda i, j, k: (k, i)) if ta else pl.BlockSpec((tm, tk), lambda i, j, k: (i, k))
    b_spec = pl.BlockSpec((tn, tk), lambda i, j, k: (j, k)) if tb else pl.BlockSpec((tk, tn), lambda i, j, k: (k, j))
    e_specs = [pl.BlockSpec((tm, tn), lambda i, j, k: (i, j)) for _ in extras]
    dims = (((0 if ta else 1,), (1 if tb else 0,)), ((), ()))
    n_e, n_o = len(extras), len(out_dtypes)

    def body(a_ref, b_ref, *rest):
        e_refs, o_refs = rest[:n_e], rest[n_e:n_e + n_o]
        part = lax.dot_general(a_ref[...].astype(BF16), b_ref[...].astype(BF16), dims, preferred_element_type=F32)

        def finish(acc):
            outs = (acc,) if epilogue is None else epilogue(acc, *[e[...] for e in e_refs])
            for o_ref, o in zip(o_refs, outs):
                o_ref[...] = o.astype(o_ref.dtype)

        if nk == 1:
            finish(part)
            return
        acc_ref = rest[n_e + n_o]
        k = pl.program_id(2)

        @pl.when(k == 0)
        def _():
            acc_ref[...] = part

        @pl.when(k > 0)
        def _():
            acc_ref[...] += part

        @pl.when(k == nk - 1)
        def _():
            finish(acc_ref[...])

    outs = pl.pallas_call(
        body, name=name, grid=(M // tm, N // tn, nk),
        in_specs=[a_spec, b_spec] + e_specs,
        out_specs=[pl.BlockSpec((tm, tn), lambda i, j, k: (i, j)) for _ in out_dtypes],
        out_shape=[jax.ShapeDtypeStruct((M, N), dt) for dt in out_dtypes],
        scratch_shapes=[pltpu.VMEM((tm, tn), F32)] if nk > 1 else [],
        compiler_params=_params("parallel", "parallel", "arbitrary"),
    )(a, b, *extras)
    return outs[0] if n_o == 1 else outs


def _row_tile(S):
    return _tile(S, 256, 8)


def _gain_spec(D, l):
    return pl.BlockSpec((None, 1, D), lambda i: (l, 0, 0))


def _rms(x, g):
    r = lax.rsqrt(jnp.mean(x * x, axis=-1, keepdims=True) + RMS_EPS)
    return x * r * g


def _rms_fwd(x, g3, l, name):
    S, D = x.shape
    tr = _row_tile(S)

    def body(x_ref, g_ref, h_ref):
        h_ref[...] = _rms(x_ref[...], g_ref[...]).astype(BF16)

    return pl.pallas_call(
        body, name=name, grid=(S // tr,),
        in_specs=[pl.BlockSpec((tr, D), lambda i: (i, 0)), _gain_spec(D, l)],
        out_specs=pl.BlockSpec((tr, D), lambda i: (i, 0)),
        out_shape=jax.ShapeDtypeStruct((S, D), BF16),
        compiler_params=_params("parallel"),
    )(x, g3)


def _post_res_fwd(x_in, f, gpost3, l, gnext3, lnext, name):
    S, D = x_in.shape
    tr = _row_tile(S)

    def body(x_ref, f_ref, gp_ref, gn_ref, xo_ref, h_ref):
        xo = x_ref[...] + _rms(f_ref[...], gp_ref[...])
        xo_ref[...] = xo
        h_ref[...] = _rms(xo, gn_ref[...]).astype(BF16)

    row = pl.BlockSpec((tr, D), lambda i: (i, 0))
    return pl.pallas_call(
        body, name=name, grid=(S // tr,),
        in_specs=[row, row, _gain_spec(D, l), _gain_spec(D, lnext)],
        out_specs=[row, row],
        out_shape=[jax.ShapeDtypeStruct((S, D), F32), jax.ShapeDtypeStruct((S, D), BF16)],
        compiler_params=_params("parallel"),
    )(x_in, f, gpost3, gnext3)


def _final_fwd_loss(x_in, f, gpost3, l, target, name):
    S, D = x_in.shape
    tr = _row_tile(S)

    def body(x_ref, f_ref, gp_ref, t_ref, dx_ref, loss_ref):
        @pl.when(pl.program_id(0) == 0)
        def _():
            loss_ref[...] = jnp.zeros_like(loss_ref)

        err = x_ref[...] + _rms(f_ref[...], gp_ref[...]) - t_ref[...]
        dx_ref[...] = err * (1.0 / D)
        loss_ref[...] += 0.5 * jnp.sum(jnp.mean(err * err, axis=-1, keepdims=True))

    row = pl.BlockSpec((tr, D), lambda i: (i, 0))
    return pl.pallas_call(
        body, name=name, grid=(S // tr,),
        in_specs=[row, row, _gain_spec(D, l), row],
        out_specs=[row, pl.BlockSpec((8, LANE), lambda i: (0, 0))],
        out_shape=[jax.ShapeDtypeStruct((S, D), F32), jax.ShapeDtypeStruct((8, LANE), F32)],
        compiler_params=_params("arbitrary"),
    )(x_in, f, gpost3, target)


def _rms_bwd_rows(dy, x, g):
    r = lax.rsqrt(jnp.mean(x * x, axis=-1, keepdims=True) + RMS_EPS)
    t = dy * g
    dx = r * t - x * (r * r * r) * jnp.mean(t * x, axis=-1, keepdims=True)
    return dx, dy * x * r


def _post_bwd(dxo, f, gpost3, l, name):
    S, D = f.shape
    tr = _row_tile(S)

    def body(d_ref, f_ref, g_ref, df_ref, dg_ref):
        @pl.when(pl.program_id(0) == 0)
        def _():
            dg_ref[...] = jnp.zeros_like(dg_ref)

        df, dg = _rms_bwd_rows(d_ref[...], f_ref[...], g_ref[...])
        df_ref[...] = df.astype(BF16)
        dg_ref[...] += jnp.sum(dg, axis=0, keepdims=True)

    row = pl.BlockSpec((tr, D), lambda i: (i, 0))
    return pl.pallas_call(
        body, name=name, grid=(S // tr,),
        in_specs=[row, row, _gain_spec(D, l)],
        out_specs=[row, pl.BlockSpec((1, D), lambda i: (0, 0))],
        out_shape=[jax.ShapeDtypeStruct((S, D), BF16), jax.ShapeDtypeStruct((1, D), F32)],
        compiler_params=_params("arbitrary"),
    )(dxo, f, gpost3)


def _pre_bwd(dxo, dh, x_in, gpre3, l, name):
    S, D = x_in.shape
    tr = _row_tile(S)

    def body(d_ref, dh_ref, x_ref, g_ref, dx_ref, dg_ref):
        @pl.when(pl.program_id(0) == 0)
        def _():
            dg_ref[...] = jnp.zeros_like(dg_ref)

        dx, dg = _rms_bwd_rows(dh_ref[...], x_ref[...], g_ref[...])
        dx_ref[...] = d_ref[...] + dx
        dg_ref[...] += jnp.sum(dg, axis=0, keepdims=True)

    row = pl.BlockSpec((tr, D), lambda i: (i, 0))
    return pl.pallas_call(
        body, name=name, grid=(S // tr,),
        in_specs=[row, row, row, _gain_spec(D, l)],
        out_specs=[row, pl.BlockSpec((1, D), lambda i: (0, 0))],
        out_shape=[jax.ShapeDtypeStruct((S, D), F32), jax.ShapeDtypeStruct((1, D), F32)],
        compiler_params=_params("arbitrary"),
    )(dxo, dh, x_in, gpre3)


def _zero_pads(pad_ref, S):
    z = jnp.zeros((CONV_PAD, pad_ref.shape[1]), F32)
    pad_ref[pl.ds(0, CONV_PAD), :] = z
    pad_ref[pl.ds(CONV_PAD + S, CONV_PAD), :] = z


def _conv_fwd_chunk(pad_ref, w_ref, K, r0, rows):
    acc = None
    for k in range(K):
        term = w_ref[pl.ds(k, 1), :] * pad_ref[pl.ds(CONV_PAD + r0 - (K - 1) + k, rows), :]
        acc = term if acc is None else acc + term
    return acc


def _conv_bwd_chunk(pad_ref, w_ref, K, r0, rows):
    acc = None
    for k in range(K):
        term = w_ref[pl.ds(k, 1), :] * pad_ref[pl.ds(CONV_PAD + r0 + (K - 1) - k, rows), :]
        acc = term if acc is None else acc + term
    return acc


def _conv_dw(upad_ref, dy_ref_or_pad, dy_off, K, S, dw_ref):
    rows = min(CONV_ROWS, S)
    for k in range(K):
        acc = None
        for r0 in range(0, S, rows):
            term = jnp.sum(dy_ref_or_pad[pl.ds(dy_off + r0, rows), :]
                           * upad_ref[pl.ds(CONV_PAD + r0 - (K - 1) + k, rows), :], axis=0, keepdims=True)
            acc = term if acc is None else acc + term
        dw_ref[pl.ds(k, 1), :] = acc


def _col_spec(S, off):
    return pl.BlockSpec((S, LANE), lambda j: (0, off // LANE + j))


def _sc_fwd(proj, conv_w, l, offs, DS, name):
    S = proj.shape[0]
    K = conv_w.shape[1]
    rows = min(CONV_ROWS, S)

    def body(b_ref, c_ref, u_ref, w_ref, o_ref, pad_ref):
        _zero_pads(pad_ref, S)
        pad_ref[pl.ds(CONV_PAD, S), :] = c_ref[...].astype(F32) * u_ref[...].astype(F32)
        for r0 in range(0, S, rows):
            cv = _conv_fwd_chunk(pad_ref, w_ref, K, r0, rows)
            o_ref[pl.ds(r0, rows), :] = (b_ref[pl.ds(r0, rows), :].astype(F32) * cv).astype(BF16)

    return pl.pallas_call(
        body, name=name, grid=(DS // LANE,),
        in_specs=[_col_spec(S, offs["sc_b"]), _col_spec(S, offs["sc_c"]), _col_spec(S, offs["sc_u"]),
                  pl.BlockSpec((None, K, LANE), lambda j: (l, 0, j))],
        out_specs=pl.BlockSpec((S, LANE), lambda j: (0, j)),
        out_shape=jax.ShapeDtypeStruct((S, DS), BF16),
        scratch_shapes=[pltpu.VMEM((S + 2 * CONV_PAD, LANE), F32)],
        compiler_params=_params("parallel"),
    )(proj, proj, proj, conv_w)


def _sc_bwd(dga, proj, conv_w, l, offs, DS, name):
    S = proj.shape[0]
    K = conv_w.shape[1]
    rows = min(CONV_ROWS, S)

    def body(d_ref, b_ref, c_ref, u_ref, w_ref, db_ref, dc_ref, du_ref, dw_ref, tpad_ref, gpad_ref):
        _zero_pads(tpad_ref, S)
        _zero_pads(gpad_ref, S)
        tpad_ref[pl.ds(CONV_PAD, S), :] = c_ref[...].astype(F32) * u_ref[...].astype(F32)
        for r0 in range(0, S, rows):
            sl = pl.ds(r0, rows)
            cv = _conv_fwd_chunk(tpad_ref, w_ref, K, r0, rows)
            d = d_ref[sl, :]
            db_ref[sl, :] = (d * cv).astype(BF16)
            gpad_ref[pl.ds(CONV_PAD + r0, rows), :] = d * b_ref[sl, :].astype(F32)
        for r0 in range(0, S, rows):
            sl = pl.ds(r0, rows)
            dt = _conv_bwd_chunk(gpad_ref, w_ref, K, r0, rows)
            dc_ref[sl, :] = (dt * u_ref[sl, :].astype(F32)).astype(BF16)
            du_ref[sl, :] = (dt * c_ref[sl, :].astype(F32)).astype(BF16)
        _conv_dw(tpad_ref, gpad_ref, CONV_PAD, K, S, dw_ref)

    blk = pl.BlockSpec((S, LANE), lambda j: (0, j))
    act = jax.ShapeDtypeStruct((S, DS), BF16)
    return pl.pallas_call(
        body, name=name, grid=(DS // LANE,),
        in_specs=[blk, _col_spec(S, offs["sc_b"]), _col_spec(S, offs["sc_c"]), _col_spec(S, offs["sc_u"]),
                  pl.BlockSpec((None, K, LANE), lambda j: (l, 0, j))],
        out_specs=[blk, blk, blk, pl.BlockSpec((K, LANE), lambda j: (0, j))],
        out_shape=[act, act, act, jax.ShapeDtypeStruct((K, DS), F32)],
        scratch_shapes=[pltpu.VMEM((S + 2 * CONV_PAD, LANE), F32), pltpu.VMEM((S + 2 * CONV_PAD, LANE), F32)],
        compiler_params=_params("parallel"),
    )(dga, proj, proj, proj, conv_w)


def _cf_conv_fwd(proj, conv_w, conv_b3, l, offs, DC, name):
    S = proj.shape[0]
    K = conv_w.shape[1]
    rows = min(CONV_ROWS, S)

    def body(a_ref, g_ref, w_ref, bias_ref, o_ref, pad_ref):
        _zero_pads(pad_ref, S)
        pad_ref[pl.ds(CONV_PAD, S), :] = a_ref[...].astype(F32) * _sigmoid(g_ref[...].astype(F32))
        for r0 in range(0, S, rows):
            o_ref[pl.ds(r0, rows), :] = _conv_fwd_chunk(pad_ref, w_ref, K, r0, rows) + bias_ref[...]

    return pl.pallas_call(
        body, name=name, grid=(DC // LANE,),
        in_specs=[_col_spec(S, offs["cf_a"]), _col_spec(S, offs["cf_g"]),
                  pl.BlockSpec((None, K, LANE), lambda j: (l, 0, j)),
                  pl.BlockSpec((None, 1, LANE), lambda j: (l, 0, j))],
        out_specs=pl.BlockSpec((S, LANE), lambda j: (0, j)),
        out_shape=jax.ShapeDtypeStruct((S, DC), F32),
        scratch_shapes=[pltpu.VMEM((S + 2 * CONV_PAD, LANE), F32)],
        compiler_params=_params("parallel"),
    )(proj, proj, conv_w, conv_b3)


def _layer_norm_hat(u):
    mu = jnp.mean(u, axis=-1, keepdims=True)
    xc = u - mu
    rstd = lax.rsqrt(jnp.mean(xc * xc, axis=-1, keepdims=True) + LN_EPS)
    return xc * rstd, rstd


def _cf_norm_fwd(u1, gam3, bet3, l, name):
    S, DC = u1.shape
    tr = _row_tile(S)

    def body(u_ref, g_ref, b_ref, o_ref):
        xhat, _ = _layer_norm_hat(u_ref[...])
        s = xhat * g_ref[...] + b_ref[...]
        o_ref[...] = (s * _sigmoid(s)).astype(BF16)

    row = pl.BlockSpec((tr, DC), lambda i: (i, 0))
    vec = pl.BlockSpec((None, 1, DC), lambda i: (l, 0, 0))
    return pl.pallas_call(
        body, name=name, grid=(S // tr,),
        in_specs=[row, vec, vec], out_specs=row,
        out_shape=jax.ShapeDtypeStruct((S, DC), BF16),
        compiler_params=_params("parallel"),
    )(u1, gam3, bet3)


def _cf_norm_bwd(du2, u1, gam3, bet3, l, name):
    S, DC = u1.shape
    tr = _row_tile(S)

    def body(d_ref, u_ref, g_ref, b_ref, du_ref, dg_ref, db_ref):
        @pl.when(pl.program_id(0) == 0)
        def _():
            dg_ref[...] = jnp.zeros_like(dg_ref)
            db_ref[...] = jnp.zeros_like(db_ref)

        xhat, rstd = _layer_norm_hat(u_ref[...])
        s = xhat * g_ref[...] + b_ref[...]
        sg = _sigmoid(s)
        ds = d_ref[...] * (sg * (1.0 + s * (1.0 - sg)))
        dg_ref[...] += jnp.sum(ds * xhat, axis=0, keepdims=True)
        db_ref[...] += jnp.sum(ds, axis=0, keepdims=True)
        dxh = ds * g_ref[...]
        du_ref[...] = rstd * (dxh - jnp.mean(dxh, axis=-1, keepdims=True)
                              - xhat * jnp.mean(dxh * xhat, axis=-1, keepdims=True))

    row = pl.BlockSpec((tr, DC), lambda i: (i, 0))
    vec = pl.BlockSpec((None, 1, DC), lambda i: (l, 0, 0))
    acc = pl.BlockSpec((1, DC), lambda i: (0, 0))
    return pl.pallas_call(
        body, name=name, grid=(S // tr,),
        in_specs=[row, row, vec, vec], out_specs=[row, acc, acc],
        out_shape=[jax.ShapeDtypeStruct((S, DC), F32), jax.ShapeDtypeStruct((1, DC), F32),
                   jax.ShapeDtypeStruct((1, DC), F32)],
        compiler_params=_params("arbitrary"),
    )(du2, u1, gam3, bet3)


def _cf_conv_bwd(du1, proj, conv_w, l, offs, DC, name):
    S = proj.shape[0]
    K = conv_w.shape[1]
    rows = min(CONV_ROWS, S)

    def body(d_ref, a_ref, g_ref, w_ref, da_ref, dgl_ref, dw_ref, dbias_ref, upad_ref, dpad_ref):
        _zero_pads(upad_ref, S)
        _zero_pads(dpad_ref, S)
        upad_ref[pl.ds(CONV_PAD, S), :] = a_ref[...].astype(F32) * _sigmoid(g_ref[...].astype(F32))
        dpad_ref[pl.ds(CONV_PAD, S), :] = d_ref[...]
        dbias_ref[...] = jnp.sum(d_ref[...], axis=0, keepdims=True)
        for r0 in range(0, S, rows):
            sl = pl.ds(r0, rows)
            du0 = _conv_bwd_chunk(dpad_ref, w_ref, K, r0, rows)
            a = a_ref[sl, :].astype(F32)
            sg = _sigmoid(g_ref[sl, :].astype(F32))
            da_ref[sl, :] = (du0 * sg).astype(BF16)
            dgl_ref[sl, :] = (du0 * a * sg * (1.0 - sg)).astype(BF16)
        _conv_dw(upad_ref, dpad_ref, CONV_PAD, K, S, dw_ref)

    blk = pl.BlockSpec((S, LANE), lambda j: (0, j))
    act = jax.ShapeDtypeStruct((S, DC), BF16)
    return pl.pallas_call(
        body, name=name, grid=(DC // LANE,),
        in_specs=[blk, _col_spec(S, offs["cf_a"]), _col_spec(S, offs["cf_g"]),
                  pl.BlockSpec((None, K, LANE), lambda j: (l, 0, j))],
        out_specs=[blk, blk, pl.BlockSpec((K, LANE), lambda j: (0, j)), pl.BlockSpec((1, LANE), lambda j: (0, j))],
        out_shape=[act, act, jax.ShapeDtypeStruct((K, DC), F32), jax.ShapeDtypeStruct((1, DC), F32)],
        scratch_shapes=[pltpu.VMEM((S + 2 * CONV_PAD, LANE), F32), pltpu.VMEM((S + 2 * CONV_PAD, LANE), F32)],
        compiler_params=_params("parallel"),
    )(du1, proj, proj, conv_w)


def _dot_nt(a, b):
    return lax.dot_general(a, b, (((1,), (1,)), ((), ())), preferred_element_type=F32)


def _dot_nn(a, b):
    return lax.dot_general(a, b, (((1,), (0,)), ((), ())), preferred_element_type=F32)


def _dot_tn(a, b):
    return lax.dot_general(a, b, (((0,), (0,)), ((), ())), preferred_element_type=F32)


def _dot_split(x, u):
    hi = x.astype(BF16)
    lo = (x - hi.astype(F32)).astype(BF16)
    return _dot_nn(hi, u) + _dot_nn(lo, u)


def _sb_scores(q, k, valid):
    z = _dot_nt(q, k) * (HEAD_DIM ** -0.5)
    lf = -(jnp.maximum(z, 0.0) + jnp.log(1.0 + jnp.exp(-jnp.abs(z))))
    return z, jnp.where(valid, lf, 0.0)


def _head_group(H, want):
    g = min(want, H)
    while H % g:
        g -= 1
    return g


def _lanes(g):
    return slice(g * HEAD_DIM, (g + 1) * HEAD_DIM)


def _attn_fwd(proj, offs, DA, name):
    S = proj.shape[0]
    H = DA // HEAD_DIM
    G = _head_group(H, ATTN_FWD_HEADS)
    nb = S // QB

    def body(q_ref, k_ref, v_ref, o_ref, tot_ref):
        row = lax.broadcasted_iota(jnp.int32, (QB, QB), 0)
        col = lax.broadcasted_iota(jnp.int32, (QB, QB), 1)
        u_after = (row > col).astype(BF16)

        def q_block(i, _):
            qs = pl.ds(pl.multiple_of(i * QB, QB), QB)
            qg = [q_ref[qs, _lanes(g)] for g in range(G)]

            def k_block(jj, carry):
                ks = pl.ds(pl.multiple_of((i - jj) * QB, QB), QB)
                valid = col < row + jj * QB
                out = []
                for g in range(G):
                    acc, c = carry[g]
                    z, lf = _sb_scores(qg[g], k_ref[ks, _lanes(g)], valid)
                    log_a = lf + z + _dot_split(lf, u_after) + c
                    a = jnp.where(valid, jnp.exp(log_a), 0.0)
                    out.append((acc + _dot_nn(a.astype(BF16), v_ref[ks, _lanes(g)]),
                                c + jnp.sum(lf, axis=1, keepdims=True)))
                return tuple(out)

            init = tuple((jnp.zeros((QB, HEAD_DIM), F32), jnp.zeros((QB, 1), F32)) for _ in range(G))
            res = lax.fori_loop(0, i + 1, k_block, init)
            for g in range(G):
                o_ref[qs, _lanes(g)] = res[g][0].astype(BF16)
                tot_ref[g, qs, :] = res[g][1]
            return 0

        lax.fori_loop(0, nb, q_block, 0)

    def hs(off):
        return pl.BlockSpec((S, G * HEAD_DIM), lambda h: (0, off // (G * HEAD_DIM) + h))

    return pl.pallas_call(
        body, name=name, grid=(H // G,),
        in_specs=[hs(offs["q"]), hs(offs["k"]), hs(offs["v"])],
        out_specs=[pl.BlockSpec((S, G * HEAD_DIM), lambda h: (0, h)), pl.BlockSpec((G, S, 1), lambda h: (h, 0, 0))],
        out_shape=[jax.ShapeDtypeStruct((S, DA), BF16), jax.ShapeDtypeStruct((H, S, 1), F32)],
        compiler_params=_params("parallel"),
    )(proj, proj, proj)


def _attn_bwd(dout, tot, proj, offs, DA, name):
    S = proj.shape[0]
    H = DA // HEAD_DIM
    G = _head_group(H, ATTN_BWD_HEADS)
    nb = S // QB
    scale = HEAD_DIM ** -0.5

    def body(q_ref, k_ref, v_ref, tot_ref, do_ref, dq_ref, dk_ref, dv_ref, dk_acc, dv_acc):
        row = lax.broadcasted_iota(jnp.int32, (QB, QB), 0)
        col = lax.broadcasted_iota(jnp.int32, (QB, QB), 1)
        u_after = (row > col).astype(BF16)
        u_before = (row < col).astype(BF16)
        dk_acc[...] = jnp.zeros_like(dk_acc)
        dv_acc[...] = jnp.zeros_like(dv_acc)

        def q_block(i, _):
            qs = pl.ds(pl.multiple_of(i * QB, QB), QB)
            qg = [q_ref[qs, _lanes(g)] for g in range(G)]
            dog = [do_ref[qs, _lanes(g)] for g in range(G)]
            totg = [tot_ref[g, qs, :] for g in range(G)]

            def k_block(j, carry):
                ks = pl.ds(pl.multiple_of(j * QB, QB), QB)
                valid = col < row + (i - j) * QB
                out = []
                for g in range(G):
                    dq, seen, gsum = carry[g]
                    k = k_ref[ks, _lanes(g)]
                    z, lf = _sb_scores(qg[g], k, valid)
                    seen = seen + jnp.sum(lf, axis=1, keepdims=True)
                    log_a = lf + z + _dot_split(lf, u_after) + (totg[g] - seen)
                    a = jnp.where(valid, jnp.exp(log_a), 0.0)
                    dlog = a * _dot_nt(dog[g], v_ref[ks, _lanes(g)])
                    before = gsum + _dot_split(dlog, u_before)
                    dz = jnp.where(valid, dlog * jnp.exp(lf) - before * jnp.exp(lf + z), 0.0) * scale
                    dzb = dz.astype(BF16)
                    dk_acc[ks, _lanes(g)] += _dot_tn(dzb, qg[g])
                    dv_acc[ks, _lanes(g)] += _dot_tn(a.astype(BF16), dog[g])
                    out.append((dq + _dot_nn(dzb, k), seen, gsum + jnp.sum(dlog, axis=1, keepdims=True)))
                return tuple(out)

            zero = jnp.zeros((QB, 1), F32)
            res = lax.fori_loop(0, i + 1, k_block, tuple((jnp.zeros((QB, HEAD_DIM), F32), zero, zero) for _ in range(G)))
            for g in range(G):
                dq_ref[qs, _lanes(g)] = res[g][0].astype(BF16)
            return 0

        lax.fori_loop(0, nb, q_block, 0)
        dk_ref[...] = dk_acc[...].astype(BF16)
        dv_ref[...] = dv_acc[...].astype(BF16)

    def hs(off):
        return pl.BlockSpec((S, G * HEAD_DIM), lambda h: (0, off // (G * HEAD_DIM) + h))

    head = pl.BlockSpec((S, G * HEAD_DIM), lambda h: (0, h))
    act = jax.ShapeDtypeStruct((S, DA), BF16)
    return pl.pallas_call(
        body, name=name, grid=(H // G,),
        in_specs=[hs(offs["q"]), hs(offs["k"]), hs(offs["v"]), pl.BlockSpec((G, S, 1), lambda h: (h, 0, 0)), head],
        out_specs=[head, head, head], out_shape=[act, act, act],
        scratch_shapes=[pltpu.VMEM((S, G * HEAD_DIM), F32), pltpu.VMEM((S, G * HEAD_DIM), F32)],
        compiler_params=_params("parallel"),
    )(proj, proj, proj, tot, dout)


def _merge_tiles(S, D, goff):
    tn = LANE
    for t in range(LANE, 513, LANE):
        if D % t == 0 and goff % t == 0:
            tn = t
    return _tile(S, 512, 8), tn


def _merge_fwd(ga, attn, u2, pa, pb, pc, proj, goff, name):
    S = ga.shape[0]
    D = pa.shape[-1]
    tm, tn = _merge_tiles(S, D, goff)

    def body(ga_ref, at_ref, u2_ref, pa_ref, pb_ref, pc_ref, la_ref, lb_ref, lc_ref, ya_ref, yb_ref, yc_ref, m_ref):
        ya = _dot_nn(ga_ref[...], pa_ref[...])
        yb = _dot_nn(at_ref[...], pb_ref[...])
        yc = _dot_nn(u2_ref[...], pc_ref[...])
        ya_ref[...] = ya.astype(BF16)
        yb_ref[...] = yb.astype(BF16)
        yc_ref[...] = yc.astype(BF16)
        m_ref[...] = (_sigmoid(la_ref[...].astype(F32)) * ya + _sigmoid(lb_ref[...].astype(F32)) * yb
                      + _sigmoid(lc_ref[...].astype(F32)) * yc).astype(BF16)

    def lhs(a):
        return pl.BlockSpec((tm, a.shape[1]), lambda i, j: (i, 0))

    def rhs(p):
        return pl.BlockSpec((p.shape[0], tn), lambda i, j: (0, j))

    def gate(r):
        return pl.BlockSpec((tm, tn), lambda i, j: (i, (goff + r * D) // tn + j))

    out = pl.BlockSpec((tm, tn), lambda i, j: (i, j))
    act = jax.ShapeDtypeStruct((S, D), BF16)
    return pl.pallas_call(
        body, name=name, grid=(S // tm, D // tn),
        in_specs=[lhs(ga), lhs(attn), lhs(u2), rhs(pa), rhs(pb), rhs(pc), gate(0), gate(1), gate(2)],
        out_specs=[out, out, out, out], out_shape=[act, act, act, act],
        compiler_params=_params("parallel", "parallel"),
    )(ga, attn, u2, pa, pb, pc, proj, proj, proj)


def _merge_bwd(dm, ya, yb, yc, proj, goff, name):
    S, D = dm.shape
    tm, tn = _merge_tiles(S, D, goff)

    def body(dm_ref, ya_ref, yb_ref, yc_ref, la_ref, lb_ref, lc_ref, *o_refs):
        dmv = dm_ref[...]
        for y_ref, l_ref, dy_ref, dl_ref in zip((ya_ref, yb_ref, yc_ref), (la_ref, lb_ref, lc_ref),
                                                o_refs[:3], o_refs[3:]):
            sg = _sigmoid(l_ref[...].astype(F32))
            dy_ref[...] = (dmv * sg).astype(BF16)
            dl_ref[...] = (dmv * y_ref[...].astype(F32) * sg * (1.0 - sg)).astype(BF16)

    def gate(r):
        return pl.BlockSpec((tm, tn), lambda i, j: (i, (goff + r * D) // tn + j))

    blk = pl.BlockSpec((tm, tn), lambda i, j: (i, j))
    act = jax.ShapeDtypeStruct((S, D), BF16)
    return pl.pallas_call(
        body, name=name, grid=(S // tm, D // tn),
        in_specs=[blk, blk, blk, blk, gate(0), gate(1), gate(2)],
        out_specs=[blk] * 6, out_shape=[act] * 6,
        compiler_params=_params("parallel", "parallel"),
    )(dm, ya, yb, yc, proj, proj, proj)


def _ew_tiles(rows, cols):
    tc = cols if cols <= 4096 else _tile(cols, 2048)
    tr = _tile(rows, max(8, (1 << 19) // tc), 8)
    return tr, tc


def _half_rows_tile(Rh, Cs):
    return _tile(Rh, max(16, (1 << 19) // Cs), 16)


def _place_shard(w, l, kind, place, name):
    _, Rs, Cs = w.shape
    tr = _half_rows_tile(Rs, Cs)

    def body(pr_ref, w_ref, o_ref):
        o_ref[...] = w_ref[...].astype(BF16)

    if kind == "col":
        shape = (Rs, 4 * Cs)
        o_spec = pl.BlockSpec((tr, Cs), lambda i, pr: (i, pr[1]))
    else:
        shape = (4, Rs, Cs)
        o_spec = pl.BlockSpec((None, tr, Cs), lambda i, pr: (pr[1], i, 0))
    out = pl.pallas_call(
        body, name=name,
        grid_spec=pltpu.PrefetchScalarGridSpec(
            num_scalar_prefetch=1, grid=(Rs // tr,),
            in_specs=[pl.BlockSpec((None, tr, Cs), lambda i, pr: (l, i, 0))], out_specs=o_spec),
        out_shape=jax.ShapeDtypeStruct(shape, BF16), compiler_params=_params("parallel"),
    )(place, w)
    return out if kind == "col" else out.reshape(4 * Rs, Cs)


def _pair_sum(g, got, kind, place, name):
    _, Rh, Cs = got.shape
    tr = _half_rows_tile(Rh, Cs)
    if kind == "col":
        gv = g.reshape(2, Rh, 4 * Cs)
        g_spec = pl.BlockSpec((None, tr, Cs), lambda p, i, pr: (pr[0], i, p))
    else:
        gv = g.reshape(4, 2, Rh, Cs)
        g_spec = pl.BlockSpec((None, None, tr, Cs), lambda p, i, pr: (p, pr[0], i, 0))
    blk = pl.BlockSpec((None, tr, Cs), lambda p, i, pr: (p, i, 0))

    def body(pr_ref, g_ref, r_ref, o_ref):
        o_ref[...] = (g_ref[...].astype(F32) + r_ref[...].astype(F32)).astype(BF16)

    return pl.pallas_call(
        body, name=name,
        grid_spec=pltpu.PrefetchScalarGridSpec(num_scalar_prefetch=1, grid=(4, Rh // tr),
                                               in_specs=[g_spec, blk], out_specs=blk),
        out_shape=jax.ShapeDtypeStruct(got.shape, BF16), compiler_params=_params("parallel", "parallel"),
    )(place, gv, got)


def _chip_sum(part, got, place, stack, name):
    n, Rh, Cs = got.shape
    l, L, buf = stack
    tr = _half_rows_tile(Rh, Cs)
    prev = () if buf is None else (buf,)

    def body(pr_ref, p_ref, r_ref, *rest):
        acc = p_ref[...].astype(F32)
        for s in range(n):
            acc = acc + r_ref[s].astype(F32)
        rest[-1][...] = acc

    return pl.pallas_call(
        body, name=name,
        grid_spec=pltpu.PrefetchScalarGridSpec(
            num_scalar_prefetch=1, grid=(Rh // tr,),
            in_specs=[pl.BlockSpec((None, tr, Cs), lambda i, pr: (pr[1], i, 0)),
                      pl.BlockSpec((n, tr, Cs), lambda i, pr: (0, i, 0))] + [ANY] * len(prev),
            out_specs=pl.BlockSpec((None, None, tr, Cs), lambda i, pr: (l, pr[0], i, 0))),
        out_shape=jax.ShapeDtypeStruct((L, 2, Rh, Cs), F32), input_output_aliases={3: 0} if prev else {},
        compiler_params=_params("parallel"),
    )(place, part, got, *prev)


def _adamw(w, g, m, v, name):
    shape = w.shape
    args = [t.reshape(-1, shape[-1]) for t in (w, g, m, v)]
    rows, cols = args[0].shape
    tr, tc = _ew_tiles(rows, cols)
    c1 = 1.0 - ADAM_B1 ** ADAM_STEP
    c2 = 1.0 - ADAM_B2 ** ADAM_STEP

    def body(w_ref, g_ref, m_ref, v_ref, d_ref, nm_ref, nv_ref):
        gv = g_ref[...]
        nm = ADAM_B1 * m_ref[...] + (1.0 - ADAM_B1) * gv
        nv = ADAM_B2 * v_ref[...] + (1.0 - ADAM_B2) * (gv * gv)
        nm_ref[...] = nm
        nv_ref[...] = nv
        d_ref[...] = -ADAM_LR * ((nm / c1) / (jnp.sqrt(nv / c2) + ADAM_EPS) + ADAM_WD * w_ref[...])

    blk = pl.BlockSpec((tr, tc), lambda i, j: (i, j))
    shp = jax.ShapeDtypeStruct((rows, cols), F32)
    outs = pl.pallas_call(
        body, name=name, grid=(rows // tr, cols // tc), in_specs=[blk] * 4, out_specs=[blk] * 3,
        out_shape=[shp] * 3, compiler_params=_params("parallel", "parallel"),
    )(*args)
    return [o.reshape(shape) for o in outs]


def _place():
    x, y, c = lax.axis_index("x"), lax.axis_index("y"), lax.axis_index("c")
    chips = [(1 - x, y), (x, 1 - y), (1 - x, 1 - y)]
    return x, y, c, chips


def _al(v, unit):
    return pl.multiple_of(v, unit) if unit % LANE == 0 else v


def _half_of_full(ref, kind, p, half, Rs, Cs):
    Rh = (ref.shape[-2] // 2) if kind == "col" else Rs // 2
    lead = (slice(None),) * (len(ref.shape) - 2)
    if kind == "col":
        return ref.at[lead + (pl.ds(_al(half * Rh, Rh), Rh), pl.ds(_al(p * Cs, Cs), Cs))]
    return ref.at[lead + (pl.ds(_al(p * Rs + half * Rh, Rh), Rh), slice(None))]


HBM = pl.BlockSpec(memory_space=pltpu.HBM)
SEM = pl.BlockSpec(memory_space=pltpu.SEMAPHORE)
EFFECT = pltpu.SideEffectType.DATAFLOW_SIDE_EFFECTING
GATHER_ID = 0
REDUCE_ID = 1


def _in_hbm(v):
    return pltpu.with_memory_space_constraint(v, pltpu.HBM)


def _ici_handshake(chips, c):
    barrier = pltpu.get_barrier_semaphore()
    for px, py in chips:
        pl.semaphore_signal(barrier, inc=1, device_id=(px, py, c), device_id_type=MESH)
    pl.semaphore_wait(barrier, len(chips))


def _shard_dims(ref, kind):
    R, C = ref.shape[-2:]
    return (R, C // 4) if kind == "col" else (R // 4, C)


def _gather_start(groups, after):
    bufs = [b for grp in groups for b, _ in grp]
    kinds = [k for grp in groups for _, k in grp]
    m, ng = len(bufs), len(groups)

    def body(*refs):
        ins = refs[:m]
        sems = refs[m + 1:m + 1 + 2 * ng]
        x, y, c, chips = _place()
        _ici_handshake(chips, c)
        me = 2 * x + y
        at = 0
        for gi, grp in enumerate(groups):
            n = len(grp)
            for j, chip in enumerate(chips):
                for w in range(n):
                    ref, kind = ins[at + w], kinds[at + w]
                    mine = _half_of_full(ref, kind, me, c, *_shard_dims(ref, kind))
                    pltpu.make_async_remote_copy(mine, mine, sems[2 * gi].at[j * n + w], sems[2 * gi + 1].at[j * n + w],
                                                 device_id=(*chip, c), device_id_type=MESH).start()
            at += n

    sem_shapes = [pltpu.SemaphoreType.DMA((3 * len(grp),)) for grp in groups for _ in range(2)]
    outs = pl.pallas_call(
        body, name="gather_start", in_specs=[HBM] * m + [ANY], out_specs=[SEM] * (2 * ng) + [HBM] * m,
        out_shape=sem_shapes + [pltpu.HBM(b.shape, b.dtype) for b in bufs],
        input_output_aliases={i: 2 * ng + i for i in range(m)},
        compiler_params=pltpu.CompilerParams(has_side_effects=EFFECT, collective_id=GATHER_ID),
    )(*[_in_hbm(b) for b in bufs], after)
    res, at = [], 2 * ng
    for gi, grp in enumerate(groups):
        res.append((outs[2 * gi], outs[2 * gi + 1], outs[at:at + len(grp)]))
        at += len(grp)
    return res


def _gather_wait(bufs, kinds, send_sem, recv_sem, after, name):
    n = len(bufs)

    def body(*refs):
        ins = refs[:n]
        send, recv = refs[n], refs[n + 1]
        x, y, c, chips = _place()
        me = 2 * x + y
        for j, (px, py) in enumerate(chips):
            for w in range(n):
                dims = _shard_dims(ins[w], kinds[w])
                mine = _half_of_full(ins[w], kinds[w], me, c, *dims)
                theirs = _half_of_full(ins[w], kinds[w], 2 * px + py, c, *dims)
                cp = pltpu.make_async_remote_copy(mine, theirs, send.at[j * n + w], recv.at[j * n + w],
                                                  device_id=(px, py, c), device_id_type=MESH)
                cp.wait_send()
                cp.wait_recv()

    return pl.pallas_call(
        body, name=name, in_specs=[HBM] * n + [SEM, SEM, ANY], out_specs=[HBM] * n,
        out_shape=[pltpu.HBM(b.shape, b.dtype) for b in bufs],
        input_output_aliases={i: i for i in range(n)},
        compiler_params=pltpu.CompilerParams(has_side_effects=EFFECT),
    )(*bufs, send_sem, recv_sem, after)


def _gather_forward(bufs, kinds, name):
    n = len(bufs)

    def body(*refs):
        outs = refs[n:2 * n]
        send_sem, recv_sem = refs[2 * n:]
        x, y, c, chips = _place()
        sib = (x, y, 1 - c)
        copies = []
        for j, (px, py) in enumerate(chips):
            for w in range(n):
                got = _half_of_full(outs[w], kinds[w], 2 * px + py, c, *_shard_dims(outs[w], kinds[w]))
                cp = pltpu.make_async_remote_copy(got, got, send_sem.at[j * n + w], recv_sem.at[j * n + w],
                                                  device_id=sib, device_id_type=MESH)
                cp.start()
                copies.append(cp)
        for cp in copies:
            cp.wait()

    return pl.pallas_call(
        body, name=name, in_specs=[ANY] * n, out_specs=[ANY] * n,
        out_shape=[jax.ShapeDtypeStruct(b.shape, b.dtype) for b in bufs],
        input_output_aliases={w: w for w in range(n)},
        scratch_shapes=[pltpu.SemaphoreType.DMA((3 * n,))] * 2,
    )(*bufs)


def _reduce_to_sibling_halves(grads, kinds, name):
    n = len(grads)
    shapes = []
    for g, kind in zip(grads, kinds):
        R, C = g.shape
        shapes.append((4, R // 2, C // 4) if kind == "col" else (4, R // 8, C))

    def body(*refs):
        ins, gots = refs[:n], refs[n:2 * n]
        send_sem, recv_sem = refs[2 * n:]
        x, y, c, _ = _place()
        sib = (x, y, 1 - c)
        copies = []
        for w in range(n):
            _, Rh, Cs = shapes[w]
            for p in range(4):
                cp = pltpu.make_async_remote_copy(_half_of_full(ins[w], kinds[w], p, 1 - c, 2 * Rh, Cs),
                                                  gots[w].at[p], send_sem.at[4 * w + p], recv_sem.at[4 * w + p],
                                                  device_id=sib, device_id_type=MESH)
                cp.start()
                copies.append(cp)
        for cp in copies:
            cp.wait()

    return pl.pallas_call(
        body, name=name, in_specs=[ANY] * n, out_specs=[ANY] * n,
        out_shape=[jax.ShapeDtypeStruct(s, BF16) for s in shapes],
        scratch_shapes=[pltpu.SemaphoreType.DMA((4 * n,))] * 2,
    )(*grads)


def _reduce_start(parts, name):
    n = len(parts)
    zones = [lax.empty((3,) + p.shape[1:], p.dtype) for p in parts]

    def body(*refs):
        ins, lands = refs[:n], refs[n:2 * n]
        send, recv = refs[2 * n], refs[2 * n + 1]
        token = refs[-1]
        x, y, c, chips = _place()
        _ici_handshake(chips, c)
        for j, (px, py) in enumerate(chips):
            for w in range(n):
                pltpu.make_async_remote_copy(ins[w].at[2 * px + py], lands[w].at[j], send.at[j * n + w],
                                             recv.at[j * n + w], device_id=(px, py, c), device_id_type=MESH).start()
        token[...] = jnp.zeros_like(token)

    both = list(parts) + zones
    outs = pl.pallas_call(
        body, name=name, in_specs=[HBM] * (2 * n),
        out_specs=[SEM, SEM] + [HBM] * (2 * n) + [pl.BlockSpec(memory_space=pltpu.VMEM)],
        out_shape=([pltpu.SemaphoreType.DMA((3 * n,))] * 2 + [pltpu.HBM(b.shape, b.dtype) for b in both]
                   + [jax.ShapeDtypeStruct((8, LANE), F32)]),
        input_output_aliases={i: 2 + i for i in range(2 * n)},
        compiler_params=pltpu.CompilerParams(has_side_effects=EFFECT, collective_id=REDUCE_ID),
    )(*[_in_hbm(b) for b in both])
    return outs[0], outs[1], outs[2:2 + n], outs[2 + n:2 + 2 * n], outs[-1]


def _reduce_wait(parts, zones, send_sem, recv_sem, after, name):
    n = len(parts)

    def body(*refs):
        ins, lands = refs[:n], refs[n:2 * n]
        send, recv = refs[2 * n], refs[2 * n + 1]
        x, y, c, chips = _place()
        for j, (px, py) in enumerate(chips):
            for w in range(n):
                cp = pltpu.make_async_remote_copy(ins[w].at[2 * px + py], lands[w].at[j], send.at[j * n + w],
                                                  recv.at[j * n + w], device_id=(px, py, c), device_id_type=MESH)
                cp.wait_send()
                cp.wait_recv()

    both = list(parts) + list(zones)
    outs = pl.pallas_call(
        body, name=name, in_specs=[HBM] * (2 * n) + [SEM, SEM, ANY], out_specs=[HBM] * (2 * n),
        out_shape=[pltpu.HBM(b.shape, b.dtype) for b in both],
        input_output_aliases={i: i for i in range(2 * n)},
        compiler_params=pltpu.CompilerParams(has_side_effects=EFFECT),
    )(*both, send_sem, recv_sem, after)
    return outs[:n], outs[n:]


def _share_with_sibling(reduced):
    n = len(reduced)

    def body(*refs):
        outs = refs[n:2 * n]
        send_sem, recv_sem = refs[2 * n:]
        x, y, c, _ = _place()
        sib = (x, y, 1 - c)
        copies = []
        for w in range(n):
            mine = outs[w].at[:, c]
            cp = pltpu.make_async_remote_copy(mine, mine, send_sem.at[w], recv_sem.at[w], device_id=sib,
                                              device_id_type=MESH)
            cp.start()
            copies.append(cp)
        for cp in copies:
            cp.wait()

    return pl.pallas_call(
        body, name="reduce_share", in_specs=[ANY] * n, out_specs=[ANY] * n,
        out_shape=[jax.ShapeDtypeStruct(r.shape, r.dtype) for r in reduced],
        input_output_aliases={w: w for w in range(n)},
        scratch_shapes=[pltpu.SemaphoreType.DMA((n,))] * 2,
    )(*reduced)


def _all_gather_small(v):
    r = v.shape[0]

    def body(v_ref, o_ref, send_sem, recv_sem):
        x, y, c, _ = _place()
        me = 4 * x + 2 * y + c
        o_ref[me] = v_ref[...]
        copies = []
        for k in range(1, 8):
            peer = (x ^ (k >> 2), y ^ ((k >> 1) & 1), c ^ (k & 1))
            cp = pltpu.make_async_remote_copy(v_ref, o_ref.at[me], send_sem.at[k - 1], recv_sem.at[k - 1],
                                              device_id=peer, device_id_type=MESH)
            cp.start()
            copies.append(cp)
        for cp in copies:
            cp.wait()

    vmem = pl.BlockSpec(memory_space=pltpu.VMEM)
    return pl.pallas_call(
        body, name="all_gather_small", in_specs=[vmem], out_specs=vmem,
        out_shape=jax.ShapeDtypeStruct((8, r, LANE), F32),
        scratch_shapes=[pltpu.SemaphoreType.DMA((7,)), pltpu.SemaphoreType.DMA((7,))],
        compiler_params=pltpu.CompilerParams(vmem_limit_bytes=VMEM_LIMIT),
    )(v)


def _sum_slots(g):
    n, r, _ = g.shape

    def body(g_ref, o_ref):
        acc = g_ref[0]
        for s in range(1, n):
            acc = acc + g_ref[s]
        o_ref[...] = acc

    vmem = pl.BlockSpec(memory_space=pltpu.VMEM)
    return pl.pallas_call(
        body, name="sum_slots", in_specs=[vmem], out_specs=vmem, out_shape=jax.ShapeDtypeStruct((r, LANE), F32),
        compiler_params=pltpu.CompilerParams(vmem_limit_bytes=VMEM_LIMIT),
    )(g)


def _pack(arrays):
    flat = jnp.concatenate([a.reshape(-1) for a in arrays])
    pad = (-flat.shape[0]) % (8 * LANE)
    return jnp.pad(flat, (0, pad)).reshape(-1, LANE)


def _unpack(packed, like):
    flat = packed.reshape(-1)
    out, off = [], 0
    for a in like:
        out.append(flat[off:off + a.size].reshape(a.shape))
        off += a.size
    return out


def _offsets(D):
    DA, DS, DC = D // 2, D // 4, D // 4
    names = ["q", "k", "v", "sc_b", "sc_c", "sc_u", "cf_a", "cf_g", "gate"]
    sizes = [DA, DA, DA, DS, DS, DS, DC, DC, 3 * D]
    offs, o = {}, 0
    for nm, sz in zip(names, sizes):
        offs[nm] = o
        o += sz
    return offs, DA, DS, DC


def _relu2_epilogue(acc):
    r = jnp.maximum(acc, 0.0)
    return acc, r * r


def _drelu2_epilogue(acc, up):
    return (acc * (2.0 * jnp.maximum(up.astype(F32), 0.0)),)


def _local_step(x, target, gains, conv_a_w, conv_c_w, conv_c_b, norm_c_g, norm_c_b, layer_weights, emit_grads):
    S, D = x.shape
    L = gains[0].shape[0]
    offs, DA, DS, DC = _offsets(D)
    goff = offs["gate"]
    g_mix_pre, g_mix_post, g_mlp_pre, g_mlp_post = gains
    cb3, ng3, nb3 = (t.reshape(L, 1, DC) for t in (conv_c_b, norm_c_g, norm_c_b))

    saved = []
    h = _rms_fwd(x, g_mix_pre, 0, "rms_first")
    xin = x
    for l in range(L):
        big = dict(layer_weights(l, "in", xin))
        proj = _mm(h, big["w_in"], out_dtypes=(BF16,), name=f"fwd_w_in_{l}", tn_cap=512)
        ga = _sc_fwd(proj, conv_a_w, l, offs, DS, f"sc_fwd_{l}")
        attn, attn_tot = _attn_fwd(proj, offs, DA, f"attn_fwd_{l}")
        u1 = _cf_conv_fwd(proj, conv_c_w, cb3, l, offs, DC, f"cf_conv_fwd_{l}")
        u2 = _cf_norm_fwd(u1, ng3, nb3, l, f"cf_norm_fwd_{l}")
        big.update(layer_weights(l, "rest", attn))
        ya, yb, yc, merged = _merge_fwd(ga, attn, u2, big["proj_a"], big["proj_b"], big["proj_c"], proj, goff,
                                        f"merge_fwd_{l}")
        mixed = _mm(merged, big["w_o"], name=f"fwd_w_o_{l}")
        x1, h2 = _post_res_fwd(xin, mixed, g_mix_post, l, g_mlp_pre, l, f"mix_residual_{l}")
        up, act = _mm(h2, big["w_up"], out_dtypes=(BF16, BF16), epilogue=_relu2_epilogue, name=f"fwd_w_up_{l}")
        f = _mm(act, big["w_down"], name=f"fwd_w_down_{l}")
        saved.append(dict(big=big, xin=xin, h=h, proj=proj, ga=ga, attn=attn, attn_tot=attn_tot, u1=u1, u2=u2, ya=ya,
                          yb=yb, yc=yc, merged=merged, mixed=mixed, x1=x1, h2=h2, up=up, act=act, f=f))
        if l + 1 < L:
            xin, h = _post_res_fwd(x1, f, g_mlp_post, l, g_mix_pre, l + 1, f"mlp_residual_{l}")
        else:
            dx, loss = _final_fwd_loss(x1, f, g_mlp_post, l, target, "loss_head")

    small = {k: [None] * L for k in ("mix_pre", "mix_post", "mlp_pre", "mlp_post", "conv_a_w", "conv_c_w",
                                       "conv_c_b", "norm_c_g", "norm_c_b")}

    def dw(key, a, b, l, **kw):
        return _mm(a, b, ta=True, out_dtypes=(BF16,), name=f"d{key}_{l}", **kw)

    for l in reversed(range(L)):
        s = saved[l]
        big = s["big"]
        g = {}
        df, small["mlp_post"][l] = _post_bwd(dx, s["f"], g_mlp_post, l, f"mlp_post_bwd_{l}")
        g["w_down"] = dw("w_down", s["act"], df, l)
        dup = _mm(df, big["w_down"], tb=True, out_dtypes=(BF16,), epilogue=_drelu2_epilogue, extras=(s["up"],),
                  name=f"d_up_{l}")
        g["w_up"] = dw("w_up", s["h2"], dup, l)
        dh2 = _mm(dup, big["w_up"], tb=True, name=f"d_h2_{l}")
        dx1, small["mlp_pre"][l] = _pre_bwd(dx, dh2, s["x1"], g_mlp_pre, l, f"mlp_pre_bwd_{l}")
        g_mix_post = g_mix_post + emit_grads(l, "mlp", g)[0, 0]
        g = {}
        dmixed, small["mix_post"][l] = _post_bwd(dx1, s["mixed"], g_mix_post, l, f"mix_post_bwd_{l}")
        g["w_o"] = dw("w_o", s["merged"], dmixed, l)
        dmerged = _mm(dmixed, big["w_o"], tb=True, name=f"d_merged_{l}")
        dya, dyb, dyc, dla, dlb, dlc = _merge_bwd(dmerged, s["ya"], s["yb"], s["yc"], s["proj"], goff,
                                                  f"merge_bwd_{l}")
        g["proj_a"] = dw("proj_a", s["ga"], dya, l)
        g["proj_b"] = dw("proj_b", s["attn"], dyb, l)
        g["proj_c"] = dw("proj_c", s["u2"], dyc, l)
        dga = _mm(dya, big["proj_a"], tb=True, name=f"d_ga_{l}")
        dattn = _mm(dyb, big["proj_b"], tb=True, out_dtypes=(BF16,), name=f"d_attn_{l}")
        du2 = _mm(dyc, big["proj_c"], tb=True, name=f"d_u2_{l}")
        dsb, dsc, dsu, small["conv_a_w"][l] = _sc_bwd(dga, s["proj"], conv_a_w, l, offs, DS, f"sc_bwd_{l}")
        du1, small["norm_c_g"][l], small["norm_c_b"][l] = _cf_norm_bwd(du2, s["u1"], ng3, nb3, l, f"cf_norm_bwd_{l}")
        dca, dcg, small["conv_c_w"][l], small["conv_c_b"][l] = _cf_conv_bwd(du1, s["proj"], conv_c_w, l, offs, DC,
                                                                          f"cf_conv_bwd_{l}")
        dq, dk, dv = _attn_bwd(dattn, s["attn_tot"], s["proj"], offs, DA, f"attn_bwd_{l}")
        dproj = jnp.concatenate([dq, dk, dv, dsb, dsc, dsu, dca, dcg, dla, dlb, dlc], axis=1)
        g["w_in"] = dw("w_in", s["h"], dproj, l, tn_cap=512)
        dh = _mm(dproj, big["w_in"], tb=True, name=f"d_h_{l}", tk_cap=3072)
        dx, small["mix_pre"][l] = _pre_bwd(dx1, dh, s["xin"], g_mix_pre, l, f"mix_pre_bwd_{l}")
        g_mlp_post = g_mlp_post + emit_grads(l, "mix", g)[0, 0]
    return loss, dx, small


BIG = ("w_in", "proj_a", "proj_b", "proj_c", "w_o", "w_up", "w_down")
BIG_KIND = {"w_in": "col", "proj_a": "col", "proj_b": "col", "proj_c": "col", "w_o": "row", "w_up": "col",
            "w_down": "row"}


def kernel(x, ln_mix_pre, ln_mix_post, ln_mlp_pre, ln_mlp_post, w_in, conv_a_w, proj_a, proj_b, conv_c_w, conv_c_b, norm_c_g, norm_c_b, proj_c, w_o, w_up, w_down, loss_target, m_ln_mix_pre, m_ln_mix_post, m_ln_mlp_pre, m_ln_mlp_post, m_w_in, m_conv_a_w, m_proj_a, m_proj_b, m_conv_c_w, m_conv_c_b, m_norm_c_g, m_norm_c_b, m_proj_c, m_w_o, m_w_up, m_w_down, v_ln_mix_pre, v_ln_mix_post, v_ln_mlp_pre, v_ln_mlp_post, v_w_in, v_conv_a_w, v_proj_a, v_proj_b, v_conv_c_w, v_conv_c_b, v_norm_c_g, v_norm_c_b, v_proj_c, v_w_o, v_w_up, v_w_down):
    weights = dict(ln_mix_pre=ln_mix_pre, ln_mix_post=ln_mix_post, ln_mlp_pre=ln_mlp_pre, ln_mlp_post=ln_mlp_post,
                   w_in=w_in, conv_a_w=conv_a_w, proj_a=proj_a, proj_b=proj_b, conv_c_w=conv_c_w, conv_c_b=conv_c_b,
                   norm_c_g=norm_c_g, norm_c_b=norm_c_b, proj_c=proj_c, w_o=w_o, w_up=w_up, w_down=w_down)
    m_in = dict(ln_mix_pre=m_ln_mix_pre, ln_mix_post=m_ln_mix_post, ln_mlp_pre=m_ln_mlp_pre, ln_mlp_post=m_ln_mlp_post,
                w_in=m_w_in, conv_a_w=m_conv_a_w, proj_a=m_proj_a, proj_b=m_proj_b, conv_c_w=m_conv_c_w,
                conv_c_b=m_conv_c_b, norm_c_g=m_norm_c_g, norm_c_b=m_norm_c_b, proj_c=m_proj_c, w_o=m_w_o,
                w_up=m_w_up, w_down=m_w_down)
    v_in = dict(ln_mix_pre=v_ln_mix_pre, ln_mix_post=v_ln_mix_post, ln_mlp_pre=v_ln_mlp_pre, ln_mlp_post=v_ln_mlp_post,
                w_in=v_w_in, conv_a_w=v_conv_a_w, proj_a=v_proj_a, proj_b=v_proj_b, conv_c_w=v_conv_c_w,
                conv_c_b=v_conv_c_b, norm_c_g=v_norm_c_g, norm_c_b=v_norm_c_b, proj_c=v_proj_c, w_o=v_w_o,
                w_up=v_w_up, w_down=v_w_down)
    order = list(weights)
    L, D = ln_mix_pre.shape
    chip = 2 * lax.axis_index("x") + lax.axis_index("y")
    place = jnp.stack([lax.axis_index("c"), chip]).astype(jnp.int32)

    conv_local = [conv_a_w, conv_c_w]
    slots = _all_gather_small(_pack(conv_local))
    gather_groups = {"in": ("w_in",), "rest": ("proj_a", "proj_b", "proj_c", "w_o", "w_up", "w_down")}
    started = _gather_start(
        [[(_place_shard(weights[k], l, BIG_KIND[k], place, f"place_{k}_{l}"), BIG_KIND[k]) for k in names]
         for l in range(L) for names in gather_groups.values()], slots)
    in_flight = {(l, part): started[l * len(gather_groups) + i]
                 for l in range(L) for i, part in enumerate(gather_groups)}

    def layer_weights(l, part, after):
        names = gather_groups[part]
        kinds = [BIG_KIND[k] for k in names]
        send_sem, recv_sem, bufs = in_flight[l, part]
        landed = _gather_wait(bufs, kinds, send_sem, recv_sem, after, f"gather_wait_{part}_{l}")
        return zip(names, _gather_forward(landed, kinds, f"gather_forward_{part}_{l}"))

    pending = []

    def emit_grads(l, part, g):
        names = [k for k in BIG if k in g]
        kinds = [BIG_KIND[k] for k in names]
        glist = [g[k] for k in names]
        got = _reduce_to_sibling_halves(glist, kinds, f"reduce_d2d_{part}_{l}")
        pair = [_pair_sum(gk, r, kd, place, f"pair_sum_{k}_{l}") for k, kd, gk, r in zip(names, kinds, glist, got)]
        send_sem, recv_sem, parts, zones, token = _reduce_start(pair, f"reduce_start_{part}_{l}")
        pending.append((l, part, names, parts, zones, send_sem, recv_sem))
        return token

    per_chip = [_unpack(slots[4 * px + 2 * py], conv_local) for px in range(2) for py in range(2)]
    conv_a_full = jnp.concatenate([pc[0] for pc in per_chip], axis=-1)
    conv_c_full = jnp.concatenate([pc[1] for pc in per_chip], axis=-1)

    gains = [weights[k].reshape(L, 1, D) for k in ("ln_mix_pre", "ln_mix_post", "ln_mlp_pre", "ln_mlp_post")]
    loss, dx, small = _local_step(x[0], loss_target[0], gains, conv_a_full, conv_c_full, conv_c_b, norm_c_g,
                                  norm_c_b, layer_weights, emit_grads)

    reduced = {k: None for k in BIG}
    for l, part, names, parts, zones, send_sem, recv_sem in pending:
        parts, landed = _reduce_wait(parts, zones, send_sem, recv_sem, dx, f"reduce_wait_{part}_{l}")
        for k, p, z in zip(names, parts, landed):
            reduced[k] = _chip_sum(p, z, place, (l, L, reduced[k]), f"chip_sum_{k}_{l}")
    shared = _share_with_sibling([reduced[k] for k in BIG])
    grads = {k: r.reshape(r.shape[0], r.shape[1] * r.shape[2], r.shape[3]) for k, r in zip(BIG, shared)}

    small_names = ["ln_mix_pre", "ln_mix_post", "ln_mlp_pre", "ln_mlp_post", "conv_a_w", "conv_c_w", "conv_c_b",
                   "norm_c_g", "norm_c_b"]
    small_key = dict(ln_mix_pre="mix_pre", ln_mix_post="mix_post", ln_mlp_pre="mlp_pre", ln_mlp_post="mlp_post")
    small_local = []
    for k in small_names:
        per_layer = small[small_key.get(k, k)]
        stacked = jnp.stack(per_layer)
        small_local.append(stacked.reshape(L, -1) if stacked.shape[1] == 1 else stacked)
    small_sum = _unpack(_sum_slots(_all_gather_small(_pack(small_local))), small_local)
    for k, g in zip(small_names, small_sum):
        if k in ("conv_a_w", "conv_c_w"):
            width = weights[k].shape[-1]
            g = lax.dynamic_slice_in_dim(g, chip * width, width, axis=2)
        grads[k] = g

    delta, new_m, new_v = {}, {}, {}
    for k in BIG:
        delta[k], new_m[k], new_v[k] = _adamw(weights[k], grads[k], m_in[k], v_in[k], f"adamw_{k}")
    packed = [_pack([t[k] for k in small_names]) for t in (weights, grads, m_in, v_in)]
    like = [weights[k] for k in small_names]
    for dst, res in zip((delta, new_m, new_v), _adamw(*packed, "adamw_small")):
        dst.update(zip(small_names, _unpack(res, like)))

    total = lax.psum(loss[0, 0], ("x", "y", "c"))
    return (total, dx[None], *[grads[k] for k in order], *[delta[k] for k in order],
            *[new_m[k] for k in order], *[new_v[k] for k in order])
```

```python
import functools

import jax
import jax.numpy as jnp
from jax import lax
from jax.experimental import pallas as pl
from jax.experimental.pallas import tpu as pltpu

F32 = jnp.float32
BF16 = jnp.bfloat16
MESH = pl.DeviceIdType.MESH

HEAD_DIM = 128
QB = 128
ATTN_FWD_HEADS = 4
ATTN_BWD_HEADS = 4
RMS_EPS = 1e-6
LN_EPS = 1e-5
ADAM_LR = 0.001
ADAM_B1 = 0.9
ADAM_B2 = 0.999
ADAM_EPS = 1e-08
ADAM_WD = 0.01
ADAM_STEP = 10
LANE = 128
VMEM_LIMIT = 56 * 1024 * 1024
CONV_PAD = 32
CONV_ROWS = 256
ANY = pl.BlockSpec(memory_space=pl.ANY)


def _tile(n, cap, mult=LANE):
    best = None
    t = mult
    while t <= min(n, cap):
        if n % t == 0:
            best = t
        t += mult
    return best if best is not None else n


def _params(*sem):
    return pltpu.CompilerParams(dimension_semantics=sem if sem else None, vmem_limit_bytes=VMEM_LIMIT)


def _sigmoid(x):
    return 1.0 / (1.0 + jnp.exp(-x))


def _mm(a, b, *, name, ta=False, tb=False, out_dtypes=(F32,), epilogue=None, extras=(),
        tm_cap=1024, tn_cap=1024, tk_cap=2048):
    if ta:
        K, M = a.shape
    else:
        M, K = a.shape
    N = b.shape[0] if tb else b.shape[1]
    tm, tn, tk = _tile(M, tm_cap), _tile(N, tn_cap), _tile(K, tk_cap)
    nk = K // tk
    a_spec = pl.BlockSpec((tk, tm), lambda i, j, k: (k, i)) if ta else pl.BlockSpec((tm, tk), lambda i, j, k: (i, k))
    b_spec = pl.BlockSpec((tn, tk), lambda i, j, k: (j, k)) if tb else pl.BlockSpec((tk, tn), lambda i, j, k: (k, j))
    e_specs = [pl.BlockSpec((tm, tn), lambda i, j, k: (i, j)) for _ in extras]
    dims = (((0 if ta else 1,), (1 if tb else 0,)), ((), ()))
    n_e, n_o = len(extras), len(out_dtypes)

    def body(a_ref, b_ref, *rest):
        e_refs, o_refs = rest[:n_e], rest[n_e:n_e + n_o]
        part = lax.dot_general(a_ref[...].astype(BF16), b_ref[...].astype(BF16), dims, preferred_element_type=F32)

        def finish(acc):
            outs = (acc,) if epilogue is None else epilogue(acc, *[e[...] for e in e_refs])
            for o_ref, o in zip(o_refs, outs):
                o_ref[...] = o.astype(o_ref.dtype)

        if nk == 1:
            finish(part)
            return
        acc_ref = rest[n_e + n_o]
        k = pl.program_id(2)

        @pl.when(k == 0)
        def _():
            acc_ref[...] = part

        @pl.when(k > 0)
        def _():
            acc_ref[...] += part

        @pl.when(k == nk - 1)
        def _():
            finish(acc_ref[...])

    outs = pl.pallas_call(
        body, name=name, grid=(M // tm, N // tn, nk),
        in_specs=[a_spec, b_spec] + e_specs,
        out_specs=[pl.BlockSpec((tm, tn), lambda i, j, k: (i, j)) for _ in out_dtypes],
        out_shape=[jax.ShapeDtypeStruct((M, N), dt) for dt in out_dtypes],
        scratch_shapes=[pltpu.VMEM((tm, tn), F32)] if nk > 1 else [],
        compiler_params=_params("parallel", "parallel", "arbitrary"),
    )(a, b, *extras)
    return outs[0] if n_o == 1 else outs


def _row_tile(S):
    return _tile(S, 256, 8)


def _gain_spec(D, l):
    return pl.BlockSpec((None, 1, D), lambda i: (l, 0, 0))


def _rms(x, g):
    r = lax.rsqrt(jnp.mean(x * x, axis=-1, keepdims=True) + RMS_EPS)
    return x * r * g


def _rms_fwd(x, g3, l, name):
    S, D = x.shape
    tr = _row_tile(S)

    def body(x_ref, g_ref, h_ref):
        h_ref[...] = _rms(x_ref[...], g_ref[...]).astype(BF16)

    return pl.pallas_call(
        body, name=name, grid=(S // tr,),
        in_specs=[pl.BlockSpec((tr, D), lambda i: (i, 0)), _gain_spec(D, l)],
        out_specs=pl.BlockSpec((tr, D), lambda i: (i, 0)),
        out_shape=jax.ShapeDtypeStruct((S, D), BF16),
        compiler_params=_params("parallel"),
    )(x, g3)


def _post_res_fwd(x_in, f, gpost3, l, gnext3, lnext, name):
    S, D = x_in.shape
    tr = _row_tile(S)

    def body(x_ref, f_ref, gp_ref, gn_ref, xo_ref, h_ref):
        xo = x_ref[...] + _rms(f_ref[...], gp_ref[...])
        xo_ref[...] = xo
        h_ref[...] = _rms(xo, gn_ref[...]).astype(BF16)

    row = pl.BlockSpec((tr, D), lambda i: (i, 0))
    return pl.pallas_call(
        body, name=name, grid=(S // tr,),
        in_specs=[row, row, _gain_spec(D, l), _gain_spec(D, lnext)],
        out_specs=[row, row],
        out_shape=[jax.ShapeDtypeStruct((S, D), F32), jax.ShapeDtypeStruct((S, D), BF16)],
        compiler_params=_params("parallel"),
    )(x_in, f, gpost3, gnext3)


def _final_fwd_loss(x_in, f, gpost3, l, target, name):
    S, D = x_in.shape
    tr = _row_tile(S)

    def body(x_ref, f_ref, gp_ref, t_ref, dx_ref, loss_ref):
        @pl.when(pl.program_id(0) == 0)
        def _():
            loss_ref[...] = jnp.zeros_like(loss_ref)

        err = x_ref[...] + _rms(f_ref[...], gp_ref[...]) - t_ref[...]
        dx_ref[...] = err * (1.0 / D)
        loss_ref[...] += 0.5 * jnp.sum(jnp.mean(err * err, axis=-1, keepdims=True))

    row = pl.BlockSpec((tr, D), lambda i: (i, 0))
    return pl.pallas_call(
        body, name=name, grid=(S // tr,),
        in_specs=[row, row, _gain_spec(D, l), row],
        out_specs=[row, pl.BlockSpec((8, LANE), lambda i: (0, 0))],
        out_shape=[jax.ShapeDtypeStruct((S, D), F32), jax.ShapeDtypeStruct((8, LANE), F32)],
        compiler_params=_params("arbitrary"),
    )(x_in, f, gpost3, target)


def _rms_bwd_rows(dy, x, g):
    r = lax.rsqrt(jnp.mean(x * x, axis=-1, keepdims=True) + RMS_EPS)
    t = dy * g
    dx = r * t - x * (r * r * r) * jnp.mean(t * x, axis=-1, keepdims=True)
    return dx, dy * x * r


def _post_bwd(dxo, f, gpost3, l, name):
    S, D = f.shape
    tr = _row_tile(S)

    def body(d_ref, f_ref, g_ref, df_ref, dg_ref):
        @pl.when(pl.program_id(0) == 0)
        def _():
            dg_ref[...] = jnp.zeros_like(dg_ref)

        df, dg = _rms_bwd_rows(d_ref[...], f_ref[...], g_ref[...])
        df_ref[...] = df.astype(BF16)
        dg_ref[...] += jnp.sum(dg, axis=0, keepdims=True)

    row = pl.BlockSpec((tr, D), lambda i: (i, 0))
    return pl.pallas_call(
        body, name=name, grid=(S // tr,),
        in_specs=[row, row, _gain_spec(D, l)],
        out_specs=[row, pl.BlockSpec((1, D), lambda i: (0, 0))],
        out_shape=[jax.ShapeDtypeStruct((S, D), BF16), jax.ShapeDtypeStruct((1, D), F32)],
        compiler_params=_params("arbitrary"),
    )(dxo, f, gpost3)


def _pre_bwd(dxo, dh, x_in, gpre3, l, name):
    S, D = x_in.shape
    tr = _row_tile(S)

    def body(d_ref, dh_ref, x_ref, g_ref, dx_ref, dg_ref):
        @pl.when(pl.program_id(0) == 0)
        def _():
            dg_ref[...] = jnp.zeros_like(dg_ref)

        dx, dg = _rms_bwd_rows(dh_ref[...], x_ref[...], g_ref[...])
        dx_ref[...] = d_ref[...] + dx
        dg_ref[...] += jnp.sum(dg, axis=0, keepdims=True)

    row = pl.BlockSpec((tr, D), lambda i: (i, 0))
    return pl.pallas_call(
        body, name=name, grid=(S // tr,),
        in_specs=[row, row, row, _gain_spec(D, l)],
        out_specs=[row, pl.BlockSpec((1, D), lambda i: (0, 0))],
        out_shape=[jax.ShapeDtypeStruct((S, D), F32), jax.ShapeDtypeStruct((1, D), F32)],
        compiler_params=_params("arbitrary"),
    )(dxo, dh, x_in, gpre3)


def _zero_pads(pad_ref, S):
    z = jnp.zeros((CONV_PAD, pad_ref.shape[1]), F32)
    pad_ref[pl.ds(0, CONV_PAD), :] = z
    pad_ref[pl.ds(CONV_PAD + S, CONV_PAD), :] = z


def _conv_fwd_chunk(pad_ref, w_ref, K, r0, rows):
    acc = None
    for k in range(K):
        term = w_ref[pl.ds(k, 1), :] * pad_ref[pl.ds(CONV_PAD + r0 - (K - 1) + k, rows), :]
        acc = term if acc is None else acc + term
    return acc


def _conv_bwd_chunk(pad_ref, w_ref, K, r0, rows):
    acc = None
    for k in range(K):
        term = w_ref[pl.ds(k, 1), :] * pad_ref[pl.ds(CONV_PAD + r0 + (K - 1) - k, rows), :]
        acc = term if acc is None else acc + term
    return acc


def _conv_dw(upad_ref, dy_ref_or_pad, dy_off, K, S, dw_ref):
    rows = min(CONV_ROWS, S)
    for k in range(K):
        acc = None
        for r0 in range(0, S, rows):
            term = jnp.sum(dy_ref_or_pad[pl.ds(dy_off + r0, rows), :]
                           * upad_ref[pl.ds(CONV_PAD + r0 - (K - 1) + k, rows), :], axis=0, keepdims=True)
            acc = term if acc is None else acc + term
        dw_ref[pl.ds(k, 1), :] = acc


def _col_spec(S, off):
    return pl.BlockSpec((S, LANE), lambda j: (0, off // LANE + j))


def _sc_fwd(proj, conv_w, l, offs, DS, name):
    S = proj.shape[0]
    K = conv_w.shape[1]
    rows = min(CONV_ROWS, S)

    def body(b_ref, c_ref, u_ref, w_ref, o_ref, pad_ref):
        _zero_pads(pad_ref, S)
        pad_ref[pl.ds(CONV_PAD, S), :] = c_ref[...].astype(F32) * u_ref[...].astype(F32)
        for r0 in range(0, S, rows):
            cv = _conv_fwd_chunk(pad_ref, w_ref, K, r0, rows)
            o_ref[pl.ds(r0, rows), :] = (b_ref[pl.ds(r0, rows), :].astype(F32) * cv).astype(BF16)

    return pl.pallas_call(
        body, name=name, grid=(DS // LANE,),
        in_specs=[_col_spec(S, offs["sc_b"]), _col_spec(S, offs["sc_c"]), _col_spec(S, offs["sc_u"]),
                  pl.BlockSpec((None, K, LANE), lambda j: (l, 0, j))],
        out_specs=pl.BlockSpec((S, LANE), lambda j: (0, j)),
        out_shape=jax.ShapeDtypeStruct((S, DS), BF16),
        scratch_shapes=[pltpu.VMEM((S + 2 * CONV_PAD, LANE), F32)],
        compiler_params=_params("parallel"),
    )(proj, proj, proj, conv_w)


def _sc_bwd(dga, proj, conv_w, l, offs, DS, name):
    S = proj.shape[0]
    K = conv_w.shape[1]
    rows = min(CONV_ROWS, S)

    def body(d_ref, b_ref, c_ref, u_ref, w_ref, db_ref, dc_ref, du_ref, dw_ref, tpad_ref, gpad_ref):
        _zero_pads(tpad_ref, S)
        _zero_pads(gpad_ref, S)
        tpad_ref[pl.ds(CONV_PAD, S), :] = c_ref[...].astype(F32) * u_ref[...].astype(F32)
        for r0 in range(0, S, rows):
            sl = pl.ds(r0, rows)
            cv = _conv_fwd_chunk(tpad_ref, w_ref, K, r0, rows)
            d = d_ref[sl, :]
            db_ref[sl, :] = (d * cv).astype(BF16)
            gpad_ref[pl.ds(CONV_PAD + r0, rows), :] = d * b_ref[sl, :].astype(F32)
        for r0 in range(0, S, rows):
            sl = pl.ds(r0, rows)
            dt = _conv_bwd_chunk(gpad_ref, w_ref, K, r0, rows)
            dc_ref[sl, :] = (dt * u_ref[sl, :].astype(F32)).astype(BF16)
            du_ref[sl, :] = (dt * c_ref[sl, :].astype(F32)).astype(BF16)
        _conv_dw(tpad_ref, gpad_ref, CONV_PAD, K, S, dw_ref)

    blk = pl.BlockSpec((S, LANE), lambda j: (0, j))
    act = jax.ShapeDtypeStruct((S, DS), BF16)
    return pl.pallas_call(
        body, name=name, grid=(DS // LANE,),
        in_specs=[blk, _col_spec(S, offs["sc_b"]), _col_spec(S, offs["sc_c"]), _col_spec(S, offs["sc_u"]),
                  pl.BlockSpec((None, K, LANE), lambda j: (l, 0, j))],
        out_specs=[blk, blk, blk, pl.BlockSpec((K, LANE), lambda j: (0, j))],
        out_shape=[act, act, act, jax.ShapeDtypeStruct((K, DS), F32)],
        scratch_shapes=[pltpu.VMEM((S + 2 * CONV_PAD, LANE), F32), pltpu.VMEM((S + 2 * CONV_PAD, LANE), F32)],
        compiler_params=_params("parallel"),
    )(dga, proj, proj, proj, conv_w)


def _cf_conv_fwd(proj, conv_w, conv_b3, l, offs, DC, name):
    S = proj.shape[0]
    K = conv_w.shape[1]
    rows = min(CONV_ROWS, S)

    def body(a_ref, g_ref, w_ref, bias_ref, o_ref, pad_ref):
        _zero_pads(pad_ref, S)
        pad_ref[pl.ds(CONV_PAD, S), :] = a_ref[...].astype(F32) * _sigmoid(g_ref[...].astype(F32))
        for r0 in range(0, S, rows):
            o_ref[pl.ds(r0, rows), :] = _conv_fwd_chunk(pad_ref, w_ref, K, r0, rows) + bias_ref[...]

    return pl.pallas_call(
        body, name=name, grid=(DC // LANE,),
        in_specs=[_col_spec(S, offs["cf_a"]), _col_spec(S, offs["cf_g"]),
                  pl.BlockSpec((None, K, LANE), lambda j: (l, 0, j)),
                  pl.BlockSpec((None, 1, LANE), lambda j: (l, 0, j))],
        out_specs=pl.BlockSpec((S, LANE), lambda j: (0, j)),
        out_shape=jax.ShapeDtypeStruct((S, DC), F32),
        scratch_shapes=[pltpu.VMEM((S + 2 * CONV_PAD, LANE), F32)],
        compiler_params=_params("parallel"),
    )(proj, proj, conv_w, conv_b3)


def _layer_norm_hat(u):
    mu = jnp.mean(u, axis=-1, keepdims=True)
    xc = u - mu
    rstd = lax.rsqrt(jnp.mean(xc * xc, axis=-1, keepdims=True) + LN_EPS)
    return xc * rstd, rstd


def _cf_norm_fwd(u1, gam3, bet3, l, name):
    S, DC = u1.shape
    tr = _row_tile(S)

    def body(u_ref, g_ref, b_ref, o_ref):
        xhat, _ = _layer_norm_hat(u_ref[...])
        s = xhat * g_ref[...] + b_ref[...]
        o_ref[...] = (s * _sigmoid(s)).astype(BF16)

    row = pl.BlockSpec((tr, DC), lambda i: (i, 0))
    vec = pl.BlockSpec((None, 1, DC), lambda i: (l, 0, 0))
    return pl.pallas_call(
        body, name=name, grid=(S // tr,),
        in_specs=[row, vec, vec], out_specs=row,
        out_shape=jax.ShapeDtypeStruct((S, DC), BF16),
        compiler_params=_params("parallel"),
    )(u1, gam3, bet3)


def _cf_norm_bwd(du2, u1, gam3, bet3, l, name):
    S, DC = u1.shape
    tr = _row_tile(S)

    def body(d_ref, u_ref, g_ref, b_ref, du_ref, dg_ref, db_ref):
        @pl.when(pl.program_id(0) == 0)
        def _():
            dg_ref[...] = jnp.zeros_like(dg_ref)
            db_ref[...] = jnp.zeros_like(db_ref)

        xhat, rstd = _layer_norm_hat(u_ref[...])
        s = xhat * g_ref[...] + b_ref[...]
        sg = _sigmoid(s)
        ds = d_ref[...] * (sg * (1.0 + s * (1.0 - sg)))
        dg_ref[...] += jnp.sum(ds * xhat, axis=0, keepdims=True)
        db_ref[...] += jnp.sum(ds, axis=0, keepdims=True)
        dxh = ds * g_ref[...]
        du_ref[...] = rstd * (dxh - jnp.mean(dxh, axis=-1, keepdims=True)
                              - xhat * jnp.mean(dxh * xhat, axis=-1, keepdims=True))

    row = pl.BlockSpec((tr, DC), lambda i: (i, 0))
    vec = pl.BlockSpec((None, 1, DC), lambda i: (l, 0, 0))
    acc = pl.BlockSpec((1, DC), lambda i: (0, 0))
    return pl.pallas_call(
        body, name=name, grid=(S // tr,),
        in_specs=[row, row, vec, vec], out_specs=[row, acc, acc],
        out_shape=[jax.ShapeDtypeStruct((S, DC), F32), jax.ShapeDtypeStruct((1, DC), F32),
                   jax.ShapeDtypeStruct((1, DC), F32)],
        compiler_params=_params("arbitrary"),
    )(du2, u1, gam3, bet3)


def _cf_conv_bwd(du1, proj, conv_w, l, offs, DC, name):
    S = proj.shape[0]
    K = conv_w.shape[1]
    rows = min(CONV_ROWS, S)

    def body(d_ref, a_ref, g_ref, w_ref, da_ref, dgl_ref, dw_ref, dbias_ref, upad_ref, dpad_ref):
        _zero_pads(upad_ref, S)
        _zero_pads(dpad_ref, S)
        upad_ref[pl.ds(CONV_PAD, S), :] = a_ref[...].astype(F32) * _sigmoid(g_ref[...].astype(F32))
        dpad_ref[pl.ds(CONV_PAD, S), :] = d_ref[...]
        dbias_ref[...] = jnp.sum(d_ref[...], axis=0, keepdims=True)
        for r0 in range(0, S, rows):
            sl = pl.ds(r0, rows)
            du0 = _conv_bwd_chunk(dpad_ref, w_ref, K, r0, rows)
            a = a_ref[sl, :].astype(F32)
            sg = _sigmoid(g_ref[sl, :].astype(F32))
            da_ref[sl, :] = (du0 * sg).astype(BF16)
            dgl_ref[sl, :] = (du0 * a * sg * (1.0 - sg)).astype(BF16)
        _conv_dw(upad_ref, dpad_ref, CONV_PAD, K, S, dw_ref)

    blk = pl.BlockSpec((S, LANE), lambda j: (0, j))
    act = jax.ShapeDtypeStruct((S, DC), BF16)
    return pl.pallas_call(
        body, name=name, grid=(DC // LANE,),
        in_specs=[blk, _col_spec(S, offs["cf_a"]), _col_spec(S, offs["cf_g"]),
                  pl.BlockSpec((None, K, LANE), lambda j: (l, 0, j))],
        out_specs=[blk, blk, pl.BlockSpec((K, LANE), lambda j: (0, j)), pl.BlockSpec((1, LANE), lambda j: (0, j))],
        out_shape=[act, act, jax.ShapeDtypeStruct((K, DC), F32), jax.ShapeDtypeStruct((1, DC), F32)],
        scratch_shapes=[pltpu.VMEM((S + 2 * CONV_PAD, LANE), F32), pltpu.VMEM((S + 2 * CONV_PAD, LANE), F32)],
        compiler_params=_params("parallel"),
    )(du1, proj, proj, conv_w)


def _dot_nt(a, b):
    return lax.dot_general(a, b, (((1,), (1,)), ((), ())), preferred_element_type=F32)


def _dot_nn(a, b):
    return lax.dot_general(a, b, (((1,), (0,)), ((), ())), preferred_element_type=F32)


def _dot_tn(a, b):
    return lax.dot_general(a, b, (((0,), (0,)), ((), ())), preferred_element_type=F32)


def _dot_split(x, u):
    hi = x.astype(BF16)
    lo = (x - hi.astype(F32)).astype(BF16)
    return _dot_nn(hi, u) + _dot_nn(lo, u)


MASKED = -1e30


def _log_fail(z):
    return -(jnp.maximum(z, 0.0) + jnp.log(1.0 + jnp.exp(-jnp.abs(z))))


def _head_group(H, want):
    g = min(want, H)
    while H % g:
        g -= 1
    return g


def _lanes(g):
    return slice(g * HEAD_DIM, (g + 1) * HEAD_DIM)


def _attn_fwd(proj, offs, DA, name):
    S = proj.shape[0]
    H = DA // HEAD_DIM
    G = _head_group(H, ATTN_FWD_HEADS)
    nb = S // QB

    def body(q_ref, k_ref, v_ref, o_ref, tot_ref):
        row = lax.broadcasted_iota(jnp.int32, (QB, QB), 0)
        col = lax.broadcasted_iota(jnp.int32, (QB, QB), 1)
        u_after = (row > col).astype(BF16)

        def rows_of(i, t):
            return pl.ds(pl.multiple_of(jnp.maximum(i - t, 0) * QB, QB), QB)

        def scores(i, t, g, q):
            return _dot_nt(q, k_ref[rows_of(i, t), _lanes(g)]) * (HEAD_DIM ** -0.5)

        def log_terms(i, t, z):
            valid = col < row + jnp.minimum(t, i) * QB
            lf = jnp.where(valid, _log_fail(z), 0.0)
            return jnp.where(valid, lf + z + _dot_split(lf, u_after), MASKED), jnp.sum(lf, axis=1, keepdims=True)

        def q_block(i, _):
            qs = pl.ds(pl.multiple_of(i * QB, QB), QB)
            qg = [q_ref[qs, _lanes(g)] for g in range(G)]

            def step(t, carry):
                out = []
                for g in range(G):
                    acc, c, z, (pre, rs) = carry[g]
                    a = jnp.exp(pre + c)
                    acc = acc + _dot_nn(a.astype(BF16), v_ref[rows_of(i, t - 2), _lanes(g)])
                    out.append((acc, c + rs, scores(i, t, g, qg[g]), log_terms(i, t - 1, z)))
                return tuple(out)

            init = []
            for g in range(G):
                z0 = scores(i, 0, g, qg[g])
                init.append((jnp.zeros((QB, HEAD_DIM), F32), jnp.zeros((QB, 1), F32), scores(i, 1, g, qg[g]),
                             log_terms(i, 0, z0)))
            res = lax.fori_loop(2, i + 3, step, tuple(init))
            for g in range(G):
                o_ref[qs, _lanes(g)] = res[g][0].astype(BF16)
                tot_ref[g, qs, :] = res[g][1]
            return 0

        lax.fori_loop(0, nb, q_block, 0)

    def hs(off):
        return pl.BlockSpec((S, G * HEAD_DIM), lambda h: (0, off // (G * HEAD_DIM) + h))

    return pl.pallas_call(
        body, name=name, grid=(H // G,),
        in_specs=[hs(offs["q"]), hs(offs["k"]), hs(offs["v"])],
        out_specs=[pl.BlockSpec((S, G * HEAD_DIM), lambda h: (0, h)), pl.BlockSpec((G, S, 1), lambda h: (h, 0, 0))],
        out_shape=[jax.ShapeDtypeStruct((S, DA), BF16), jax.ShapeDtypeStruct((H, S, 1), F32)],
        compiler_params=_params("parallel"),
    )(proj, proj, proj)


def _attn_bwd(dout, tot, proj, offs, DA, name):
    S = proj.shape[0]
    H = DA // HEAD_DIM
    G = _head_group(H, ATTN_BWD_HEADS)
    nb = S // QB
    scale = HEAD_DIM ** -0.5

    def body(q_ref, k_ref, v_ref, tot_ref, do_ref, dq_ref, dk_ref, dv_ref, dk_acc, dv_acc, stage_ref):
        row = lax.broadcasted_iota(jnp.int32, (QB, QB), 0)
        col = lax.broadcasted_iota(jnp.int32, (QB, QB), 1)
        u_after = (row > col).astype(BF16)
        u_before = (row < col).astype(BF16)
        dk_acc[...] = jnp.zeros_like(dk_acc)
        dv_acc[...] = jnp.zeros_like(dv_acc)

        def rows_of(i, b):
            return pl.ds(pl.multiple_of(jnp.minimum(b, i) * QB, QB), QB)

        def scores(i, b, g, q, do):
            ks = rows_of(i, b)
            return _dot_nt(q, k_ref[ks, _lanes(g)]) * scale, _dot_nt(do, v_ref[ks, _lanes(g)])

        def log_terms(i, b, z, da):
            valid = col < row + (i - jnp.minimum(b, i)) * QB
            lf = jnp.where(valid, _log_fail(z), 0.0)
            pre = jnp.where(valid, lf + z + _dot_split(lf, u_after), MASKED)
            return (pre, da, jnp.exp(lf), jnp.where(valid, jnp.exp(lf + z), 0.0),
                    jnp.sum(lf, axis=1, keepdims=True))

        def d_log_a(tot_q, seen, terms):
            pre, da, fail, beta, rs = terms
            seen = seen + rs
            a = jnp.exp(pre + (tot_q - seen))
            dlog = a * da
            return seen, (dlog, _dot_split(dlog, u_before), a.astype(BF16), fail, beta)

        def q_block(i, _):
            qs = pl.ds(pl.multiple_of(i * QB, QB), QB)
            qg = [q_ref[qs, _lanes(g)] for g in range(G)]
            dog = [do_ref[qs, _lanes(g)] for g in range(G)]
            totg = [tot_ref[g, qs, :] for g in range(G)]

            def put(g, first, tiles):
                for n, tile in enumerate(tiles):
                    stage_ref[g, first + n] = tile.astype(F32)

            def get(g, first, count):
                return [stage_ref[g, first + n] for n in range(count)]

            def step(t, carry):
                out = []
                for g in range(G):
                    dq, seen, gsum, rs = carry[g]
                    dlog, left, a, fail, beta = get(g, 6, 5)
                    terms = get(g, 2, 4) + [rs]
                    z, da = get(g, 0, 2)
                    ks = rows_of(i, t - 3)
                    dz = (dlog * fail - (gsum + left) * beta) * scale
                    dzb = dz.astype(BF16)
                    dk_acc[ks, _lanes(g)] += _dot_tn(dzb, qg[g])
                    dv_acc[ks, _lanes(g)] += _dot_tn(a.astype(BF16), dog[g])
                    dq = dq + _dot_nn(dzb, k_ref[ks, _lanes(g)])
                    gsum = gsum + jnp.sum(dlog, axis=1, keepdims=True)
                    seen, grads = d_log_a(totg[g], seen, terms)
                    terms = log_terms(i, t - 1, z, da)
                    put(g, 6, grads)
                    put(g, 2, terms[:4])
                    put(g, 0, scores(i, t, g, qg[g], dog[g]))
                    out.append((dq, seen, gsum, terms[4]))
                return tuple(out)

            zero = jnp.zeros((QB, 1), F32)
            init = []
            for g in range(G):
                terms0 = log_terms(i, 0, *scores(i, 0, g, qg[g], dog[g]))
                terms1 = log_terms(i, 1, *scores(i, 1, g, qg[g], dog[g]))
                seen, grads0 = d_log_a(totg[g], zero, terms0)
                put(g, 6, grads0)
                put(g, 2, terms1[:4])
                put(g, 0, scores(i, 2, g, qg[g], dog[g]))
                init.append((jnp.zeros((QB, HEAD_DIM), F32), seen, zero, terms1[4]))
            res = lax.fori_loop(3, i + 4, step, tuple(init))
            for g in range(G):
                dq_ref[qs, _lanes(g)] = res[g][0].astype(BF16)
            return 0

        lax.fori_loop(0, nb, q_block, 0)
        dk_ref[...] = dk_acc[...].astype(BF16)
        dv_ref[...] = dv_acc[...].astype(BF16)

    def hs(off):
        return pl.BlockSpec((S, G * HEAD_DIM), lambda h: (0, off // (G * HEAD_DIM) + h))

    head = pl.BlockSpec((S, G * HEAD_DIM), lambda h: (0, h))
    act = jax.ShapeDtypeStruct((S, DA), BF16)
    return pl.pallas_call(
        body, name=name, grid=(H // G,),
        in_specs=[hs(offs["q"]), hs(offs["k"]), hs(offs["v"]), pl.BlockSpec((G, S, 1), lambda h: (h, 0, 0)), head],
        out_specs=[head, head, head], out_shape=[act, act, act],
        scratch_shapes=[pltpu.VMEM((S, G * HEAD_DIM), F32), pltpu.VMEM((S, G * HEAD_DIM), F32),
                        pltpu.VMEM((G, 11, QB, QB), F32)],
        compiler_params=_params("parallel"),
    )(proj, proj, proj, tot, dout)


def _merge_tiles(S, D, goff):
    tn = LANE
    for t in range(LANE, 513, LANE):
        if D % t == 0 and goff % t == 0:
            tn = t
    return _tile(S, 512, 8), tn


def _merge_fwd(ga, attn, u2, pa, pb, pc, proj, goff, name):
    S = ga.shape[0]
    D = pa.shape[-1]
    tm, tn = _merge_tiles(S, D, goff)

    def body(ga_ref, at_ref, u2_ref, pa_ref, pb_ref, pc_ref, la_ref, lb_ref, lc_ref, ya_ref, yb_ref, yc_ref, m_ref):
        ya = _dot_nn(ga_ref[...], pa_ref[...])
        yb = _dot_nn(at_ref[...], pb_ref[...])
        yc = _dot_nn(u2_ref[...], pc_ref[...])
        ya_ref[...] = ya.astype(BF16)
        yb_ref[...] = yb.astype(BF16)
        yc_ref[...] = yc.astype(BF16)
        m_ref[...] = (_sigmoid(la_ref[...].astype(F32)) * ya + _sigmoid(lb_ref[...].astype(F32)) * yb
                      + _sigmoid(lc_ref[...].astype(F32)) * yc).astype(BF16)

    def lhs(a):
        return pl.BlockSpec((tm, a.shape[1]), lambda i, j: (i, 0))

    def rhs(p):
        return pl.BlockSpec((p.shape[0], tn), lambda i, j: (0, j))

    def gate(r):
        return pl.BlockSpec((tm, tn), lambda i, j: (i, (goff + r * D) // tn + j))

    out = pl.BlockSpec((tm, tn), lambda i, j: (i, j))
    act = jax.ShapeDtypeStruct((S, D), BF16)
    return pl.pallas_call(
        body, name=name, grid=(S // tm, D // tn),
        in_specs=[lhs(ga), lhs(attn), lhs(u2), rhs(pa), rhs(pb), rhs(pc), gate(0), gate(1), gate(2)],
        out_specs=[out, out, out, out], out_shape=[act, act, act, act],
        compiler_params=_params("parallel", "parallel"),
    )(ga, attn, u2, pa, pb, pc, proj, proj, proj)


def _merge_bwd(dm, ya, yb, yc, proj, goff, name):
    S, D = dm.shape
    tm, tn = _merge_tiles(S, D, goff)

    def body(dm_ref, ya_ref, yb_ref, yc_ref, la_ref, lb_ref, lc_ref, *o_refs):
        dmv = dm_ref[...]
        for y_ref, l_ref, dy_ref, dl_ref in zip((ya_ref, yb_ref, yc_ref), (la_ref, lb_ref, lc_ref),
                                                o_refs[:3], o_refs[3:]):
            sg = _sigmoid(l_ref[...].astype(F32))
            dy_ref[...] = (dmv * sg).astype(BF16)
            dl_ref[...] = (dmv * y_ref[...].astype(F32) * sg * (1.0 - sg)).astype(BF16)

    def gate(r):
        return pl.BlockSpec((tm, tn), lambda i, j: (i, (goff + r * D) // tn + j))

    blk = pl.BlockSpec((tm, tn), lambda i, j: (i, j))
    act = jax.ShapeDtypeStruct((S, D), BF16)
    return pl.pallas_call(
        body, name=name, grid=(S // tm, D // tn),
        in_specs=[blk, blk, blk, blk, gate(0), gate(1), gate(2)],
        out_specs=[blk] * 6, out_shape=[act] * 6,
        compiler_params=_params("parallel", "parallel"),
    )(dm, ya, yb, yc, proj, proj, proj)


def _ew_tiles(rows, cols):
    tc = cols if cols <= 4096 else _tile(cols, 2048)
    tr = _tile(rows, max(8, (1 << 19) // tc), 8)
    return tr, tc


def _half_rows_tile(Rh, Cs):
    return _tile(Rh, max(16, (1 << 19) // Cs), 16)


def _place_shard(w, l, kind, place, name):
    _, Rs, Cs = w.shape
    tr = _half_rows_tile(Rs, Cs)

    def body(pr_ref, w_ref, o_ref):
        o_ref[...] = w_ref[...].astype(BF16)

    if kind == "col":
        shape = (Rs, 4 * Cs)
        o_spec = pl.BlockSpec((tr, Cs), lambda i, pr: (i, pr[1]))
    else:
        shape = (4, Rs, Cs)
        o_spec = pl.BlockSpec((None, tr, Cs), lambda i, pr: (pr[1], i, 0))
    out = pl.pallas_call(
        body, name=name,
        grid_spec=pltpu.PrefetchScalarGridSpec(
            num_scalar_prefetch=1, grid=(Rs // tr,),
            in_specs=[pl.BlockSpec((None, tr, Cs), lambda i, pr: (l, i, 0))], out_specs=o_spec),
        out_shape=jax.ShapeDtypeStruct(shape, BF16), compiler_params=_params("parallel"),
    )(place, w)
    return out if kind == "col" else out.reshape(4 * Rs, Cs)


def _pair_sum(g, got, kind, place, name):
    _, Rh, Cs = got.shape
    tr = _half_rows_tile(Rh, Cs)
    if kind == "col":
        gv = g.reshape(2, Rh, 4 * Cs)
        g_spec = pl.BlockSpec((None, tr, Cs), lambda p, i, pr: (pr[0], i, p))
    else:
        gv = g.reshape(4, 2, Rh, Cs)
        g_spec = pl.BlockSpec((None, None, tr, Cs), lambda p, i, pr: (p, pr[0], i, 0))
    blk = pl.BlockSpec((None, tr, Cs), lambda p, i, pr: (p, i, 0))

    def body(pr_ref, g_ref, r_ref, o_ref):
        o_ref[...] = (g_ref[...].astype(F32) + r_ref[...].astype(F32)).astype(BF16)

    return pl.pallas_call(
        body, name=name,
        grid_spec=pltpu.PrefetchScalarGridSpec(num_scalar_prefetch=1, grid=(4, Rh // tr),
                                               in_specs=[g_spec, blk], out_specs=blk),
        out_shape=jax.ShapeDtypeStruct(got.shape, BF16), compiler_params=_params("parallel", "parallel"),
    )(place, gv, got)


def _chip_sum(part, got, place, stack, name):
    n, Rh, Cs = got.shape
    l, L, buf = stack
    tr = _half_rows_tile(Rh, Cs)
    prev = () if buf is None else (buf,)

    def body(pr_ref, p_ref, r_ref, *rest):
        acc = p_ref[...].astype(F32)
        for s in range(n):
            acc = acc + r_ref[s].astype(F32)
        rest[-1][...] = acc

    return pl.pallas_call(
        body, name=name,
        grid_spec=pltpu.PrefetchScalarGridSpec(
            num_scalar_prefetch=1, grid=(Rh // tr,),
            in_specs=[pl.BlockSpec((None, tr, Cs), lambda i, pr: (pr[1], i, 0)),
                      pl.BlockSpec((n, tr, Cs), lambda i, pr: (0, i, 0))] + [ANY] * len(prev),
            out_specs=pl.BlockSpec((None, None, tr, Cs), lambda i, pr: (l, pr[0], i, 0))),
        out_shape=jax.ShapeDtypeStruct((L, 2, Rh, Cs), F32), input_output_aliases={3: 0} if prev else {},
        compiler_params=_params("parallel"),
    )(place, part, got, *prev)


def _adamw(w, g, m, v, name):
    shape = w.shape
    args = [t.reshape(-1, shape[-1]) for t in (w, g, m, v)]
    rows, cols = args[0].shape
    tr, tc = _ew_tiles(rows, cols)
    c1 = 1.0 - ADAM_B1 ** ADAM_STEP
    c2 = 1.0 - ADAM_B2 ** ADAM_STEP

    def body(w_ref, g_ref, m_ref, v_ref, d_ref, nm_ref, nv_ref):
        gv = g_ref[...]
        nm = ADAM_B1 * m_ref[...] + (1.0 - ADAM_B1) * gv
        nv = ADAM_B2 * v_ref[...] + (1.0 - ADAM_B2) * (gv * gv)
        nm_ref[...] = nm
        nv_ref[...] = nv
        d_ref[...] = -ADAM_LR * ((nm / c1) / (jnp.sqrt(nv / c2) + ADAM_EPS) + ADAM_WD * w_ref[...])

    blk = pl.BlockSpec((tr, tc), lambda i, j: (i, j))
    shp = jax.ShapeDtypeStruct((rows, cols), F32)
    outs = pl.pallas_call(
        body, name=name, grid=(rows // tr, cols // tc), in_specs=[blk] * 4, out_specs=[blk] * 3,
        out_shape=[shp] * 3, compiler_params=_params("parallel", "parallel"),
    )(*args)
    return [o.reshape(shape) for o in outs]


def _place():
    x, y, c = lax.axis_index("x"), lax.axis_index("y"), lax.axis_index("c")
    chips = [(1 - x, y), (x, 1 - y), (1 - x, 1 - y)]
    return x, y, c, chips


def _al(v, unit):
    return pl.multiple_of(v, unit) if unit % LANE == 0 else v


def _half_of_full(ref, kind, p, half, Rs, Cs):
    Rh = (ref.shape[-2] // 2) if kind == "col" else Rs // 2
    lead = (slice(None),) * (len(ref.shape) - 2)
    if kind == "col":
        return ref.at[lead + (pl.ds(_al(half * Rh, Rh), Rh), pl.ds(_al(p * Cs, Cs), Cs))]
    return ref.at[lead + (pl.ds(_al(p * Rs + half * Rh, Rh), Rh), slice(None))]


HBM = pl.BlockSpec(memory_space=pltpu.HBM)
SEM = pl.BlockSpec(memory_space=pltpu.SEMAPHORE)
EFFECT = pltpu.SideEffectType.DATAFLOW_SIDE_EFFECTING
GATHER_ID = 0
REDUCE_ID = 1


def _in_hbm(v):
    return pltpu.with_memory_space_constraint(v, pltpu.HBM)


def _ici_handshake(chips, c):
    barrier = pltpu.get_barrier_semaphore()
    for px, py in chips:
        pl.semaphore_signal(barrier, inc=1, device_id=(px, py, c), device_id_type=MESH)
    pl.semaphore_wait(barrier, len(chips))


def _shard_dims(ref, kind):
    R, C = ref.shape[-2:]
    return (R, C // 4) if kind == "col" else (R // 4, C)


def _gather_start(groups, after):
    bufs = [b for grp in groups for b, _ in grp]
    kinds = [k for grp in groups for _, k in grp]
    m, ng = len(bufs), len(groups)

    def body(*refs):
        ins = refs[:m]
        sems = refs[m + 1:m + 1 + 2 * ng]
        x, y, c, chips = _place()
        _ici_handshake(chips, c)
        me = 2 * x + y
        at = 0
        for gi, grp in enumerate(groups):
            n = len(grp)
            for j, chip in enumerate(chips):
                for w in range(n):
                    ref, kind = ins[at + w], kinds[at + w]
                    mine = _half_of_full(ref, kind, me, c, *_shard_dims(ref, kind))
                    pltpu.make_async_remote_copy(mine, mine, sems[2 * gi].at[j * n + w], sems[2 * gi + 1].at[j * n + w],
                                                 device_id=(*chip, c), device_id_type=MESH).start()
            at += n

    sem_shapes = [pltpu.SemaphoreType.DMA((3 * len(grp),)) for grp in groups for _ in range(2)]
    outs = pl.pallas_call(
        body, name="gather_start", in_specs=[HBM] * m + [ANY], out_specs=[SEM] * (2 * ng) + [HBM] * m,
        out_shape=sem_shapes + [pltpu.HBM(b.shape, b.dtype) for b in bufs],
        input_output_aliases={i: 2 * ng + i for i in range(m)},
        compiler_params=pltpu.CompilerParams(has_side_effects=EFFECT, collective_id=GATHER_ID),
    )(*[_in_hbm(b) for b in bufs], after)
    res, at = [], 2 * ng
    for gi, grp in enumerate(groups):
        res.append((outs[2 * gi], outs[2 * gi + 1], outs[at:at + len(grp)]))
        at += len(grp)
    return res


def _gather_wait(bufs, kinds, send_sem, recv_sem, after, name):
    n = len(bufs)

    def body(*refs):
        ins = refs[:n]
        send, recv = refs[n], refs[n + 1]
        x, y, c, chips = _place()
        me = 2 * x + y
        for j, (px, py) in enumerate(chips):
            for w in range(n):
                dims = _shard_dims(ins[w], kinds[w])
                mine = _half_of_full(ins[w], kinds[w], me, c, *dims)
                theirs = _half_of_full(ins[w], kinds[w], 2 * px + py, c, *dims)
                cp = pltpu.make_async_remote_copy(mine, theirs, send.at[j * n + w], recv.at[j * n + w],
                                                  device_id=(px, py, c), device_id_type=MESH)
                cp.wait_send()
                cp.wait_recv()

    return pl.pallas_call(
        body, name=name, in_specs=[HBM] * n + [SEM, SEM, ANY], out_specs=[HBM] * n,
        out_shape=[pltpu.HBM(b.shape, b.dtype) for b in bufs],
        input_output_aliases={i: i for i in range(n)},
        compiler_params=pltpu.CompilerParams(has_side_effects=EFFECT),
    )(*bufs, send_sem, recv_sem, after)


def _gather_forward(bufs, kinds, name):
    n = len(bufs)

    def body(*refs):
        outs = refs[n:2 * n]
        send_sem, recv_sem = refs[2 * n:]
        x, y, c, chips = _place()
        sib = (x, y, 1 - c)
        copies = []
        for j, (px, py) in enumerate(chips):
            for w in range(n):
                got = _half_of_full(outs[w], kinds[w], 2 * px + py, c, *_shard_dims(outs[w], kinds[w]))
                cp = pltpu.make_async_remote_copy(got, got, send_sem.at[j * n + w], recv_sem.at[j * n + w],
                                                  device_id=sib, device_id_type=MESH)
                cp.start()
                copies.append(cp)
        for cp in copies:
            cp.wait()

    return pl.pallas_call(
        body, name=name, in_specs=[ANY] * n, out_specs=[ANY] * n,
        out_shape=[jax.ShapeDtypeStruct(b.shape, b.dtype) for b in bufs],
        input_output_aliases={w: w for w in range(n)},
        scratch_shapes=[pltpu.SemaphoreType.DMA((3 * n,))] * 2,
    )(*bufs)


def _reduce_to_sibling_halves(grads, kinds, name):
    n = len(grads)
    shapes = []
    for g, kind in zip(grads, kinds):
        R, C = g.shape
        shapes.append((4, R // 2, C // 4) if kind == "col" else (4, R // 8, C))

    def body(*refs):
        ins, gots = refs[:n], refs[n:2 * n]
        send_sem, recv_sem = refs[2 * n:]
        x, y, c, _ = _place()
        sib = (x, y, 1 - c)
        copies = []
        for w in range(n):
            _, Rh, Cs = shapes[w]
            for p in range(4):
                cp = pltpu.make_async_remote_copy(_half_of_full(ins[w], kinds[w], p, 1 - c, 2 * Rh, Cs),
                                                  gots[w].at[p], send_sem.at[4 * w + p], recv_sem.at[4 * w + p],
                                                  device_id=sib, device_id_type=MESH)
                cp.start()
                copies.append(cp)
        for cp in copies:
            cp.wait()

    return pl.pallas_call(
        body, name=name, in_specs=[ANY] * n, out_specs=[ANY] * n,
        out_shape=[jax.ShapeDtypeStruct(s, BF16) for s in shapes],
        scratch_shapes=[pltpu.SemaphoreType.DMA((4 * n,))] * 2,
    )(*grads)


def _reduce_start(parts, name):
    n = len(parts)
    zones = [lax.empty((3,) + p.shape[1:], p.dtype) for p in parts]

    def body(*refs):
        ins, lands = refs[:n], refs[n:2 * n]
        send, recv = refs[2 * n], refs[2 * n + 1]
        token = refs[-1]
        x, y, c, chips = _place()
        _ici_handshake(chips, c)
        for j, (px, py) in enumerate(chips):
            for w in range(n):
                pltpu.make_async_remote_copy(ins[w].at[2 * px + py], lands[w].at[j], send.at[j * n + w],
                                             recv.at[j * n + w], device_id=(px, py, c), device_id_type=MESH).start()
        token[...] = jnp.zeros_like(token)

    both = list(parts) + zones
    outs = pl.pallas_call(
        body, name=name, in_specs=[HBM] * (2 * n),
        out_specs=[SEM, SEM] + [HBM] * (2 * n) + [pl.BlockSpec(memory_space=pltpu.VMEM)],
        out_shape=([pltpu.SemaphoreType.DMA((3 * n,))] * 2 + [pltpu.HBM(b.shape, b.dtype) for b in both]
                   + [jax.ShapeDtypeStruct((8, LANE), F32)]),
        input_output_aliases={i: 2 + i for i in range(2 * n)},
        compiler_params=pltpu.CompilerParams(has_side_effects=EFFECT, collective_id=REDUCE_ID),
    )(*[_in_hbm(b) for b in both])
    return outs[0], outs[1], outs[2:2 + n], outs[2 + n:2 + 2 * n], outs[-1]


def _reduce_wait(parts, zones, send_sem, recv_sem, after, name):
    n = len(parts)

    def body(*refs):
        ins, lands = refs[:n], refs[n:2 * n]
        send, recv = refs[2 * n], refs[2 * n + 1]
        x, y, c, chips = _place()
        for j, (px, py) in enumerate(chips):
            for w in range(n):
                cp = pltpu.make_async_remote_copy(ins[w].at[2 * px + py], lands[w].at[j], send.at[j * n + w],
                                                  recv.at[j * n + w], device_id=(px, py, c), device_id_type=MESH)
                cp.wait_send()
                cp.wait_recv()

    both = list(parts) + list(zones)
    outs = pl.pallas_call(
        body, name=name, in_specs=[HBM] * (2 * n) + [SEM, SEM, ANY], out_specs=[HBM] * (2 * n),
        out_shape=[pltpu.HBM(b.shape, b.dtype) for b in both],
        input_output_aliases={i: i for i in range(2 * n)},
        compiler_params=pltpu.CompilerParams(has_side_effects=EFFECT),
    )(*both, send_sem, recv_sem, after)
    return outs[:n], outs[n:]


def _share_with_sibling(reduced, name):
    n = len(reduced)

    def body(*refs):
        outs = refs[n:2 * n]
        send_sem, recv_sem = refs[2 * n:]
        x, y, c, _ = _place()
        sib = (x, y, 1 - c)
        copies = []
        for w in range(n):
            mine = outs[w].at[:, c]
            cp = pltpu.make_async_remote_copy(mine, mine, send_sem.at[w], recv_sem.at[w], device_id=sib,
                                              device_id_type=MESH)
            cp.start()
            copies.append(cp)
        for cp in copies:
            cp.wait()

    return pl.pallas_call(
        body, name=name, in_specs=[ANY] * n, out_specs=[ANY] * n,
        out_shape=[jax.ShapeDtypeStruct(r.shape, r.dtype) for r in reduced],
        input_output_aliases={w: w for w in range(n)},
        scratch_shapes=[pltpu.SemaphoreType.DMA((n,))] * 2,
    )(*reduced)


def _all_gather_small(v):
    r = v.shape[0]

    def body(v_ref, o_ref, send_sem, recv_sem):
        x, y, c, _ = _place()
        me = 4 * x + 2 * y + c
        o_ref[me] = v_ref[...]
        copies = []
        for k in range(1, 8):
            peer = (x ^ (k >> 2), y ^ ((k >> 1) & 1), c ^ (k & 1))
            cp = pltpu.make_async_remote_copy(v_ref, o_ref.at[me], send_sem.at[k - 1], recv_sem.at[k - 1],
                                              device_id=peer, device_id_type=MESH)
            cp.start()
            copies.append(cp)
        for cp in copies:
            cp.wait()

    vmem = pl.BlockSpec(memory_space=pltpu.VMEM)
    return pl.pallas_call(
        body, name="all_gather_small", in_specs=[vmem], out_specs=vmem,
        out_shape=jax.ShapeDtypeStruct((8, r, LANE), F32),
        scratch_shapes=[pltpu.SemaphoreType.DMA((7,)), pltpu.SemaphoreType.DMA((7,))],
        compiler_params=pltpu.CompilerParams(vmem_limit_bytes=VMEM_LIMIT),
    )(v)


def _sum_slots(g):
    n, r, _ = g.shape

    def body(g_ref, o_ref):
        acc = g_ref[0]
        for s in range(1, n):
            acc = acc + g_ref[s]
        o_ref[...] = acc

    vmem = pl.BlockSpec(memory_space=pltpu.VMEM)
    return pl.pallas_call(
        body, name="sum_slots", in_specs=[vmem], out_specs=vmem, out_shape=jax.ShapeDtypeStruct((r, LANE), F32),
        compiler_params=pltpu.CompilerParams(vmem_limit_bytes=VMEM_LIMIT),
    )(g)


def _pack(arrays):
    flat = jnp.concatenate([a.reshape(-1) for a in arrays])
    pad = (-flat.shape[0]) % (8 * LANE)
    return jnp.pad(flat, (0, pad)).reshape(-1, LANE)


def _unpack(packed, like):
    flat = packed.reshape(-1)
    out, off = [], 0
    for a in like:
        out.append(flat[off:off + a.size].reshape(a.shape))
        off += a.size
    return out


def _offsets(D):
    DA, DS, DC = D // 2, D // 4, D // 4
    names = ["q", "k", "v", "sc_b", "sc_c", "sc_u", "cf_a", "cf_g", "gate"]
    sizes = [DA, DA, DA, DS, DS, DS, DC, DC, 3 * D]
    offs, o = {}, 0
    for nm, sz in zip(names, sizes):
        offs[nm] = o
        o += sz
    return offs, DA, DS, DC


def _relu2_epilogue(acc):
    r = jnp.maximum(acc, 0.0)
    return acc, r * r


def _drelu2_epilogue(acc, up):
    return (acc * (2.0 * jnp.maximum(up.astype(F32), 0.0)),)


def _local_step(x, target, gains, conv_a_w, conv_c_w, conv_c_b, norm_c_g, norm_c_b, layer_weights, emit_grads):
    S, D = x.shape
    L = gains[0].shape[0]
    offs, DA, DS, DC = _offsets(D)
    goff = offs["gate"]
    g_mix_pre, g_mix_post, g_mlp_pre, g_mlp_post = gains
    cb3, ng3, nb3 = (t.reshape(L, 1, DC) for t in (conv_c_b, norm_c_g, norm_c_b))

    saved = []
    h = _rms_fwd(x, g_mix_pre, 0, "rms_first")
    xin = x
    for l in range(L):
        big = dict(layer_weights(l, "in", xin))
        proj = _mm(h, big["w_in"], out_dtypes=(BF16,), name=f"fwd_w_in_{l}", tn_cap=512)
        ga = _sc_fwd(proj, conv_a_w, l, offs, DS, f"sc_fwd_{l}")
        attn, attn_tot = _attn_fwd(proj, offs, DA, f"attn_fwd_{l}")
        u1 = _cf_conv_fwd(proj, conv_c_w, cb3, l, offs, DC, f"cf_conv_fwd_{l}")
        u2 = _cf_norm_fwd(u1, ng3, nb3, l, f"cf_norm_fwd_{l}")
        big.update(layer_weights(l, "rest", attn))
        ya, yb, yc, merged = _merge_fwd(ga, attn, u2, big["proj_a"], big["proj_b"], big["proj_c"], proj, goff,
                                        f"merge_fwd_{l}")
        mixed = _mm(merged, big["w_o"], name=f"fwd_w_o_{l}")
        x1, h2 = _post_res_fwd(xin, mixed, g_mix_post, l, g_mlp_pre, l, f"mix_residual_{l}")
        up, act = _mm(h2, big["w_up"], out_dtypes=(BF16, BF16), epilogue=_relu2_epilogue, name=f"fwd_w_up_{l}")
        f = _mm(act, big["w_down"], name=f"fwd_w_down_{l}")
        saved.append(dict(big=big, xin=xin, h=h, proj=proj, ga=ga, attn=attn, attn_tot=attn_tot, u1=u1, u2=u2, ya=ya,
                          yb=yb, yc=yc, merged=merged, mixed=mixed, x1=x1, h2=h2, up=up, act=act, f=f))
        if l + 1 < L:
            xin, h = _post_res_fwd(x1, f, g_mlp_post, l, g_mix_pre, l + 1, f"mlp_residual_{l}")
        else:
            dx, loss = _final_fwd_loss(x1, f, g_mlp_post, l, target, "loss_head")

    small = {k: [None] * L for k in ("mix_pre", "mix_post", "mlp_pre", "mlp_post", "conv_a_w", "conv_c_w",
                                       "conv_c_b", "norm_c_g", "norm_c_b")}

    def dw(key, a, b, l, **kw):
        return _mm(a, b, ta=True, out_dtypes=(BF16,), name=f"d{key}_{l}", **kw)

    for l in reversed(range(L)):
        s = saved[l]
        big = s["big"]
        g = {}
        df, small["mlp_post"][l] = _post_bwd(dx, s["f"], g_mlp_post, l, f"mlp_post_bwd_{l}")
        g["w_down"] = dw("w_down", s["act"], df, l)
        dup = _mm(df, big["w_down"], tb=True, out_dtypes=(BF16,), epilogue=_drelu2_epilogue, extras=(s["up"],),
                  name=f"d_up_{l}")
        g["w_up"] = dw("w_up", s["h2"], dup, l)
        dh2 = _mm(dup, big["w_up"], tb=True, name=f"d_h2_{l}")
        dx1, small["mlp_pre"][l] = _pre_bwd(dx, dh2, s["x1"], g_mlp_pre, l, f"mlp_pre_bwd_{l}")
        g_mix_post = g_mix_post + emit_grads(l, "mlp", g)[0, 0]
        g = {}
        dmixed, small["mix_post"][l] = _post_bwd(dx1, s["mixed"], g_mix_post, l, f"mix_post_bwd_{l}")
        g["w_o"] = dw("w_o", s["merged"], dmixed, l)
        dmerged = _mm(dmixed, big["w_o"], tb=True, name=f"d_merged_{l}")
        dya, dyb, dyc, dla, dlb, dlc = _merge_bwd(dmerged, s["ya"], s["yb"], s["yc"], s["proj"], goff,
                                                  f"merge_bwd_{l}")
        g["proj_a"] = dw("proj_a", s["ga"], dya, l)
        g["proj_b"] = dw("proj_b", s["attn"], dyb, l)
        g["proj_c"] = dw("proj_c", s["u2"], dyc, l)
        dga = _mm(dya, big["proj_a"], tb=True, name=f"d_ga_{l}")
        dattn = _mm(dyb, big["proj_b"], tb=True, out_dtypes=(BF16,), name=f"d_attn_{l}")
        du2 = _mm(dyc, big["proj_c"], tb=True, name=f"d_u2_{l}")
        dsb, dsc, dsu, small["conv_a_w"][l] = _sc_bwd(dga, s["proj"], conv_a_w, l, offs, DS, f"sc_bwd_{l}")
        du1, small["norm_c_g"][l], small["norm_c_b"][l] = _cf_norm_bwd(du2, s["u1"], ng3, nb3, l, f"cf_norm_bwd_{l}")
        dca, dcg, small["conv_c_w"][l], small["conv_c_b"][l] = _cf_conv_bwd(du1, s["proj"], conv_c_w, l, offs, DC,
                                                                          f"cf_conv_bwd_{l}")
        dq, dk, dv = _attn_bwd(dattn, s["attn_tot"], s["proj"], offs, DA, f"attn_bwd_{l}")
        dproj = jnp.concatenate([dq, dk, dv, dsb, dsc, dsu, dca, dcg, dla, dlb, dlc], axis=1)
        g["w_in"] = dw("w_in", s["h"], dproj, l, tn_cap=512)
        dh = _mm(dproj, big["w_in"], tb=True, name=f"d_h_{l}", tk_cap=3072)
        dx, small["mix_pre"][l] = _pre_bwd(dx1, dh, s["xin"], g_mix_pre, l, f"mix_pre_bwd_{l}")
        g_mlp_post = g_mlp_post + emit_grads(l, "mix", g)[0, 0]
    return loss, dx, small


BIG = ("w_in", "proj_a", "proj_b", "proj_c", "w_o", "w_up", "w_down")
BIG_KIND = {"w_in": "col", "proj_a": "col", "proj_b": "col", "proj_c": "col", "w_o": "row", "w_up": "col",
            "w_down": "row"}


def kernel(x, ln_mix_pre, ln_mix_post, ln_mlp_pre, ln_mlp_post, w_in, conv_a_w, proj_a, proj_b, conv_c_w, conv_c_b, norm_c_g, norm_c_b, proj_c, w_o, w_up, w_down, loss_target, m_ln_mix_pre, m_ln_mix_post, m_ln_mlp_pre, m_ln_mlp_post, m_w_in, m_conv_a_w, m_proj_a, m_proj_b, m_conv_c_w, m_conv_c_b, m_norm_c_g, m_norm_c_b, m_proj_c, m_w_o, m_w_up, m_w_down, v_ln_mix_pre, v_ln_mix_post, v_ln_mlp_pre, v_ln_mlp_post, v_w_in, v_conv_a_w, v_proj_a, v_proj_b, v_conv_c_w, v_conv_c_b, v_norm_c_g, v_norm_c_b, v_proj_c, v_w_o, v_w_up, v_w_down):
    weights = dict(ln_mix_pre=ln_mix_pre, ln_mix_post=ln_mix_post, ln_mlp_pre=ln_mlp_pre, ln_mlp_post=ln_mlp_post,
                   w_in=w_in, conv_a_w=conv_a_w, proj_a=proj_a, proj_b=proj_b, conv_c_w=conv_c_w, conv_c_b=conv_c_b,
                   norm_c_g=norm_c_g, norm_c_b=norm_c_b, proj_c=proj_c, w_o=w_o, w_up=w_up, w_down=w_down)
    m_in = dict(ln_mix_pre=m_ln_mix_pre, ln_mix_post=m_ln_mix_post, ln_mlp_pre=m_ln_mlp_pre, ln_mlp_post=m_ln_mlp_post,
                w_in=m_w_in, conv_a_w=m_conv_a_w, proj_a=m_proj_a, proj_b=m_proj_b, conv_c_w=m_conv_c_w,
                conv_c_b=m_conv_c_b, norm_c_g=m_norm_c_g, norm_c_b=m_norm_c_b, proj_c=m_proj_c, w_o=m_w_o,
                w_up=m_w_up, w_down=m_w_down)
    v_in = dict(ln_mix_pre=v_ln_mix_pre, ln_mix_post=v_ln_mix_post, ln_mlp_pre=v_ln_mlp_pre, ln_mlp_post=v_ln_mlp_post,
                w_in=v_w_in, conv_a_w=v_conv_a_w, proj_a=v_proj_a, proj_b=v_proj_b, conv_c_w=v_conv_c_w,
                conv_c_b=v_conv_c_b, norm_c_g=v_norm_c_g, norm_c_b=v_norm_c_b, proj_c=v_proj_c, w_o=v_w_o,
                w_up=v_w_up, w_down=v_w_down)
    order = list(weights)
    L, D = ln_mix_pre.shape
    chip = 2 * lax.axis_index("x") + lax.axis_index("y")
    place = jnp.stack([lax.axis_index("c"), chip]).astype(jnp.int32)

    conv_local = [conv_a_w, conv_c_w]
    slots = _all_gather_small(_pack(conv_local))
    gather_groups = {"in": ("w_in",), "rest": ("proj_a", "proj_b", "proj_c", "w_o", "w_up", "w_down")}
    started = _gather_start(
        [[(_place_shard(weights[k], l, BIG_KIND[k], place, f"place_{k}_{l}"), BIG_KIND[k]) for k in names]
         for l in range(L) for names in gather_groups.values()], slots)
    in_flight = {(l, part): started[l * len(gather_groups) + i]
                 for l in range(L) for i, part in enumerate(gather_groups)}

    def layer_weights(l, part, after):
        names = gather_groups[part]
        kinds = [BIG_KIND[k] for k in names]
        send_sem, recv_sem, bufs = in_flight[l, part]
        landed = _gather_wait(bufs, kinds, send_sem, recv_sem, after, f"gather_wait_{part}_{l}")
        return zip(names, _gather_forward(landed, kinds, f"gather_forward_{part}_{l}"))

    pending = []

    def emit_grads(l, part, g):
        names = [k for k in BIG if k in g]
        kinds = [BIG_KIND[k] for k in names]
        glist = [g[k] for k in names]
        got = _reduce_to_sibling_halves(glist, kinds, f"reduce_d2d_{part}_{l}")
        pair = [_pair_sum(gk, r, kd, place, f"pair_sum_{k}_{l}") for k, kd, gk, r in zip(names, kinds, glist, got)]
        send_sem, recv_sem, parts, zones, token = _reduce_start(pair, f"reduce_start_{part}_{l}")
        pending.append((l, part, names, parts, zones, send_sem, recv_sem))
        return token

    per_chip = [_unpack(slots[4 * px + 2 * py], conv_local) for px in range(2) for py in range(2)]
    conv_a_full = jnp.concatenate([pc[0] for pc in per_chip], axis=-1)
    conv_c_full = jnp.concatenate([pc[1] for pc in per_chip], axis=-1)

    gains = [weights[k].reshape(L, 1, D) for k in ("ln_mix_pre", "ln_mix_post", "ln_mlp_pre", "ln_mlp_post")]
    loss, dx, small = _local_step(x[0], loss_target[0], gains, conv_a_full, conv_c_full, conv_c_b, norm_c_g,
                                  norm_c_b, layer_weights, emit_grads)

    grads, delta, new_m, new_v = {}, {}, {}, {}

    def finish(part, after):
        reduced = {}
        for l, p, names, parts, zones, send_sem, recv_sem in pending:
            if p == part:
                parts, landed = _reduce_wait(parts, zones, send_sem, recv_sem, after, f"reduce_wait_{p}_{l}")
                for k, mine, theirs in zip(names, parts, landed):
                    reduced[k] = _chip_sum(mine, theirs, place, (l, L, reduced.get(k)), f"chip_sum_{k}_{l}")
        for k, r in zip(reduced, _share_with_sibling(list(reduced.values()), f"reduce_share_{part}")):
            grads[k] = r.reshape(r.shape[0], r.shape[1] * r.shape[2], r.shape[3])
            delta[k], new_m[k], new_v[k] = _adamw(weights[k], grads[k], m_in[k], v_in[k], f"adamw_{k}")

    finish("mlp", dx)
    finish("mix", delta["w_down"])

    small_names = ["ln_mix_pre", "ln_mix_post", "ln_mlp_pre", "ln_mlp_post", "conv_a_w", "conv_c_w", "conv_c_b",
                   "norm_c_g", "norm_c_b"]
    small_key = dict(ln_mix_pre="mix_pre", ln_mix_post="mix_post", ln_mlp_pre="mlp_pre", ln_mlp_post="mlp_post")
    small_local = []
    for k in small_names:
        per_layer = small[small_key.get(k, k)]
        stacked = jnp.stack(per_layer)
        small_local.append(stacked.reshape(L, -1) if stacked.shape[1] == 1 else stacked)
    small_sum = _unpack(_sum_slots(_all_gather_small(_pack(small_local))), small_local)
    for k, g in zip(small_names, small_sum):
        if k in ("conv_a_w", "conv_c_w"):
            width = weights[k].shape[-1]
            g = lax.dynamic_slice_in_dim(g, chip * width, width, axis=2)
        grads[k] = g

    packed =[_pack([t[k] for k in small_names]) for t in (weights, grads, m_in, v_in)]
    like = [weights[k] for k in small_names]
    for dst, res in zip((delta, new_m, new_v), _adamw(*packed, "adamw_small")):
        dst.update(zip(small_names, _unpack(res, like)))

    total = lax.psum(loss[0, 0], ("x", "y", "c"))
    return (total, dx[None], *[grads[k] for k in order], *[delta[k] for k in order],
            *[new_m[k] for k in order], *[new_v[k] for k in order])
```

```python
import functools

import jax
import jax.numpy as jnp
from jax import lax
from jax.experimental import pallas as pl
from jax.experimental.pallas import tpu as pltpu

F32 = jnp.float32
BF16 = jnp.bfloat16
MESH = pl.DeviceIdType.MESH

HEAD_DIM = 128
QB = 128
ATTN_FWD_HEADS = 4
ATTN_BWD_HEADS = 4
RMS_EPS = 1e-6
LN_EPS = 1e-5
ADAM_LR = 0.001
ADAM_B1 = 0.9
ADAM_B2 = 0.999
ADAM_EPS = 1e-08
ADAM_WD = 0.01
ADAM_STEP = 10
LANE = 128
VMEM_LIMIT = 56 * 1024 * 1024
CONV_PAD = 32
CONV_ROWS = 256
ANY = pl.BlockSpec(memory_space=pl.ANY)


def _tile(n, cap, mult=LANE):
    best = None
    t = mult
    while t <= min(n, cap):
        if n % t == 0:
            best = t
        t += mult
    return best if best is not None else n


def _params(*sem):
    return pltpu.CompilerParams(dimension_semantics=sem if sem else None, vmem_limit_bytes=VMEM_LIMIT)


def _sigmoid(x):
    return 1.0 / (1.0 + jnp.exp(-x))


def _mm(a, b, *, name, ta=False, tb=False, out_dtypes=(F32,), epilogue=None, extras=(),
        tm_cap=1024, tn_cap=1024, tk_cap=2048, ride=None):
    if ta:
        K, M = a.shape
    else:
        M, K = a.shape
    N = b.shape[0] if tb else b.shape[1]
    tm, tn, tk = _tile(M, tm_cap), _tile(N, tn_cap), _tile(K, tk_cap)
    gm, gn, nk = M // tm, N // tn, K // tk
    a_spec = pl.BlockSpec((tk, tm), lambda i, j, k: (k, i)) if ta else pl.BlockSpec((tm, tk), lambda i, j, k: (i, k))
    b_spec = pl.BlockSpec((tn, tk), lambda i, j, k: (j, k)) if tb else pl.BlockSpec((tk, tn), lambda i, j, k: (k, j))
    e_specs = [pl.BlockSpec((tm, tn), lambda i, j, k: (i, j)) for _ in extras]
    dims = (((0 if ta else 1,), (1 if tb else 0,)), ((), ()))
    n_e, n_o = len(extras), len(out_dtypes)
    n_r = len(ride["arrays"]) if ride else 0

    def body(a_ref, b_ref, *rest):
        e_refs, o_refs = rest[:n_e], rest[n_e + n_r:n_e + n_r + n_o]
        step = (pl.program_id(0) * gn + pl.program_id(1)) * nk + pl.program_id(2)

        def riding():
            at = n_e + n_r + n_o
            send_sem, recv_sem = rest[-2], rest[-1]
            return ride["copies"](rest[n_e:n_e + n_r], rest[at:at + n_r], send_sem, recv_sem)

        if ride:
            @pl.when(step == 0)
            def _():
                for cp in riding():
                    cp.start()

        part = lax.dot_general(a_ref[...].astype(BF16), b_ref[...].astype(BF16), dims, preferred_element_type=F32)

        def finish(acc):
            outs = (acc,) if epilogue is None else epilogue(acc, *[e[...] for e in e_refs])
            for o_ref, o in zip(o_refs, outs):
                o_ref[...] = o.astype(o_ref.dtype)

        if nk == 1:
            finish(part)
        else:
            acc_ref = rest[n_e + 2 * n_r + n_o]
            k = pl.program_id(2)

            @pl.when(k == 0)
            def _():
                acc_ref[...] = part

            @pl.when(k > 0)
            def _():
                acc_ref[...] += part

            @pl.when(k == nk - 1)
            def _():
                finish(acc_ref[...])

        if ride:
            @pl.when(step == gm * gn * nk - 1)
            def _():
                for cp in riding():
                    cp.wait()

    scratch = [pltpu.VMEM((tm, tn), F32)] if nk > 1 else []
    if ride:
        scratch += [pltpu.SemaphoreType.DMA((ride["n_sems"],))] * 2
    outs = pl.pallas_call(
        body, name=name, grid=(gm, gn, nk),
        in_specs=[a_spec, b_spec] + e_specs + [ANY] * n_r,
        out_specs=[pl.BlockSpec((tm, tn), lambda i, j, k: (i, j)) for _ in out_dtypes] + [ANY] * n_r,
        out_shape=[jax.ShapeDtypeStruct((M, N), dt) for dt in out_dtypes] + (ride["out_shape"] if ride else []),
        input_output_aliases={2 + n_e + w: n_o + w for w in range(n_r)} if ride and ride["alias"] else {},
        scratch_shapes=scratch,
        compiler_params=_params(*(("arbitrary",) * 3 if ride else ("parallel", "parallel", "arbitrary"))),
    )(a, b, *extras, *(ride["arrays"] if ride else ()))
    main = outs[0] if n_o == 1 else outs[:n_o]
    return (main, outs[n_o:]) if ride else main


def _row_tile(S):
    return _tile(S, 256, 8)


def _gain_spec(D, l):
    return pl.BlockSpec((None, 1, D), lambda i: (l, 0, 0))


def _rms(x, g):
    r = lax.rsqrt(jnp.mean(x * x, axis=-1, keepdims=True) + RMS_EPS)
    return x * r * g


def _rms_fwd(x, g3, l, name):
    S, D = x.shape
    tr = _row_tile(S)

    def body(x_ref, g_ref, h_ref):
        h_ref[...] = _rms(x_ref[...], g_ref[...]).astype(BF16)

    return pl.pallas_call(
        body, name=name, grid=(S // tr,),
        in_specs=[pl.BlockSpec((tr, D), lambda i: (i, 0)), _gain_spec(D, l)],
        out_specs=pl.BlockSpec((tr, D), lambda i: (i, 0)),
        out_shape=jax.ShapeDtypeStruct((S, D), BF16),
        compiler_params=_params("parallel"),
    )(x, g3)


def _post_res_fwd(x_in, f, gpost3, l, gnext3, lnext, name):
    S, D = x_in.shape
    tr = _row_tile(S)

    def body(x_ref, f_ref, gp_ref, gn_ref, xo_ref, h_ref):
        xo = x_ref[...] + _rms(f_ref[...], gp_ref[...])
        xo_ref[...] = xo
        h_ref[...] = _rms(xo, gn_ref[...]).astype(BF16)

    row = pl.BlockSpec((tr, D), lambda i: (i, 0))
    return pl.pallas_call(
        body, name=name, grid=(S // tr,),
        in_specs=[row, row, _gain_spec(D, l), _gain_spec(D, lnext)],
        out_specs=[row, row],
        out_shape=[jax.ShapeDtypeStruct((S, D), F32), jax.ShapeDtypeStruct((S, D), BF16)],
        compiler_params=_params("parallel"),
    )(x_in, f, gpost3, gnext3)


def _final_fwd_loss(x_in, f, gpost3, l, target, name):
    S, D = x_in.shape
    tr = _row_tile(S)

    def body(x_ref, f_ref, gp_ref, t_ref, dx_ref, loss_ref):
        @pl.when(pl.program_id(0) == 0)
        def _():
            loss_ref[...] = jnp.zeros_like(loss_ref)

        err = x_ref[...] + _rms(f_ref[...], gp_ref[...]) - t_ref[...]
        dx_ref[...] = err * (1.0 / D)
        loss_ref[...] += 0.5 * jnp.sum(jnp.mean(err * err, axis=-1, keepdims=True))

    row = pl.BlockSpec((tr, D), lambda i: (i, 0))
    return pl.pallas_call(
        body, name=name, grid=(S // tr,),
        in_specs=[row, row, _gain_spec(D, l), row],
        out_specs=[row, pl.BlockSpec((8, LANE), lambda i: (0, 0))],
        out_shape=[jax.ShapeDtypeStruct((S, D), F32), jax.ShapeDtypeStruct((8, LANE), F32)],
        compiler_params=_params("arbitrary"),
    )(x_in, f, gpost3, target)


def _rms_bwd_rows(dy, x, g):
    r = lax.rsqrt(jnp.mean(x * x, axis=-1, keepdims=True) + RMS_EPS)
    t = dy * g
    dx = r * t - x * (r * r * r) * jnp.mean(t * x, axis=-1, keepdims=True)
    return dx, dy * x * r


def _post_bwd(dxo, f, gpost3, l, name):
    S, D = f.shape
    tr = _row_tile(S)

    def body(d_ref, f_ref, g_ref, df_ref, dg_ref):
        @pl.when(pl.program_id(0) == 0)
        def _():
            dg_ref[...] = jnp.zeros_like(dg_ref)

        df, dg = _rms_bwd_rows(d_ref[...], f_ref[...], g_ref[...])
        df_ref[...] = df.astype(BF16)
        dg_ref[...] += jnp.sum(dg, axis=0, keepdims=True)

    row = pl.BlockSpec((tr, D), lambda i: (i, 0))
    return pl.pallas_call(
        body, name=name, grid=(S // tr,),
        in_specs=[row, row, _gain_spec(D, l)],
        out_specs=[row, pl.BlockSpec((1, D), lambda i: (0, 0))],
        out_shape=[jax.ShapeDtypeStruct((S, D), BF16), jax.ShapeDtypeStruct((1, D), F32)],
        compiler_params=_params("arbitrary"),
    )(dxo, f, gpost3)


def _pre_bwd(dxo, dh, x_in, gpre3, l, name):
    S, D = x_in.shape
    tr = _row_tile(S)

    def body(d_ref, dh_ref, x_ref, g_ref, dx_ref, dg_ref):
        @pl.when(pl.program_id(0) == 0)
        def _():
            dg_ref[...] = jnp.zeros_like(dg_ref)

        dx, dg = _rms_bwd_rows(dh_ref[...], x_ref[...], g_ref[...])
        dx_ref[...] = d_ref[...] + dx
        dg_ref[...] += jnp.sum(dg, axis=0, keepdims=True)

    row = pl.BlockSpec((tr, D), lambda i: (i, 0))
    return pl.pallas_call(
        body, name=name, grid=(S // tr,),
        in_specs=[row, row, row, _gain_spec(D, l)],
        out_specs=[row, pl.BlockSpec((1, D), lambda i: (0, 0))],
        out_shape=[jax.ShapeDtypeStruct((S, D), F32), jax.ShapeDtypeStruct((1, D), F32)],
        compiler_params=_params("arbitrary"),
    )(dxo, dh, x_in, gpre3)


def _zero_pads(pad_ref, S):
    z = jnp.zeros((CONV_PAD, pad_ref.shape[1]), F32)
    pad_ref[pl.ds(0, CONV_PAD), :] = z
    pad_ref[pl.ds(CONV_PAD + S, CONV_PAD), :] = z


def _conv_fwd_chunk(pad_ref, w_ref, K, r0, rows):
    acc = None
    for k in range(K):
        term = w_ref[pl.ds(k, 1), :] * pad_ref[pl.ds(CONV_PAD + r0 - (K - 1) + k, rows), :]
        acc = term if acc is None else acc + term
    return acc


def _conv_bwd_chunk(pad_ref, w_ref, K, r0, rows):
    acc = None
    for k in range(K):
        term = w_ref[pl.ds(k, 1), :] * pad_ref[pl.ds(CONV_PAD + r0 + (K - 1) - k, rows), :]
        acc = term if acc is None else acc + term
    return acc


def _conv_dw(upad_ref, dy_ref_or_pad, dy_off, K, S, dw_ref):
    rows = min(CONV_ROWS, S)
    for k in range(K):
        acc = None
        for r0 in range(0, S, rows):
            term = jnp.sum(dy_ref_or_pad[pl.ds(dy_off + r0, rows), :]
                           * upad_ref[pl.ds(CONV_PAD + r0 - (K - 1) + k, rows), :], axis=0, keepdims=True)
            acc = term if acc is None else acc + term
        dw_ref[pl.ds(k, 1), :] = acc


def _col_spec(S, off):
    return pl.BlockSpec((S, LANE), lambda j: (0, off // LANE + j))


def _sc_fwd(proj, conv_w, l, offs, DS, name):
    S = proj.shape[0]
    K = conv_w.shape[1]
    rows = min(CONV_ROWS, S)

    def body(b_ref, c_ref, u_ref, w_ref, o_ref, pad_ref):
        _zero_pads(pad_ref, S)
        pad_ref[pl.ds(CONV_PAD, S), :] = c_ref[...].astype(F32) * u_ref[...].astype(F32)
        for r0 in range(0, S, rows):
            cv = _conv_fwd_chunk(pad_ref, w_ref, K, r0, rows)
            o_ref[pl.ds(r0, rows), :] = (b_ref[pl.ds(r0, rows), :].astype(F32) * cv).astype(BF16)

    return pl.pallas_call(
        body, name=name, grid=(DS // LANE,),
        in_specs=[_col_spec(S, offs["sc_b"]), _col_spec(S, offs["sc_c"]), _col_spec(S, offs["sc_u"]),
                  pl.BlockSpec((None, K, LANE), lambda j: (l, 0, j))],
        out_specs=pl.BlockSpec((S, LANE), lambda j: (0, j)),
        out_shape=jax.ShapeDtypeStruct((S, DS), BF16),
        scratch_shapes=[pltpu.VMEM((S + 2 * CONV_PAD, LANE), F32)],
        compiler_params=_params("parallel"),
    )(proj, proj, proj, conv_w)


def _sc_bwd(dga, proj, conv_w, l, offs, DS, name):
    S = proj.shape[0]
    K = conv_w.shape[1]
    rows = min(CONV_ROWS, S)

    def body(d_ref, b_ref, c_ref, u_ref, w_ref, db_ref, dc_ref, du_ref, dw_ref, tpad_ref, gpad_ref):
        _zero_pads(tpad_ref, S)
        _zero_pads(gpad_ref, S)
        tpad_ref[pl.ds(CONV_PAD, S), :] = c_ref[...].astype(F32) * u_ref[...].astype(F32)
        for r0 in range(0, S, rows):
            sl = pl.ds(r0, rows)
            cv = _conv_fwd_chunk(tpad_ref, w_ref, K, r0, rows)
            d = d_ref[sl, :]
            db_ref[sl, :] = (d * cv).astype(BF16)
            gpad_ref[pl.ds(CONV_PAD + r0, rows), :] = d * b_ref[sl, :].astype(F32)
        for r0 in range(0, S, rows):
            sl = pl.ds(r0, rows)
            dt = _conv_bwd_chunk(gpad_ref, w_ref, K, r0, rows)
            dc_ref[sl, :] = (dt * u_ref[sl, :].astype(F32)).astype(BF16)
            du_ref[sl, :] = (dt * c_ref[sl, :].astype(F32)).astype(BF16)
        _conv_dw(tpad_ref, gpad_ref, CONV_PAD, K, S, dw_ref)

    blk = pl.BlockSpec((S, LANE), lambda j: (0, j))
    act = jax.ShapeDtypeStruct((S, DS), BF16)
    return pl.pallas_call(
        body, name=name, grid=(DS // LANE,),
        in_specs=[blk, _col_spec(S, offs["sc_b"]), _col_spec(S, offs["sc_c"]), _col_spec(S, offs["sc_u"]),
                  pl.BlockSpec((None, K, LANE), lambda j: (l, 0, j))],
        out_specs=[blk, blk, blk, pl.BlockSpec((K, LANE), lambda j: (0, j))],
        out_shape=[act, act, act, jax.ShapeDtypeStruct((K, DS), F32)],
        scratch_shapes=[pltpu.VMEM((S + 2 * CONV_PAD, LANE), F32), pltpu.VMEM((S + 2 * CONV_PAD, LANE), F32)],
        compiler_params=_params("parallel"),
    )(dga, proj, proj, proj, conv_w)


def _cf_conv_fwd(proj, conv_w, conv_b3, l, offs, DC, name):
    S = proj.shape[0]
    K = conv_w.shape[1]
    rows = min(CONV_ROWS, S)

    def body(a_ref, g_ref, w_ref, bias_ref, o_ref, pad_ref):
        _zero_pads(pad_ref, S)
        pad_ref[pl.ds(CONV_PAD, S), :] = a_ref[...].astype(F32) * _sigmoid(g_ref[...].astype(F32))
        for r0 in range(0, S, rows):
            o_ref[pl.ds(r0, rows), :] = _conv_fwd_chunk(pad_ref, w_ref, K, r0, rows) + bias_ref[...]

    return pl.pallas_call(
        body, name=name, grid=(DC // LANE,),
        in_specs=[_col_spec(S, offs["cf_a"]), _col_spec(S, offs["cf_g"]),
                  pl.BlockSpec((None, K, LANE), lambda j: (l, 0, j)),
                  pl.BlockSpec((None, 1, LANE), lambda j: (l, 0, j))],
        out_specs=pl.BlockSpec((S, LANE), lambda j: (0, j)),
        out_shape=jax.ShapeDtypeStruct((S, DC), F32),
        scratch_shapes=[pltpu.VMEM((S + 2 * CONV_PAD, LANE), F32)],
        compiler_params=_params("parallel"),
    )(proj, proj, conv_w, conv_b3)


def _layer_norm_hat(u):
    mu = jnp.mean(u, axis=-1, keepdims=True)
    xc = u - mu
    rstd = lax.rsqrt(jnp.mean(xc * xc, axis=-1, keepdims=True) + LN_EPS)
    return xc * rstd, rstd


def _cf_norm_fwd(u1, gam3, bet3, l, name):
    S, DC = u1.shape
    tr = _row_tile(S)

    def body(u_ref, g_ref, b_ref, o_ref):
        xhat, _ = _layer_norm_hat(u_ref[...])
        s = xhat * g_ref[...] + b_ref[...]
        o_ref[...] = (s * _sigmoid(s)).astype(BF16)

    row = pl.BlockSpec((tr, DC), lambda i: (i, 0))
    vec = pl.BlockSpec((None, 1, DC), lambda i: (l, 0, 0))
    return pl.pallas_call(
        body, name=name, grid=(S // tr,),
        in_specs=[row, vec, vec], out_specs=row,
        out_shape=jax.ShapeDtypeStruct((S, DC), BF16),
        compiler_params=_params("parallel"),
    )(u1, gam3, bet3)


def _cf_norm_bwd(du2, u1, gam3, bet3, l, name):
    S, DC = u1.shape
    tr = _row_tile(S)

    def body(d_ref, u_ref, g_ref, b_ref, du_ref, dg_ref, db_ref):
        @pl.when(pl.program_id(0) == 0)
        def _():
            dg_ref[...] = jnp.zeros_like(dg_ref)
            db_ref[...] = jnp.zeros_like(db_ref)

        xhat, rstd = _layer_norm_hat(u_ref[...])
        s = xhat * g_ref[...] + b_ref[...]
        sg = _sigmoid(s)
        ds = d_ref[...] * (sg * (1.0 + s * (1.0 - sg)))
        dg_ref[...] += jnp.sum(ds * xhat, axis=0, keepdims=True)
        db_ref[...] += jnp.sum(ds, axis=0, keepdims=True)
        dxh = ds * g_ref[...]
        du_ref[...] = rstd * (dxh - jnp.mean(dxh, axis=-1, keepdims=True)
                              - xhat * jnp.mean(dxh * xhat, axis=-1, keepdims=True))

    row = pl.BlockSpec((tr, DC), lambda i: (i, 0))
    vec = pl.BlockSpec((None, 1, DC), lambda i: (l, 0, 0))
    acc = pl.BlockSpec((1, DC), lambda i: (0, 0))
    return pl.pallas_call(
        body, name=name, grid=(S // tr,),
        in_specs=[row, row, vec, vec], out_specs=[row, acc, acc],
        out_shape=[jax.ShapeDtypeStruct((S, DC), F32), jax.ShapeDtypeStruct((1, DC), F32),
                   jax.ShapeDtypeStruct((1, DC), F32)],
        compiler_params=_params("arbitrary"),
    )(du2, u1, gam3, bet3)


def _cf_conv_bwd(du1, proj, conv_w, l, offs, DC, name):
    S = proj.shape[0]
    K = conv_w.shape[1]
    rows = min(CONV_ROWS, S)

    def body(d_ref, a_ref, g_ref, w_ref, da_ref, dgl_ref, dw_ref, dbias_ref, upad_ref, dpad_ref):
        _zero_pads(upad_ref, S)
        _zero_pads(dpad_ref, S)
        upad_ref[pl.ds(CONV_PAD, S), :] = a_ref[...].astype(F32) * _sigmoid(g_ref[...].astype(F32))
        dpad_ref[pl.ds(CONV_PAD, S), :] = d_ref[...]
        dbias_ref[...] = jnp.sum(d_ref[...], axis=0, keepdims=True)
        for r0 in range(0, S, rows):
            sl = pl.ds(r0, rows)
            du0 = _conv_bwd_chunk(dpad_ref, w_ref, K, r0, rows)
            a = a_ref[sl, :].astype(F32)
            sg = _sigmoid(g_ref[sl, :].astype(F32))
            da_ref[sl, :] = (du0 * sg).astype(BF16)
            dgl_ref[sl, :] = (du0 * a * sg * (1.0 - sg)).astype(BF16)
        _conv_dw(upad_ref, dpad_ref, CONV_PAD, K, S, dw_ref)

    blk = pl.BlockSpec((S, LANE), lambda j: (0, j))
    act = jax.ShapeDtypeStruct((S, DC), BF16)
    return pl.pallas_call(
        body, name=name, grid=(DC // LANE,),
        in_specs=[blk, _col_spec(S, offs["cf_a"]), _col_spec(S, offs["cf_g"]),
                  pl.BlockSpec((None, K, LANE), lambda j: (l, 0, j))],
        out_specs=[blk, blk, pl.BlockSpec((K, LANE), lambda j: (0, j)), pl.BlockSpec((1, LANE), lambda j: (0, j))],
        out_shape=[act, act, jax.ShapeDtypeStruct((K, DC), F32), jax.ShapeDtypeStruct((1, DC), F32)],
        scratch_shapes=[pltpu.VMEM((S + 2 * CONV_PAD, LANE), F32), pltpu.VMEM((S + 2 * CONV_PAD, LANE), F32)],
        compiler_params=_params("parallel"),
    )(du1, proj, proj, conv_w)


def _dot_nt(a, b):
    return lax.dot_general(a, b, (((1,), (1,)), ((), ())), preferred_element_type=F32)


def _dot_nn(a, b):
    return lax.dot_general(a, b, (((1,), (0,)), ((), ())), preferred_element_type=F32)


def _dot_tn(a, b):
    return lax.dot_general(a, b, (((0,), (0,)), ((), ())), preferred_element_type=F32)


def _dot_split(x, u):
    hi = x.astype(BF16)
    lo = (x - hi.astype(F32)).astype(BF16)
    return _dot_nn(hi, u) + _dot_nn(lo, u)


MASKED = -1e30


def _log_fail(z):
    return -(jnp.maximum(z, 0.0) + jnp.log(1.0 + jnp.exp(-jnp.abs(z))))


def _head_group(H, want):
    g = min(want, H)
    while H % g:
        g -= 1
    return g


def _lanes(g):
    return slice(g * HEAD_DIM, (g + 1) * HEAD_DIM)


def _attn_fwd(proj, offs, DA, name):
    S = proj.shape[0]
    H = DA // HEAD_DIM
    G = _head_group(H, ATTN_FWD_HEADS)
    nb = S // QB

    def body(q_ref, k_ref, v_ref, o_ref, tot_ref):
        row = lax.broadcasted_iota(jnp.int32, (QB, QB), 0)
        col = lax.broadcasted_iota(jnp.int32, (QB, QB), 1)
        u_after = (row > col).astype(BF16)

        def rows_of(i, t):
            return pl.ds(pl.multiple_of(jnp.maximum(i - t, 0) * QB, QB), QB)

        def scores(i, t, g, q):
            return _dot_nt(q, k_ref[rows_of(i, t), _lanes(g)]) * (HEAD_DIM ** -0.5)

        def log_terms(i, t, z):
            valid = col < row + jnp.minimum(t, i) * QB
            lf = jnp.where(valid, _log_fail(z), 0.0)
            return jnp.where(valid, lf + z + _dot_split(lf, u_after), MASKED), jnp.sum(lf, axis=1, keepdims=True)

        def q_block(i, _):
            qs = pl.ds(pl.multiple_of(i * QB, QB), QB)
            qg = [q_ref[qs, _lanes(g)] for g in range(G)]

            def step(t, carry):
                out = []
                for g in range(G):
                    acc, c, z, (pre, rs) = carry[g]
                    a = jnp.exp(pre + c)
                    acc = acc + _dot_nn(a.astype(BF16), v_ref[rows_of(i, t - 2), _lanes(g)])
                    out.append((acc, c + rs, scores(i, t, g, qg[g]), log_terms(i, t - 1, z)))
                return tuple(out)

            init = []
            for g in range(G):
                z0 = scores(i, 0, g, qg[g])
                init.append((jnp.zeros((QB, HEAD_DIM), F32), jnp.zeros((QB, 1), F32), scores(i, 1, g, qg[g]),
                             log_terms(i, 0, z0)))
            res = lax.fori_loop(2, i + 3, step, tuple(init))
            for g in range(G):
                o_ref[qs, _lanes(g)] = res[g][0].astype(BF16)
                tot_ref[g, qs, :] = res[g][1]
            return 0

        lax.fori_loop(0, nb, q_block, 0)

    def hs(off):
        return pl.BlockSpec((S, G * HEAD_DIM), lambda h: (0, off // (G * HEAD_DIM) + h))

    return pl.pallas_call(
        body, name=name, grid=(H // G,),
        in_specs=[hs(offs["q"]), hs(offs["k"]), hs(offs["v"])],
        out_specs=[pl.BlockSpec((S, G * HEAD_DIM), lambda h: (0, h)), pl.BlockSpec((G, S, 1), lambda h: (h, 0, 0))],
        out_shape=[jax.ShapeDtypeStruct((S, DA), BF16), jax.ShapeDtypeStruct((H, S, 1), F32)],
        compiler_params=_params("parallel"),
    )(proj, proj, proj)


def _attn_bwd(dout, tot, proj, offs, DA, name):
    S = proj.shape[0]
    H = DA // HEAD_DIM
    G = _head_group(H, ATTN_BWD_HEADS)
    nb = S // QB
    scale = HEAD_DIM ** -0.5

    def body(q_ref, k_ref, v_ref, tot_ref, do_ref, dq_ref, dk_ref, dv_ref, dk_acc, dv_acc, stage_ref):
        row = lax.broadcasted_iota(jnp.int32, (QB, QB), 0)
        col = lax.broadcasted_iota(jnp.int32, (QB, QB), 1)
        u_after = (row > col).astype(BF16)
        u_before = (row < col).astype(BF16)
        dk_acc[...] = jnp.zeros_like(dk_acc)
        dv_acc[...] = jnp.zeros_like(dv_acc)

        def rows_of(i, b):
            return pl.ds(pl.multiple_of(jnp.minimum(b, i) * QB, QB), QB)

        def scores(i, b, g, q, do):
            ks = rows_of(i, b)
            return _dot_nt(q, k_ref[ks, _lanes(g)]) * scale, _dot_nt(do, v_ref[ks, _lanes(g)])

        def log_terms(i, b, z, da):
            valid = col < row + (i - jnp.minimum(b, i)) * QB
            lf = jnp.where(valid, _log_fail(z), 0.0)
            pre = jnp.where(valid, lf + z + _dot_split(lf, u_after), MASKED)
            return (pre, da, jnp.exp(lf), jnp.where(valid, jnp.exp(lf + z), 0.0),
                    jnp.sum(lf, axis=1, keepdims=True))

        def d_log_a(tot_q, seen, terms):
            pre, da, fail, beta, rs = terms
            seen = seen + rs
            a = jnp.exp(pre + (tot_q - seen))
            dlog = a * da
            return seen, (dlog, _dot_split(dlog, u_before), a.astype(BF16), fail, beta)

        def q_block(i, _):
            qs = pl.ds(pl.multiple_of(i * QB, QB), QB)
            qg = [q_ref[qs, _lanes(g)] for g in range(G)]
            dog = [do_ref[qs, _lanes(g)] for g in range(G)]
            totg = [tot_ref[g, qs, :] for g in range(G)]

            def put(g, first, tiles):
                for n, tile in enumerate(tiles):
                    stage_ref[g, first + n] = tile.astype(F32)

            def get(g, first, count):
                return [stage_ref[g, first + n] for n in range(count)]

            def step(t, carry):
                out = []
                for g in range(G):
                    dq, seen, gsum, rs = carry[g]
                    dlog, left, a, fail, beta = get(g, 6, 5)
                    terms = get(g, 2, 4) + [rs]
                    z, da = get(g, 0, 2)
                    ks = rows_of(i, t - 3)
                    dz = (dlog * fail - (gsum + left) * beta) * scale
                    dzb = dz.astype(BF16)
                    dk_acc[ks, _lanes(g)] += _dot_tn(dzb, qg[g])
                    dv_acc[ks, _lanes(g)] += _dot_tn(a.astype(BF16), dog[g])
                    dq = dq + _dot_nn(dzb, k_ref[ks, _lanes(g)])
                    gsum = gsum + jnp.sum(dlog, axis=1, keepdims=True)
                    seen, grads = d_log_a(totg[g], seen, terms)
                    terms = log_terms(i, t - 1, z, da)
                    put(g, 6, grads)
                    put(g, 2, terms[:4])
                    put(g, 0, scores(i, t, g, qg[g], dog[g]))
                    out.append((dq, seen, gsum, terms[4]))
                return tuple(out)

            zero = jnp.zeros((QB, 1), F32)
            init = []
            for g in range(G):
                terms0 = log_terms(i, 0, *scores(i, 0, g, qg[g], dog[g]))
                terms1 = log_terms(i, 1, *scores(i, 1, g, qg[g], dog[g]))
                seen, grads0 = d_log_a(totg[g], zero, terms0)
                put(g, 6, grads0)
                put(g, 2, terms1[:4])
                put(g, 0, scores(i, 2, g, qg[g], dog[g]))
                init.append((jnp.zeros((QB, HEAD_DIM), F32), seen, zero, terms1[4]))
            res = lax.fori_loop(3, i + 4, step, tuple(init))
            for g in range(G):
                dq_ref[qs, _lanes(g)] = res[g][0].astype(BF16)
            return 0

        lax.fori_loop(0, nb, q_block, 0)
        dk_ref[...] = dk_acc[...].astype(BF16)
        dv_ref[...] = dv_acc[...].astype(BF16)

    def hs(off):
        return pl.BlockSpec((S, G * HEAD_DIM), lambda h: (0, off // (G * HEAD_DIM) + h))

    head = pl.BlockSpec((S, G * HEAD_DIM), lambda h: (0, h))
    act = jax.ShapeDtypeStruct((S, DA), BF16)
    return pl.pallas_call(
        body, name=name, grid=(H // G,),
        in_specs=[hs(offs["q"]), hs(offs["k"]), hs(offs["v"]), pl.BlockSpec((G, S, 1), lambda h: (h, 0, 0)), head],
        out_specs=[head, head, head], out_shape=[act, act, act],
        scratch_shapes=[pltpu.VMEM((S, G * HEAD_DIM), F32), pltpu.VMEM((S, G * HEAD_DIM), F32),
                        pltpu.VMEM((G, 11, QB, QB), F32)],
        compiler_params=_params("parallel"),
    )(proj, proj, proj, tot, dout)


def _merge_tiles(S, D, goff):
    tn = LANE
    for t in range(LANE, 513, LANE):
        if D % t == 0 and goff % t == 0:
            tn = t
    return _tile(S, 512, 8), tn


def _merge_fwd(ga, attn, u2, pa, pb, pc, proj, goff, name):
    S = ga.shape[0]
    D = pa.shape[-1]
    tm, tn = _merge_tiles(S, D, goff)

    def body(ga_ref, at_ref, u2_ref, pa_ref, pb_ref, pc_ref, la_ref, lb_ref, lc_ref, ya_ref, yb_ref, yc_ref, m_ref):
        ya = _dot_nn(ga_ref[...], pa_ref[...])
        yb = _dot_nn(at_ref[...], pb_ref[...])
        yc = _dot_nn(u2_ref[...], pc_ref[...])
        ya_ref[...] = ya.astype(BF16)
        yb_ref[...] = yb.astype(BF16)
        yc_ref[...] = yc.astype(BF16)
        m_ref[...] = (_sigmoid(la_ref[...].astype(F32)) * ya + _sigmoid(lb_ref[...].astype(F32)) * yb
                      + _sigmoid(lc_ref[...].astype(F32)) * yc).astype(BF16)

    def lhs(a):
        return pl.BlockSpec((tm, a.shape[1]), lambda i, j: (i, 0))

    def rhs(p):
        return pl.BlockSpec((p.shape[0], tn), lambda i, j: (0, j))

    def gate(r):
        return pl.BlockSpec((tm, tn), lambda i, j: (i, (goff + r * D) // tn + j))

    out = pl.BlockSpec((tm, tn), lambda i, j: (i, j))
    act = jax.ShapeDtypeStruct((S, D), BF16)
    return pl.pallas_call(
        body, name=name, grid=(S // tm, D // tn),
        in_specs=[lhs(ga), lhs(attn), lhs(u2), rhs(pa), rhs(pb), rhs(pc), gate(0), gate(1), gate(2)],
        out_specs=[out, out, out, out], out_shape=[act, act, act, act],
        compiler_params=_params("parallel", "parallel"),
    )(ga, attn, u2, pa, pb, pc, proj, proj, proj)


def _merge_bwd(dm, ya, yb, yc, proj, goff, name):
    S, D = dm.shape
    tm, tn = _merge_tiles(S, D, goff)

    def body(dm_ref, ya_ref, yb_ref, yc_ref, la_ref, lb_ref, lc_ref, *o_refs):
        dmv = dm_ref[...]
        for y_ref, l_ref, dy_ref, dl_ref in zip((ya_ref, yb_ref, yc_ref), (la_ref, lb_ref, lc_ref),
                                                o_refs[:3], o_refs[3:]):
            sg = _sigmoid(l_ref[...].astype(F32))
            dy_ref[...] = (dmv * sg).astype(BF16)
            dl_ref[...] = (dmv * y_ref[...].astype(F32) * sg * (1.0 - sg)).astype(BF16)

    def gate(r):
        return pl.BlockSpec((tm, tn), lambda i, j: (i, (goff + r * D) // tn + j))

    blk = pl.BlockSpec((tm, tn), lambda i, j: (i, j))
    act = jax.ShapeDtypeStruct((S, D), BF16)
    return pl.pallas_call(
        body, name=name, grid=(S // tm, D // tn),
        in_specs=[blk, blk, blk, blk, gate(0), gate(1), gate(2)],
        out_specs=[blk] * 6, out_shape=[act] * 6,
        compiler_params=_params("parallel", "parallel"),
    )(dm, ya, yb, yc, proj, proj, proj)


def _ew_tiles(rows, cols):
    tc = cols if cols <= 4096 else _tile(cols, 2048)
    tr = _tile(rows, max(8, (1 << 19) // tc), 8)
    return tr, tc


def _half_rows_tile(Rh, Cs):
    return _tile(Rh, max(16, (1 << 20) // Cs), 16)


def _place_shard(w, l, kind, place, name):
    _, Rs, Cs = w.shape
    tr = _half_rows_tile(Rs, Cs)

    def body(pr_ref, w_ref, o_ref):
        o_ref[...] = w_ref[...].astype(BF16)

    if kind == "col":
        shape = (Rs, 4 * Cs)
        o_spec = pl.BlockSpec((tr, Cs), lambda i, pr: (i, pr[1]))
    else:
        shape = (4, Rs, Cs)
        o_spec = pl.BlockSpec((None, tr, Cs), lambda i, pr: (pr[1], i, 0))
    out = pl.pallas_call(
        body, name=name,
        grid_spec=pltpu.PrefetchScalarGridSpec(
            num_scalar_prefetch=1, grid=(Rs // tr,),
            in_specs=[pl.BlockSpec((None, tr, Cs), lambda i, pr: (l, i, 0))], out_specs=o_spec),
        out_shape=jax.ShapeDtypeStruct(shape, BF16), compiler_params=_params("parallel"),
    )(place, w)
    return out if kind == "col" else out.reshape(4 * Rs, Cs)


def _pair_sum(g, got, kind, place, name):
    _, Rh, Cs = got.shape
    tr = _half_rows_tile(Rh, Cs)
    if kind == "col":
        gv = g.reshape(2, Rh, 4 * Cs)
        g_spec = pl.BlockSpec((None, tr, Cs), lambda p, i, pr: (pr[0], i, p))
    else:
        gv = g.reshape(4, 2, Rh, Cs)
        g_spec = pl.BlockSpec((None, None, tr, Cs), lambda p, i, pr: (p, pr[0], i, 0))
    blk = pl.BlockSpec((None, tr, Cs), lambda p, i, pr: (p, i, 0))

    def body(pr_ref, g_ref, r_ref, o_ref):
        o_ref[...] = (g_ref[...].astype(F32) + r_ref[...].astype(F32)).astype(BF16)

    return pl.pallas_call(
        body, name=name,
        grid_spec=pltpu.PrefetchScalarGridSpec(num_scalar_prefetch=1, grid=(4, Rh // tr),
                                               in_specs=[g_spec, blk], out_specs=blk),
        out_shape=jax.ShapeDtypeStruct(got.shape, BF16), compiler_params=_params("parallel", "parallel"),
    )(place, gv, got)


def _chip_sum(part, got, place, stack, name):
    n, Rh, Cs = got.shape
    l, L, buf = stack
    tr = _half_rows_tile(Rh, Cs)
    prev = () if buf is None else (buf,)

    def body(pr_ref, p_ref, r_ref, *rest):
        acc = p_ref[...].astype(F32)
        for s in range(n):
            acc = acc + r_ref[s].astype(F32)
        rest[-1][...] = acc

    return pl.pallas_call(
        body, name=name,
        grid_spec=pltpu.PrefetchScalarGridSpec(
            num_scalar_prefetch=1, grid=(Rh // tr,),
            in_specs=[pl.BlockSpec((None, tr, Cs), lambda i, pr: (pr[1], i, 0)),
                      pl.BlockSpec((n, tr, Cs), lambda i, pr: (0, i, 0))] + [ANY] * len(prev),
            out_specs=pl.BlockSpec((None, None, tr, Cs), lambda i, pr: (l, pr[0], i, 0))),
        out_shape=jax.ShapeDtypeStruct((L, 2, Rh, Cs), F32), input_output_aliases={3: 0} if prev else {},
        compiler_params=_params("parallel"),
    )(place, part, got, *prev)


def _adamw(w, g, m, v, name):
    shape = w.shape
    args = [t.reshape(-1, shape[-1]) for t in (w, g, m, v)]
    rows, cols = args[0].shape
    tr, tc = _ew_tiles(rows, cols)
    c1 = 1.0 - ADAM_B1 ** ADAM_STEP
    c2 = 1.0 - ADAM_B2 ** ADAM_STEP

    def body(w_ref, g_ref, m_ref, v_ref, d_ref, nm_ref, nv_ref, go_ref):
        gv = g_ref[...]
        nm = ADAM_B1 * m_ref[...] + (1.0 - ADAM_B1) * gv
        nv = ADAM_B2 * v_ref[...] + (1.0 - ADAM_B2) * (gv * gv)
        nm_ref[...] = nm
        nv_ref[...] = nv
        go_ref[...] = gv
        d_ref[...] = -ADAM_LR * ((nm / c1) / (jnp.sqrt(nv / c2) + ADAM_EPS) + ADAM_WD * w_ref[...])

    blk = pl.BlockSpec((tr, tc), lambda i, j: (i, j))
    shp = jax.ShapeDtypeStruct((rows, cols), F32)
    outs = pl.pallas_call(
        body, name=name, grid=(rows // tr, cols // tc), in_specs=[blk] * 4, out_specs=[blk] * 4,
        out_shape=[shp] * 4, compiler_params=_params("parallel", "parallel"),
    )(*args)
    return [o.reshape(shape) for o in outs]


def _place():
    x, y, c = lax.axis_index("x"), lax.axis_index("y"), lax.axis_index("c")
    chips = [(1 - x, y), (x, 1 - y), (1 - x, 1 - y)]
    return x, y, c, chips


def _al(v, unit):
    return pl.multiple_of(v, unit) if unit % LANE == 0 else v


def _half_of_full(ref, kind, p, half, Rs, Cs):
    Rh = (ref.shape[-2] // 2) if kind == "col" else Rs // 2
    lead = (slice(None),) * (len(ref.shape) - 2)
    if kind == "col":
        return ref.at[lead + (pl.ds(_al(half * Rh, Rh), Rh), pl.ds(_al(p * Cs, Cs), Cs))]
    return ref.at[lead + (pl.ds(_al(p * Rs + half * Rh, Rh), Rh), slice(None))]


HBM = pl.BlockSpec(memory_space=pltpu.HBM)
SEM = pl.BlockSpec(memory_space=pltpu.SEMAPHORE)
EFFECT = pltpu.SideEffectType.DATAFLOW_SIDE_EFFECTING
FIRST_GATHER_ID = 0
GATHER_ID = 2
REDUCE_ID = 1


def _in_hbm(v):
    return pltpu.with_memory_space_constraint(v, pltpu.HBM)


def _ici_handshake(chips, c):
    barrier = pltpu.get_barrier_semaphore()
    for px, py in chips:
        pl.semaphore_signal(barrier, inc=1, device_id=(px, py, c), device_id_type=MESH)
    pl.semaphore_wait(barrier, len(chips))


def _shard_dims(ref, kind):
    R, C = ref.shape[-2:]
    return (R, C // 4) if kind == "col" else (R // 4, C)


def _gather_start(groups, after, collective_id, name):
    bufs = [b for grp in groups for b, _ in grp]
    kinds = [k for grp in groups for _, k in grp]
    m, ng = len(bufs), len(groups)

    def body(*refs):
        ins = refs[:m]
        sems = refs[m + 1:m + 1 + 2 * ng]
        token = refs[-1]
        x, y, c, chips = _place()
        _ici_handshake(chips, c)
        me = 2 * x + y
        at = 0
        for gi, grp in enumerate(groups):
            n = len(grp)
            for j, chip in enumerate(chips):
                for w in range(n):
                    ref, kind = ins[at + w], kinds[at + w]
                    mine = _half_of_full(ref, kind, me, c, *_shard_dims(ref, kind))
                    pltpu.make_async_remote_copy(mine, mine, sems[2 * gi].at[j * n + w], sems[2 * gi + 1].at[j * n + w],
                                                 device_id=(*chip, c), device_id_type=MESH).start()
            at += n
        token[...] = jnp.zeros_like(token)

    sem_shapes = [pltpu.SemaphoreType.DMA((3 * len(grp),)) for grp in groups for _ in range(2)]
    outs = pl.pallas_call(
        body, name=name, in_specs=[HBM] * m + [ANY],
        out_specs=[SEM] * (2 * ng) + [HBM] * m + [pl.BlockSpec(memory_space=pltpu.VMEM)],
        out_shape=sem_shapes + [pltpu.HBM(b.shape, b.dtype) for b in bufs] + [jax.ShapeDtypeStruct((8, LANE), F32)],
        input_output_aliases={i: 2 * ng + i for i in range(m)},
        compiler_params=pltpu.CompilerParams(has_side_effects=EFFECT, collective_id=collective_id),
    )(*[_in_hbm(b) for b in bufs], after)
    res, at = [], 2 * ng
    for gi, grp in enumerate(groups):
        res.append((outs[2 * gi], outs[2 * gi + 1], outs[at:at + len(grp)]))
        at += len(grp)
    return res, outs[-1]


def _gather_wait(bufs, kinds, send_sem, recv_sem, after, name):
    n = len(bufs)

    def body(*refs):
        ins = refs[:n]
        send, recv = refs[n], refs[n + 1]
        x, y, c, chips = _place()
        me = 2 * x + y
        for j, (px, py) in enumerate(chips):
            for w in range(n):
                dims = _shard_dims(ins[w], kinds[w])
                mine = _half_of_full(ins[w], kinds[w], me, c, *dims)
                theirs = _half_of_full(ins[w], kinds[w], 2 * px + py, c, *dims)
                cp = pltpu.make_async_remote_copy(mine, theirs, send.at[j * n + w], recv.at[j * n + w],
                                                  device_id=(px, py, c), device_id_type=MESH)
                cp.wait_send()
                cp.wait_recv()

    return pl.pallas_call(
        body, name=name, in_specs=[HBM] * n + [SEM, SEM, ANY], out_specs=[HBM] * n,
        out_shape=[pltpu.HBM(b.shape, b.dtype) for b in bufs],
        input_output_aliases={i: i for i in range(n)},
        compiler_params=pltpu.CompilerParams(has_side_effects=EFFECT),
    )(*bufs, send_sem, recv_sem, after)


def _ride_forward(bufs, kinds):
    n = len(bufs)

    def copies(ins, outs, send_sem, recv_sem):
        x, y, c, chips = _place()
        made = []
        for j, (px, py) in enumerate(chips):
            for w in range(n):
                got = _half_of_full(outs[w], kinds[w], 2 * px + py, c, *_shard_dims(outs[w], kinds[w]))
                made.append(pltpu.make_async_remote_copy(got, got, send_sem.at[j * n + w], recv_sem.at[j * n + w],
                                                         device_id=(x, y, 1 - c), device_id_type=MESH))
        return made

    return dict(arrays=list(bufs), out_shape=[jax.ShapeDtypeStruct(b.shape, b.dtype) for b in bufs], alias=True,
                n_sems=3 * n, copies=copies)


def _ride_halves(grads, kinds):
    n = len(grads)
    shapes = []
    for g, kind in zip(grads, kinds):
        R, C = g.shape
        shapes.append((4, R // 2, C // 4) if kind == "col" else (4, R // 8, C))

    def copies(ins, outs, send_sem, recv_sem):
        x, y, c, _ = _place()
        made = []
        for w in range(n):
            _, Rh, Cs = shapes[w]
            for p in range(4):
                made.append(pltpu.make_async_remote_copy(
                    _half_of_full(ins[w], kinds[w], p, 1 - c, 2 * Rh, Cs), outs[w].at[p], send_sem.at[4 * w + p],
                    recv_sem.at[4 * w + p], device_id=(x, y, 1 - c), device_id_type=MESH))
        return made

    return dict(arrays=list(grads), out_shape=[jax.ShapeDtypeStruct(s, BF16) for s in shapes], alias=False,
                n_sems=4 * n, copies=copies)


def _exchange_now(ride, name):
    n = len(ride["arrays"])

    def body(*refs):
        made = ride["copies"](refs[:n], refs[n:2 * n], refs[2 * n], refs[2 * n + 1])
        for cp in made:
            cp.start()
        for cp in made:
            cp.wait()

    return pl.pallas_call(
        body, name=name, in_specs=[ANY] * n, out_specs=[ANY] * n, out_shape=ride["out_shape"],
        input_output_aliases={w: w for w in range(n)} if ride["alias"] else {},
        scratch_shapes=[pltpu.SemaphoreType.DMA((ride["n_sems"],))] * 2,
    )(*ride["arrays"])


def _reduce_start(parts, name):
    n = len(parts)
    zones = [lax.empty((3,) + p.shape[1:], p.dtype) for p in parts]

    def body(*refs):
        ins, lands = refs[:n], refs[n:2 * n]
        send, recv = refs[2 * n], refs[2 * n + 1]
        token = refs[-1]
        x, y, c, chips = _place()
        _ici_handshake(chips, c)
        for j, (px, py) in enumerate(chips):
            for w in range(n):
                pltpu.make_async_remote_copy(ins[w].at[2 * px + py], lands[w].at[j], send.at[j * n + w],
                                             recv.at[j * n + w], device_id=(px, py, c), device_id_type=MESH).start()
        token[...] = jnp.zeros_like(token)

    both = list(parts) + zones
    outs = pl.pallas_call(
        body, name=name, in_specs=[HBM] * (2 * n),
        out_specs=[SEM, SEM] + [HBM] * (2 * n) + [pl.BlockSpec(memory_space=pltpu.VMEM)],
        out_shape=([pltpu.SemaphoreType.DMA((3 * n,))] * 2 + [pltpu.HBM(b.shape, b.dtype) for b in both]
                   + [jax.ShapeDtypeStruct((8, LANE), F32)]),
        input_output_aliases={i: 2 + i for i in range(2 * n)},
        compiler_params=pltpu.CompilerParams(has_side_effects=EFFECT, collective_id=REDUCE_ID),
    )(*[_in_hbm(b) for b in both])
    return outs[0], outs[1], outs[2:2 + n], outs[2 + n:2 + 2 * n], outs[-1]


def _reduce_wait(parts, zones, send_sem, recv_sem, after, name):
    n = len(parts)

    def body(*refs):
        ins, lands = refs[:n], refs[n:2 * n]
        send, recv = refs[2 * n], refs[2 * n + 1]
        x, y, c, chips = _place()
        for j, (px, py) in enumerate(chips):
            for w in range(n):
                cp = pltpu.make_async_remote_copy(ins[w].at[2 * px + py], lands[w].at[j], send.at[j * n + w],
                                                  recv.at[j * n + w], device_id=(px, py, c), device_id_type=MESH)
                cp.wait_send()
                cp.wait_recv()

    both = list(parts) + list(zones)
    outs = pl.pallas_call(
        body, name=name, in_specs=[HBM] * (2 * n) + [SEM, SEM, ANY], out_specs=[HBM] * (2 * n),
        out_shape=[pltpu.HBM(b.shape, b.dtype) for b in both],
        input_output_aliases={i: i for i in range(2 * n)},
        compiler_params=pltpu.CompilerParams(has_side_effects=EFFECT),
    )(*both, send_sem, recv_sem, after)
    return outs[:n], outs[n:]


def _share_with_sibling(reduced, name):
    n = len(reduced)

    def body(*refs):
        outs = refs[n:2 * n]
        send_sem, recv_sem = refs[2 * n:]
        x, y, c, _ = _place()
        sib = (x, y, 1 - c)
        copies = []
        for w in range(n):
            mine = outs[w].at[:, c]
            cp = pltpu.make_async_remote_copy(mine, mine, send_sem.at[w], recv_sem.at[w], device_id=sib,
                                              device_id_type=MESH)
            cp.start()
            copies.append(cp)
        for cp in copies:
            cp.wait()

    return pl.pallas_call(
        body, name=name, in_specs=[ANY] * n, out_specs=[ANY] * n,
        out_shape=[jax.ShapeDtypeStruct(r.shape, r.dtype) for r in reduced],
        input_output_aliases={w: w for w in range(n)},
        scratch_shapes=[pltpu.SemaphoreType.DMA((n,))] * 2,
    )(*reduced)


def _all_gather_small(v):
    r = v.shape[0]

    def body(v_ref, o_ref, send_sem, recv_sem):
        x, y, c, _ = _place()
        me = 4 * x + 2 * y + c
        o_ref[me] = v_ref[...]
        copies = []
        for k in range(1, 8):
            peer = (x ^ (k >> 2), y ^ ((k >> 1) & 1), c ^ (k & 1))
            cp = pltpu.make_async_remote_copy(v_ref, o_ref.at[me], send_sem.at[k - 1], recv_sem.at[k - 1],
                                              device_id=peer, device_id_type=MESH)
            cp.start()
            copies.append(cp)
        for cp in copies:
            cp.wait()

    vmem = pl.BlockSpec(memory_space=pltpu.VMEM)
    return pl.pallas_call(
        body, name="all_gather_small", in_specs=[vmem], out_specs=vmem,
        out_shape=jax.ShapeDtypeStruct((8, r, LANE), F32),
        scratch_shapes=[pltpu.SemaphoreType.DMA((7,)), pltpu.SemaphoreType.DMA((7,))],
        compiler_params=pltpu.CompilerParams(vmem_limit_bytes=VMEM_LIMIT),
    )(v)


def _sum_slots(g):
    n, r, _ = g.shape

    def body(g_ref, o_ref):
        acc = g_ref[0]
        for s in range(1, n):
            acc = acc + g_ref[s]
        o_ref[...] = acc

    vmem = pl.BlockSpec(memory_space=pltpu.VMEM)
    return pl.pallas_call(
        body, name="sum_slots", in_specs=[vmem], out_specs=vmem, out_shape=jax.ShapeDtypeStruct((r, LANE), F32),
        compiler_params=pltpu.CompilerParams(vmem_limit_bytes=VMEM_LIMIT),
    )(g)


def _pack(arrays):
    flat = jnp.concatenate([a.reshape(-1) for a in arrays])
    pad = (-flat.shape[0]) % (8 * LANE)
    return jnp.pad(flat, (0, pad)).reshape(-1, LANE)


def _unpack(packed, like):
    flat = packed.reshape(-1)
    out, off = [], 0
    for a in like:
        out.append(flat[off:off + a.size].reshape(a.shape))
        off += a.size
    return out


def _offsets(D):
    DA, DS, DC = D // 2, D // 4, D // 4
    names = ["q", "k", "v", "sc_b", "sc_c", "sc_u", "cf_a", "cf_g", "gate"]
    sizes = [DA, DA, DA, DS, DS, DS, DC, DC, 3 * D]
    offs, o = {}, 0
    for nm, sz in zip(names, sizes):
        offs[nm] = o
        o += sz
    return offs, DA, DS, DC


def _relu2_epilogue(acc):
    r = jnp.maximum(acc, 0.0)
    return acc, r * r


def _drelu2_epilogue(acc, up):
    return (acc * (2.0 * jnp.maximum(up.astype(F32), 0.0)),)


def _local_step(x, target, gains, conv_a_w, conv_c_w, conv_c_b, norm_c_g, norm_c_b, landed, emit_grads):
    S, D = x.shape
    L = gains[0].shape[0]
    offs, DA, DS, DC = _offsets(D)
    goff = offs["gate"]
    g_mix_pre, g_mix_post, g_mlp_pre, g_mlp_post = gains
    cb3, ng3, nb3 = (t.reshape(L, 1, DC) for t in (conv_c_b, norm_c_g, norm_c_b))

    saved = []
    h = _rms_fwd(x, g_mix_pre, 0, "rms_first")
    xin = x
    def whole(l, part, after, carrier=None):
        names, halves, kinds = landed(l, part, after)
        ride = _ride_forward(halves, kinds)
        if carrier is None:
            return list(zip(names, _exchange_now(ride, f"gather_forward_{part}_{l}"))), None
        result, made = carrier(ride)
        return list(zip(names, made)), result

    next_in, _ = whole(0, "in", xin)
    for l in range(L):
        big = dict(next_in)
        proj = _mm(h, big["w_in"], out_dtypes=(BF16,), name=f"fwd_w_in_{l}", tn_cap=512)
        ga = _sc_fwd(proj, conv_a_w, l, offs, DS, f"sc_fwd_{l}")
        attn, attn_tot = _attn_fwd(proj, offs, DA, f"attn_fwd_{l}")
        u1 = _cf_conv_fwd(proj, conv_c_w, cb3, l, offs, DC, f"cf_conv_fwd_{l}")
        u2 = _cf_norm_fwd(u1, ng3, nb3, l, f"cf_norm_fwd_{l}")
        big.update(whole(l, "rest", attn)[0])
        ya, yb, yc, merged = _merge_fwd(ga, attn, u2, big["proj_a"], big["proj_b"], big["proj_c"], proj, goff,
                                        f"merge_fwd_{l}")
        mixed = _mm(merged, big["w_o"], name=f"fwd_w_o_{l}")
        x1, h2 = _post_res_fwd(xin, mixed, g_mix_post, l, g_mlp_pre, l, f"mix_residual_{l}")
        up, act = _mm(h2, big["w_up"], out_dtypes=(BF16, BF16), epilogue=_relu2_epilogue, name=f"fwd_w_up_{l}")
        if l + 1 < L:
            next_in, f = whole(l + 1, "in", act,
                               lambda ride: _mm(act, big["w_down"], name=f"fwd_w_down_{l}", ride=ride))
        else:
            f = _mm(act, big["w_down"], name=f"fwd_w_down_{l}")
        saved.append(dict(big=big, xin=xin, h=h, proj=proj, ga=ga, attn=attn, attn_tot=attn_tot, u1=u1, u2=u2, ya=ya,
                          yb=yb, yc=yc, merged=merged, mixed=mixed, x1=x1, h2=h2, up=up, act=act, f=f))
        if l + 1 < L:
            xin, h = _post_res_fwd(x1, f, g_mlp_post, l, g_mix_pre, l + 1, f"mlp_residual_{l}")
        else:
            dx, loss = _final_fwd_loss(x1, f, g_mlp_post, l, target, "loss_head")

    small = {k: [None] * L for k in ("mix_pre", "mix_post", "mlp_pre", "mlp_post", "conv_a_w", "conv_c_w",
                                       "conv_c_b", "norm_c_g", "norm_c_b")}

    def dw(key, a, b, l, **kw):
        return _mm(a, b, ta=True, out_dtypes=(BF16,), name=f"d{key}_{l}", **kw)

    def halves_of(g):
        names = [k for k in BIG if k in g]
        return names, [BIG_KIND[k] for k in names]

    for l in reversed(range(L)):
        s = saved[l]
        big = s["big"]
        g = {}
        df, small["mlp_post"][l] = _post_bwd(dx, s["f"], g_mlp_post, l, f"mlp_post_bwd_{l}")
        g["w_down"] = dw("w_down", s["act"], df, l)
        dup = _mm(df, big["w_down"], tb=True, out_dtypes=(BF16,), epilogue=_drelu2_epilogue, extras=(s["up"],),
                  name=f"d_up_{l}")
        g["w_up"] = dw("w_up", s["h2"], dup, l)
        names, kinds = halves_of(g)
        dh2, got = _mm(dup, big["w_up"], tb=True, name=f"d_h2_{l}", ride=_ride_halves([g[k] for k in names], kinds))
        dx1, small["mlp_pre"][l] = _pre_bwd(dx, dh2, s["x1"], g_mlp_pre, l, f"mlp_pre_bwd_{l}")
        g_mix_post = g_mix_post + emit_grads(l, "mlp", names, [g[k] for k in names], got)[0, 0]
        g = {}
        dmixed, small["mix_post"][l] = _post_bwd(dx1, s["mixed"], g_mix_post, l, f"mix_post_bwd_{l}")
        g["w_o"] = dw("w_o", s["merged"], dmixed, l)
        dmerged = _mm(dmixed, big["w_o"], tb=True, name=f"d_merged_{l}")
        dya, dyb, dyc, dla, dlb, dlc = _merge_bwd(dmerged, s["ya"], s["yb"], s["yc"], s["proj"], goff,
                                                  f"merge_bwd_{l}")
        g["proj_a"] = dw("proj_a", s["ga"], dya, l)
        g["proj_b"] = dw("proj_b", s["attn"], dyb, l)
        g["proj_c"] = dw("proj_c", s["u2"], dyc, l)
        dga = _mm(dya, big["proj_a"], tb=True, name=f"d_ga_{l}")
        dattn = _mm(dyb, big["proj_b"], tb=True, out_dtypes=(BF16,), name=f"d_attn_{l}")
        du2 = _mm(dyc, big["proj_c"], tb=True, name=f"d_u2_{l}")
        dsb, dsc, dsu, small["conv_a_w"][l] = _sc_bwd(dga, s["proj"], conv_a_w, l, offs, DS, f"sc_bwd_{l}")
        du1, small["norm_c_g"][l], small["norm_c_b"][l] = _cf_norm_bwd(du2, s["u1"], ng3, nb3, l, f"cf_norm_bwd_{l}")
        dca, dcg, small["conv_c_w"][l], small["conv_c_b"][l] = _cf_conv_bwd(du1, s["proj"], conv_c_w, l, offs, DC,
                                                                          f"cf_conv_bwd_{l}")
        dq, dk, dv = _attn_bwd(dattn, s["attn_tot"], s["proj"], offs, DA, f"attn_bwd_{l}")
        dproj = jnp.concatenate([dq, dk, dv, dsb, dsc, dsu, dca, dcg, dla, dlb, dlc], axis=1)
        g["w_in"] = dw("w_in", s["h"], dproj, l, tn_cap=512)
        names, kinds = halves_of(g)
        dh, got = _mm(dproj, big["w_in"], tb=True, name=f"d_h_{l}", tk_cap=3072,
                      ride=_ride_halves([g[k] for k in names], kinds))
        dx, small["mix_pre"][l] = _pre_bwd(dx1, dh, s["xin"], g_mix_pre, l, f"mix_pre_bwd_{l}")
        g_mlp_post = g_mlp_post + emit_grads(l, "mix", names, [g[k] for k in names], got)[0, 0]
    return loss, dx, small


BIG = ("w_in", "proj_a", "proj_b", "proj_c", "w_o", "w_up", "w_down")
BIG_KIND = {"w_in": "col", "proj_a": "col", "proj_b": "col", "proj_c": "col", "w_o": "row", "w_up": "col",
            "w_down": "row"}


def kernel(x, ln_mix_pre, ln_mix_post, ln_mlp_pre, ln_mlp_post, w_in, conv_a_w, proj_a, proj_b, conv_c_w, conv_c_b, norm_c_g, norm_c_b, proj_c, w_o, w_up, w_down, loss_target, m_ln_mix_pre, m_ln_mix_post, m_ln_mlp_pre, m_ln_mlp_post, m_w_in, m_conv_a_w, m_proj_a, m_proj_b, m_conv_c_w, m_conv_c_b, m_norm_c_g, m_norm_c_b, m_proj_c, m_w_o, m_w_up, m_w_down, v_ln_mix_pre, v_ln_mix_post, v_ln_mlp_pre, v_ln_mlp_post, v_w_in, v_conv_a_w, v_proj_a, v_proj_b, v_conv_c_w, v_conv_c_b, v_norm_c_g, v_norm_c_b, v_proj_c, v_w_o, v_w_up, v_w_down):
    weights = dict(ln_mix_pre=ln_mix_pre, ln_mix_post=ln_mix_post, ln_mlp_pre=ln_mlp_pre, ln_mlp_post=ln_mlp_post,
                   w_in=w_in, conv_a_w=conv_a_w, proj_a=proj_a, proj_b=proj_b, conv_c_w=conv_c_w, conv_c_b=conv_c_b,
                   norm_c_g=norm_c_g, norm_c_b=norm_c_b, proj_c=proj_c, w_o=w_o, w_up=w_up, w_down=w_down)
    m_in = dict(ln_mix_pre=m_ln_mix_pre, ln_mix_post=m_ln_mix_post, ln_mlp_pre=m_ln_mlp_pre, ln_mlp_post=m_ln_mlp_post,
                w_in=m_w_in, conv_a_w=m_conv_a_w, proj_a=m_proj_a, proj_b=m_proj_b, conv_c_w=m_conv_c_w,
                conv_c_b=m_conv_c_b, norm_c_g=m_norm_c_g, norm_c_b=m_norm_c_b, proj_c=m_proj_c, w_o=m_w_o,
                w_up=m_w_up, w_down=m_w_down)
    v_in = dict(ln_mix_pre=v_ln_mix_pre, ln_mix_post=v_ln_mix_post, ln_mlp_pre=v_ln_mlp_pre, ln_mlp_post=v_ln_mlp_post,
                w_in=v_w_in, conv_a_w=v_conv_a_w, proj_a=v_proj_a, proj_b=v_proj_b, conv_c_w=v_conv_c_w,
                conv_c_b=v_conv_c_b, norm_c_g=v_norm_c_g, norm_c_b=v_norm_c_b, proj_c=v_proj_c, w_o=v_w_o,
                w_up=v_w_up, w_down=v_w_down)
    order = list(weights)
    L, D = ln_mix_pre.shape
    chip = 2 * lax.axis_index("x") + lax.axis_index("y")
    place = jnp.stack([lax.axis_index("c"), chip]).astype(jnp.int32)

    conv_local = [conv_a_w, conv_c_w]
    slots = _all_gather_small(_pack(conv_local))
    gather_groups = {"in": ("w_in",), "rest": ("proj_a", "proj_b", "proj_c", "w_o", "w_up", "w_down")}
    def placed(layers, where):
        return [[(_place_shard(weights[k], l, BIG_KIND[k], where, f"place_{k}_{l}"), BIG_KIND[k]) for k in names]
                for l in layers for names in gather_groups.values()]

    first, token = _gather_start(placed([0], place), slots, FIRST_GATHER_ID, "gather_start_first")
    later, _ = _gather_start(placed(range(1, L), place + token[0, 0].astype(jnp.int32)), token, GATHER_ID,
                             "gather_start")
    started = first + later
    in_flight = {(l, part): started[l * len(gather_groups) + i]
                 for l in range(L) for i, part in enumerate(gather_groups)}

    def landed(l, part, after):
        names = gather_groups[part]
        kinds = [BIG_KIND[k] for k in names]
        send_sem, recv_sem, bufs = in_flight[l, part]
        return names, _gather_wait(bufs, kinds, send_sem, recv_sem, after, f"gather_wait_{part}_{l}"), kinds

    pending = []

    def emit_grads(l, part, names, glist, got):
        kinds = [BIG_KIND[k] for k in names]
        pair = [_pair_sum(gk, r, kd, place, f"pair_sum_{k}_{l}") for k, kd, gk, r in zip(names, kinds, glist, got)]
        send_sem, recv_sem, parts, zones, token = _reduce_start(pair, f"reduce_start_{part}_{l}")
        pending.append((l, part, names, parts, zones, send_sem, recv_sem))
        return token

    per_chip = [_unpack(slots[4 * px + 2 * py], conv_local) for px in range(2) for py in range(2)]
    conv_a_full = jnp.concatenate([pc[0] for pc in per_chip], axis=-1)
    conv_c_full = jnp.concatenate([pc[1] for pc in per_chip], axis=-1)

    gains = [weights[k].reshape(L, 1, D) for k in ("ln_mix_pre", "ln_mix_post", "ln_mlp_pre", "ln_mlp_post")]
    loss, dx, small = _local_step(x[0], loss_target[0], gains, conv_a_full, conv_c_full, conv_c_b, norm_c_g,
                                  norm_c_b, landed, emit_grads)

    grads, delta, new_m, new_v = {}, {}, {}, {}

    def finish(part, after):
        reduced = {}
        for l, p, names, parts, zones, send_sem, recv_sem in pending:
            if p == part:
                parts, landed = _reduce_wait(parts, zones, send_sem, recv_sem, after, f"reduce_wait_{p}_{l}")
                for k, mine, theirs in zip(names, parts, landed):
                    reduced[k] = _chip_sum(mine, theirs, place, (l, L, reduced.get(k)), f"chip_sum_{k}_{l}")
        for k, r in zip(reduced, _share_with_sibling(list(reduced.values()), f"reduce_share_{part}")):
            whole = r.reshape(r.shape[0], r.shape[1] * r.shape[2], r.shape[3])
            delta[k], new_m[k], new_v[k], grads[k] = _adamw(weights[k], whole, m_in[k], v_in[k], f"adamw_{k}")

    finish("mlp", dx)
    finish("mix", delta["w_down"])

    small_names = ["ln_mix_pre", "ln_mix_post", "ln_mlp_pre", "ln_mlp_post", "conv_a_w", "conv_c_w", "conv_c_b",
                   "norm_c_g", "norm_c_b"]
    small_key = dict(ln_mix_pre="mix_pre", ln_mix_post="mix_post", ln_mlp_pre="mlp_pre", ln_mlp_post="mlp_post")
    small_local = []
    for k in small_names:
        per_layer = small[small_key.get(k, k)]
        stacked = jnp.stack(per_layer)
        small_local.append(stacked.reshape(L, -1) if stacked.shape[1] == 1 else stacked)
    small_sum = _unpack(_sum_slots(_all_gather_small(_pack(small_local))), small_local)
    for k, g in zip(small_names, small_sum):
        if k in ("conv_a_w", "conv_c_w"):
            width = weights[k].shape[-1]
            g = lax.dynamic_slice_in_dim(g, chip * width, width, axis=2)
        grads[k] = g

    packed = [_pack([t[k] for k in small_names]) for t in (weights, grads, m_in, v_in)]
    like = [weights[k] for k in small_names]
    for dst, res in zip((delta, new_m, new_v), _adamw(*packed, "adamw_small")):
        dst.update(zip(small_names, _unpack(res, like)))

    total = lax.psum(loss[0, 0], ("x", "y", "c"))
    return (total, dx[None], *[grads[k] for k in order], *[delta[k] for k in order],
            *[new_m[k] for k in order], *[new_v[k] for k in order])
```

```python
import functools

import jax
import jax.numpy as jnp
from jax import lax
from jax.experimental import pallas as pl
from jax.experimental.pallas import tpu as pltpu

F32 = jnp.float32
BF16 = jnp.bfloat16
MESH = pl.DeviceIdType.MESH

HEAD_DIM = 128
QB = 128
ATTN_FWD_HEADS = 4
ATTN_BWD_HEADS = 4
RMS_EPS = 1e-6
LN_EPS = 1e-5
ADAM_LR = 0.001
ADAM_B1 = 0.9
ADAM_B2 = 0.999
ADAM_EPS = 1e-08
ADAM_WD = 0.01
ADAM_STEP = 10
LANE = 128
VMEM_LIMIT = 56 * 1024 * 1024
CONV_PAD = 32
CONV_ROWS = 256
ANY = pl.BlockSpec(memory_space=pl.ANY)


def _tile(n, cap, mult=LANE):
    best = None
    t = mult
    while t <= min(n, cap):
        if n % t == 0:
            best = t
        t += mult
    return best if best is not None else n


def _params(*sem):
    return pltpu.CompilerParams(dimension_semantics=sem if sem else None, vmem_limit_bytes=VMEM_LIMIT)


def _sigmoid(x):
    return 1.0 / (1.0 + jnp.exp(-x))


def _mm(a, b, *, name, ta=False, tb=False, out_dtypes=(F32,), epilogue=None, extras=(),
        tm_cap=1024, tn_cap=1024, tk_cap=2048, ride=None):
    if ta:
        K, M = a.shape
    else:
        M, K = a.shape
    N = b.shape[0] if tb else b.shape[1]
    tm, tn, tk = _tile(M, tm_cap), _tile(N, tn_cap), _tile(K, tk_cap)
    gm, gn, nk = M // tm, N // tn, K // tk
    a_spec = pl.BlockSpec((tk, tm), lambda i, j, k: (k, i)) if ta else pl.BlockSpec((tm, tk), lambda i, j, k: (i, k))
    b_spec = pl.BlockSpec((tn, tk), lambda i, j, k: (j, k)) if tb else pl.BlockSpec((tk, tn), lambda i, j, k: (k, j))
    e_specs = [pl.BlockSpec((tm, tn), lambda i, j, k: (i, j)) for _ in extras]
    dims = (((0 if ta else 1,), (1 if tb else 0,)), ((), ()))
    n_e, n_o = len(extras), len(out_dtypes)
    n_r = len(ride["arrays"]) if ride else 0

    def body(a_ref, b_ref, *rest):
        e_refs, o_refs = rest[:n_e], rest[n_e + n_r:n_e + n_r + n_o]
        step = (pl.program_id(0) * gn + pl.program_id(1)) * nk + pl.program_id(2)

        def riding():
            at = n_e + n_r + n_o
            send_sem, recv_sem = rest[-2], rest[-1]
            return ride["copies"](rest[n_e:n_e + n_r], rest[at:at + n_r], send_sem, recv_sem)

        if ride:
            @pl.when(step == 0)
            def _():
                for cp in riding():
                    cp.start()

        part = lax.dot_general(a_ref[...].astype(BF16), b_ref[...].astype(BF16), dims, preferred_element_type=F32)

        def finish(acc):
            outs = (acc,) if epilogue is None else epilogue(acc, *[e[...] for e in e_refs])
            for o_ref, o in zip(o_refs, outs):
                o_ref[...] = o.astype(o_ref.dtype)

        if nk == 1:
            finish(part)
        else:
            acc_ref = rest[n_e + 2 * n_r + n_o]
            k = pl.program_id(2)

            @pl.when(k == 0)
            def _():
                acc_ref[...] = part

            @pl.when(k > 0)
            def _():
                acc_ref[...] += part

            @pl.when(k == nk - 1)
            def _():
                finish(acc_ref[...])

        if ride:
            @pl.when(step == gm * gn * nk - 1)
            def _():
                for cp in riding():
                    cp.wait()

    scratch = [pltpu.VMEM((tm, tn), F32)] if nk > 1 else []
    if ride:
        scratch += [pltpu.SemaphoreType.DMA((ride["n_sems"],))] * 2
    outs = pl.pallas_call(
        body, name=name, grid=(gm, gn, nk),
        in_specs=[a_spec, b_spec] + e_specs + [ANY] * n_r,
        out_specs=[pl.BlockSpec((tm, tn), lambda i, j, k: (i, j)) for _ in out_dtypes] + [ANY] * n_r,
        out_shape=[jax.ShapeDtypeStruct((M, N), dt) for dt in out_dtypes] + (ride["out_shape"] if ride else []),
        input_output_aliases={2 + n_e + w: n_o + w for w in range(n_r)} if ride and ride["alias"] else {},
        scratch_shapes=scratch,
        compiler_params=_params(*(("arbitrary",) * 3 if ride else ("parallel", "parallel", "arbitrary"))),
    )(a, b, *extras, *(ride["arrays"] if ride else ()))
    main = outs[0] if n_o == 1 else outs[:n_o]
    return (main, outs[n_o:]) if ride else main


def _row_tile(S):
    return _tile(S, 256, 8)


def _gain_spec(D, l):
    return pl.BlockSpec((None, 1, D), lambda i: (l, 0, 0))


def _rms(x, g):
    r = lax.rsqrt(jnp.mean(x * x, axis=-1, keepdims=True) + RMS_EPS)
    return x * r * g


def _rms_fwd(x, g3, l, name):
    S, D = x.shape
    tr = _row_tile(S)

    def body(x_ref, g_ref, h_ref):
        h_ref[...] = _rms(x_ref[...], g_ref[...]).astype(BF16)

    return pl.pallas_call(
        body, name=name, grid=(S // tr,),
        in_specs=[pl.BlockSpec((tr, D), lambda i: (i, 0)), _gain_spec(D, l)],
        out_specs=pl.BlockSpec((tr, D), lambda i: (i, 0)),
        out_shape=jax.ShapeDtypeStruct((S, D), BF16),
        compiler_params=_params("parallel"),
    )(x, g3)


def _post_res_fwd(x_in, f, gpost3, l, gnext3, lnext, name):
    S, D = x_in.shape
    tr = _row_tile(S)

    def body(x_ref, f_ref, gp_ref, gn_ref, xo_ref, h_ref):
        xo = x_ref[...] + _rms(f_ref[...], gp_ref[...])
        xo_ref[...] = xo
        h_ref[...] = _rms(xo, gn_ref[...]).astype(BF16)

    row = pl.BlockSpec((tr, D), lambda i: (i, 0))
    return pl.pallas_call(
        body, name=name, grid=(S // tr,),
        in_specs=[row, row, _gain_spec(D, l), _gain_spec(D, lnext)],
        out_specs=[row, row],
        out_shape=[jax.ShapeDtypeStruct((S, D), F32), jax.ShapeDtypeStruct((S, D), BF16)],
        compiler_params=_params("parallel"),
    )(x_in, f, gpost3, gnext3)


def _final_fwd_loss(x_in, f, gpost3, l, target, name):
    S, D = x_in.shape
    tr = _row_tile(S)

    def body(x_ref, f_ref, gp_ref, t_ref, dx_ref, loss_ref):
        @pl.when(pl.program_id(0) == 0)
        def _():
            loss_ref[...] = jnp.zeros_like(loss_ref)

        err = x_ref[...] + _rms(f_ref[...], gp_ref[...]) - t_ref[...]
        dx_ref[...] = err * (1.0 / D)
        loss_ref[...] += 0.5 * jnp.sum(jnp.mean(err * err, axis=-1, keepdims=True))

    row = pl.BlockSpec((tr, D), lambda i: (i, 0))
    return pl.pallas_call(
        body, name=name, grid=(S // tr,),
        in_specs=[row, row, _gain_spec(D, l), row],
        out_specs=[row, pl.BlockSpec((8, LANE), lambda i: (0, 0))],
        out_shape=[jax.ShapeDtypeStruct((S, D), F32), jax.ShapeDtypeStruct((8, LANE), F32)],
        compiler_params=_params("arbitrary"),
    )(x_in, f, gpost3, target)


def _rms_bwd_rows(dy, x, g):
    r = lax.rsqrt(jnp.mean(x * x, axis=-1, keepdims=True) + RMS_EPS)
    t = dy * g
    dx = r * t - x * (r * r * r) * jnp.mean(t * x, axis=-1, keepdims=True)
    return dx, dy * x * r


def _post_bwd(dxo, f, gpost3, l, name):
    S, D = f.shape
    tr = _row_tile(S)

    def body(d_ref, f_ref, g_ref, df_ref, dg_ref):
        @pl.when(pl.program_id(0) == 0)
        def _():
            dg_ref[...] = jnp.zeros_like(dg_ref)

        df, dg = _rms_bwd_rows(d_ref[...], f_ref[...], g_ref[...])
        df_ref[...] = df.astype(BF16)
        dg_ref[...] += jnp.sum(dg, axis=0, keepdims=True)

    row = pl.BlockSpec((tr, D), lambda i: (i, 0))
    return pl.pallas_call(
        body, name=name, grid=(S // tr,),
        in_specs=[row, row, _gain_spec(D, l)],
        out_specs=[row, pl.BlockSpec((1, D), lambda i: (0, 0))],
        out_shape=[jax.ShapeDtypeStruct((S, D), BF16), jax.ShapeDtypeStruct((1, D), F32)],
        compiler_params=_params("arbitrary"),
    )(dxo, f, gpost3)


def _pre_bwd(dxo, dh, x_in, gpre3, l, name):
    S, D = x_in.shape
    tr = _row_tile(S)

    def body(d_ref, dh_ref, x_ref, g_ref, dx_ref, dg_ref):
        @pl.when(pl.program_id(0) == 0)
        def _():
            dg_ref[...] = jnp.zeros_like(dg_ref)

        dx, dg = _rms_bwd_rows(dh_ref[...], x_ref[...], g_ref[...])
        dx_ref[...] = d_ref[...] + dx
        dg_ref[...] += jnp.sum(dg, axis=0, keepdims=True)

    row = pl.BlockSpec((tr, D), lambda i: (i, 0))
    return pl.pallas_call(
        body, name=name, grid=(S // tr,),
        in_specs=[row, row, row, _gain_spec(D, l)],
        out_specs=[row, pl.BlockSpec((1, D), lambda i: (0, 0))],
        out_shape=[jax.ShapeDtypeStruct((S, D), F32), jax.ShapeDtypeStruct((1, D), F32)],
        compiler_params=_params("arbitrary"),
    )(dxo, dh, x_in, gpre3)


def _zero_pads(pad_ref, S):
    z = jnp.zeros((CONV_PAD, pad_ref.shape[1]), F32)
    pad_ref[pl.ds(0, CONV_PAD), :] = z
    pad_ref[pl.ds(CONV_PAD + S, CONV_PAD), :] = z


def _conv_fwd_chunk(pad_ref, w_ref, K, r0, rows):
    acc = None
    for k in range(K):
        term = w_ref[pl.ds(k, 1), :] * pad_ref[pl.ds(CONV_PAD + r0 - (K - 1) + k, rows), :]
        acc = term if acc is None else acc + term
    return acc


def _conv_bwd_chunk(pad_ref, w_ref, K, r0, rows):
    acc = None
    for k in range(K):
        term = w_ref[pl.ds(k, 1), :] * pad_ref[pl.ds(CONV_PAD + r0 + (K - 1) - k, rows), :]
        acc = term if acc is None else acc + term
    return acc


def _conv_dw(upad_ref, dy_ref_or_pad, dy_off, K, S, dw_ref):
    rows = min(CONV_ROWS, S)
    for k in range(K):
        acc = None
        for r0 in range(0, S, rows):
            term = jnp.sum(dy_ref_or_pad[pl.ds(dy_off + r0, rows), :]
                           * upad_ref[pl.ds(CONV_PAD + r0 - (K - 1) + k, rows), :], axis=0, keepdims=True)
            acc = term if acc is None else acc + term
        dw_ref[pl.ds(k, 1), :] = acc


def _col_spec(S, off):
    return pl.BlockSpec((S, LANE), lambda j: (0, off // LANE + j))


def _sc_fwd(proj, conv_w, l, offs, DS, name):
    S = proj.shape[0]
    K = conv_w.shape[1]
    rows = min(CONV_ROWS, S)

    def body(b_ref, c_ref, u_ref, w_ref, o_ref, pad_ref):
        _zero_pads(pad_ref, S)
        pad_ref[pl.ds(CONV_PAD, S), :] = c_ref[...].astype(F32) * u_ref[...].astype(F32)
        for r0 in range(0, S, rows):
            cv = _conv_fwd_chunk(pad_ref, w_ref, K, r0, rows)
            o_ref[pl.ds(r0, rows), :] = (b_ref[pl.ds(r0, rows), :].astype(F32) * cv).astype(BF16)

    return pl.pallas_call(
        body, name=name, grid=(DS // LANE,),
        in_specs=[_col_spec(S, offs["sc_b"]), _col_spec(S, offs["sc_c"]), _col_spec(S, offs["sc_u"]),
                  pl.BlockSpec((None, K, LANE), lambda j: (l, 0, j))],
        out_specs=pl.BlockSpec((S, LANE), lambda j: (0, j)),
        out_shape=jax.ShapeDtypeStruct((S, DS), BF16),
        scratch_shapes=[pltpu.VMEM((S + 2 * CONV_PAD, LANE), F32)],
        compiler_params=_params("parallel"),
    )(proj, proj, proj, conv_w)


def _sc_bwd(dga, proj, conv_w, l, offs, DS, name):
    S = proj.shape[0]
    K = conv_w.shape[1]
    rows = min(CONV_ROWS, S)

    def body(d_ref, b_ref, c_ref, u_ref, w_ref, db_ref, dc_ref, du_ref, dw_ref, tpad_ref, gpad_ref):
        _zero_pads(tpad_ref, S)
        _zero_pads(gpad_ref, S)
        tpad_ref[pl.ds(CONV_PAD, S), :] = c_ref[...].astype(F32) * u_ref[...].astype(F32)
        for r0 in range(0, S, rows):
            sl = pl.ds(r0, rows)
            cv = _conv_fwd_chunk(tpad_ref, w_ref, K, r0, rows)
            d = d_ref[sl, :]
            db_ref[sl, :] = (d * cv).astype(BF16)
            gpad_ref[pl.ds(CONV_PAD + r0, rows), :] = d * b_ref[sl, :].astype(F32)
        for r0 in range(0, S, rows):
            sl = pl.ds(r0, rows)
            dt = _conv_bwd_chunk(gpad_ref, w_ref, K, r0, rows)
            dc_ref[sl, :] = (dt * u_ref[sl, :].astype(F32)).astype(BF16)
            du_ref[sl, :] = (dt * c_ref[sl, :].astype(F32)).astype(BF16)
        _conv_dw(tpad_ref, gpad_ref, CONV_PAD, K, S, dw_ref)

    blk = pl.BlockSpec((S, LANE), lambda j: (0, j))
    act = jax.ShapeDtypeStruct((S, DS), BF16)
    return pl.pallas_call(
        body, name=name, grid=(DS // LANE,),
        in_specs=[blk, _col_spec(S, offs["sc_b"]), _col_spec(S, offs["sc_c"]), _col_spec(S, offs["sc_u"]),
                  pl.BlockSpec((None, K, LANE), lambda j: (l, 0, j))],
        out_specs=[blk, blk, blk, pl.BlockSpec((K, LANE), lambda j: (0, j))],
        out_shape=[act, act, act, jax.ShapeDtypeStruct((K, DS), F32)],
        scratch_shapes=[pltpu.VMEM((S + 2 * CONV_PAD, LANE), F32), pltpu.VMEM((S + 2 * CONV_PAD, LANE), F32)],
        compiler_params=_params("parallel"),
    )(dga, proj, proj, proj, conv_w)


def _cf_conv_fwd(proj, conv_w, conv_b3, l, offs, DC, name):
    S = proj.shape[0]
    K = conv_w.shape[1]
    rows = min(CONV_ROWS, S)

    def body(a_ref, g_ref, w_ref, bias_ref, o_ref, pad_ref):
        _zero_pads(pad_ref, S)
        pad_ref[pl.ds(CONV_PAD, S), :] = a_ref[...].astype(F32) * _sigmoid(g_ref[...].astype(F32))
        for r0 in range(0, S, rows):
            o_ref[pl.ds(r0, rows), :] = _conv_fwd_chunk(pad_ref, w_ref, K, r0, rows) + bias_ref[...]

    return pl.pallas_call(
        body, name=name, grid=(DC // LANE,),
        in_specs=[_col_spec(S, offs["cf_a"]), _col_spec(S, offs["cf_g"]),
                  pl.BlockSpec((None, K, LANE), lambda j: (l, 0, j)),
                  pl.BlockSpec((None, 1, LANE), lambda j: (l, 0, j))],
        out_specs=pl.BlockSpec((S, LANE), lambda j: (0, j)),
        out_shape=jax.ShapeDtypeStruct((S, DC), F32),
        scratch_shapes=[pltpu.VMEM((S + 2 * CONV_PAD, LANE), F32)],
        compiler_params=_params("parallel"),
    )(proj, proj, conv_w, conv_b3)


def _layer_norm_hat(u):
    mu = jnp.mean(u, axis=-1, keepdims=True)
    xc = u - mu
    rstd = lax.rsqrt(jnp.mean(xc * xc, axis=-1, keepdims=True) + LN_EPS)
    return xc * rstd, rstd


def _cf_norm_fwd(u1, gam3, bet3, l, name):
    S, DC = u1.shape
    tr = _row_tile(S)

    def body(u_ref, g_ref, b_ref, o_ref):
        xhat, _ = _layer_norm_hat(u_ref[...])
        s = xhat * g_ref[...] + b_ref[...]
        o_ref[...] = (s * _sigmoid(s)).astype(BF16)

    row = pl.BlockSpec((tr, DC), lambda i: (i, 0))
    vec = pl.BlockSpec((None, 1, DC), lambda i: (l, 0, 0))
    return pl.pallas_call(
        body, name=name, grid=(S // tr,),
        in_specs=[row, vec, vec], out_specs=row,
        out_shape=jax.ShapeDtypeStruct((S, DC), BF16),
        compiler_params=_params("parallel"),
    )(u1, gam3, bet3)


def _cf_norm_bwd(du2, u1, gam3, bet3, l, name):
    S, DC = u1.shape
    tr = _row_tile(S)

    def body(d_ref, u_ref, g_ref, b_ref, du_ref, dg_ref, db_ref):
        @pl.when(pl.program_id(0) == 0)
        def _():
            dg_ref[...] = jnp.zeros_like(dg_ref)
            db_ref[...] = jnp.zeros_like(db_ref)

        xhat, rstd = _layer_norm_hat(u_ref[...])
        s = xhat * g_ref[...] + b_ref[...]
        sg = _sigmoid(s)
        ds = d_ref[...] * (sg * (1.0 + s * (1.0 - sg)))
        dg_ref[...] += jnp.sum(ds * xhat, axis=0, keepdims=True)
        db_ref[...] += jnp.sum(ds, axis=0, keepdims=True)
        dxh = ds * g_ref[...]
        du_ref[...] = rstd * (dxh - jnp.mean(dxh, axis=-1, keepdims=True)
                              - xhat * jnp.mean(dxh * xhat, axis=-1, keepdims=True))

    row = pl.BlockSpec((tr, DC), lambda i: (i, 0))
    vec = pl.BlockSpec((None, 1, DC), lambda i: (l, 0, 0))
    acc = pl.BlockSpec((1, DC), lambda i: (0, 0))
    return pl.pallas_call(
        body, name=name, grid=(S // tr,),
        in_specs=[row, row, vec, vec], out_specs=[row, acc, acc],
        out_shape=[jax.ShapeDtypeStruct((S, DC), F32), jax.ShapeDtypeStruct((1, DC), F32),
                   jax.ShapeDtypeStruct((1, DC), F32)],
        compiler_params=_params("arbitrary"),
    )(du2, u1, gam3, bet3)


def _cf_conv_bwd(du1, proj, conv_w, l, offs, DC, name):
    S = proj.shape[0]
    K = conv_w.shape[1]
    rows = min(CONV_ROWS, S)

    def body(d_ref, a_ref, g_ref, w_ref, da_ref, dgl_ref, dw_ref, dbias_ref, upad_ref, dpad_ref):
        _zero_pads(upad_ref, S)
        _zero_pads(dpad_ref, S)
        upad_ref[pl.ds(CONV_PAD, S), :] = a_ref[...].astype(F32) * _sigmoid(g_ref[...].astype(F32))
        dpad_ref[pl.ds(CONV_PAD, S), :] = d_ref[...]
        dbias_ref[...] = jnp.sum(d_ref[...], axis=0, keepdims=True)
        for r0 in range(0, S, rows):
            sl = pl.ds(r0, rows)
            du0 = _conv_bwd_chunk(dpad_ref, w_ref, K, r0, rows)
            a = a_ref[sl, :].astype(F32)
            sg = _sigmoid(g_ref[sl, :].astype(F32))
            da_ref[sl, :] = (du0 * sg).astype(BF16)
            dgl_ref[sl, :] = (du0 * a * sg * (1.0 - sg)).astype(BF16)
        _conv_dw(upad_ref, dpad_ref, CONV_PAD, K, S, dw_ref)

    blk = pl.BlockSpec((S, LANE), lambda j: (0, j))
    act = jax.ShapeDtypeStruct((S, DC), BF16)
    return pl.pallas_call(
        body, name=name, grid=(DC // LANE,),
        in_specs=[blk, _col_spec(S, offs["cf_a"]), _col_spec(S, offs["cf_g"]),
                  pl.BlockSpec((None, K, LANE), lambda j: (l, 0, j))],
        out_specs=[blk, blk, pl.BlockSpec((K, LANE), lambda j: (0, j)), pl.BlockSpec((1, LANE), lambda j: (0, j))],
        out_shape=[act, act, jax.ShapeDtypeStruct((K, DC), F32), jax.ShapeDtypeStruct((1, DC), F32)],
        scratch_shapes=[pltpu.VMEM((S + 2 * CONV_PAD, LANE), F32), pltpu.VMEM((S + 2 * CONV_PAD, LANE), F32)],
        compiler_params=_params("parallel"),
    )(du1, proj, proj, conv_w)


def _dot_nt(a, b):
    return lax.dot_general(a, b, (((1,), (1,)), ((), ())), preferred_element_type=F32)


def _dot_nn(a, b):
    return lax.dot_general(a, b, (((1,), (0,)), ((), ())), preferred_element_type=F32)


def _dot_tn(a, b):
    return lax.dot_general(a, b, (((0,), (0,)), ((), ())), preferred_element_type=F32)


def _dot_split(x, u):
    hi = x.astype(BF16)
    lo = (x - hi.astype(F32)).astype(BF16)
    return _dot_nn(hi, u) + _dot_nn(lo, u)


MASKED = -1e30


def _log_fail(z):
    return -(jnp.maximum(z, 0.0) + jnp.log(1.0 + jnp.exp(-jnp.abs(z))))


def _head_group(H, want):
    g = min(want, H)
    while H % g:
        g -= 1
    return g


def _lanes(g):
    return slice(g * HEAD_DIM, (g + 1) * HEAD_DIM)


def _attn_fwd(proj, offs, DA, name):
    S = proj.shape[0]
    H = DA // HEAD_DIM
    G = _head_group(H, ATTN_FWD_HEADS)
    nb = S // QB

    def body(q_ref, k_ref, v_ref, o_ref, tot_ref):
        row = lax.broadcasted_iota(jnp.int32, (QB, QB), 0)
        col = lax.broadcasted_iota(jnp.int32, (QB, QB), 1)
        u_after = (row > col).astype(BF16)

        def rows_of(i, t):
            return pl.ds(pl.multiple_of(jnp.maximum(i - t, 0) * QB, QB), QB)

        def scores(i, t, g, q):
            return _dot_nt(q, k_ref[rows_of(i, t), _lanes(g)]) * (HEAD_DIM ** -0.5)

        def log_terms(i, t, z):
            valid = col < row + jnp.minimum(t, i) * QB
            lf = jnp.where(valid, _log_fail(z), 0.0)
            return jnp.where(valid, lf + z + _dot_split(lf, u_after), MASKED), jnp.sum(lf, axis=1, keepdims=True)

        def q_block(i, _):
            qs = pl.ds(pl.multiple_of(i * QB, QB), QB)
            qg = [q_ref[qs, _lanes(g)] for g in range(G)]

            def step(t, carry):
                out = []
                for g in range(G):
                    acc, c, z, (pre, rs) = carry[g]
                    a = jnp.exp(pre + c)
                    acc = acc + _dot_nn(a.astype(BF16), v_ref[rows_of(i, t - 2), _lanes(g)])
                    out.append((acc, c + rs, scores(i, t, g, qg[g]), log_terms(i, t - 1, z)))
                return tuple(out)

            init = []
            for g in range(G):
                z0 = scores(i, 0, g, qg[g])
                init.append((jnp.zeros((QB, HEAD_DIM), F32), jnp.zeros((QB, 1), F32), scores(i, 1, g, qg[g]),
                             log_terms(i, 0, z0)))
            res = lax.fori_loop(2, i + 3, step, tuple(init))
            for g in range(G):
                o_ref[qs, _lanes(g)] = res[g][0].astype(BF16)
                tot_ref[g, qs, :] = res[g][1]
            return 0

        lax.fori_loop(0, nb, q_block, 0)

    def hs(off):
        return pl.BlockSpec((S, G * HEAD_DIM), lambda h: (0, off // (G * HEAD_DIM) + h))

    return pl.pallas_call(
        body, name=name, grid=(H // G,),
        in_specs=[hs(offs["q"]), hs(offs["k"]), hs(offs["v"])],
        out_specs=[pl.BlockSpec((S, G * HEAD_DIM), lambda h: (0, h)), pl.BlockSpec((G, S, 1), lambda h: (h, 0, 0))],
        out_shape=[jax.ShapeDtypeStruct((S, DA), BF16), jax.ShapeDtypeStruct((H, S, 1), F32)],
        compiler_params=_params("parallel"),
    )(proj, proj, proj)


def _attn_bwd(dout, tot, proj, offs, DA, name):
    S = proj.shape[0]
    H = DA // HEAD_DIM
    G = _head_group(H, ATTN_BWD_HEADS)
    nb = S // QB
    scale = HEAD_DIM ** -0.5

    def body(q_ref, k_ref, v_ref, tot_ref, do_ref, dq_ref, dk_ref, dv_ref, dk_acc, dv_acc, stage_ref):
        row = lax.broadcasted_iota(jnp.int32, (QB, QB), 0)
        col = lax.broadcasted_iota(jnp.int32, (QB, QB), 1)
        u_after = (row > col).astype(BF16)
        u_before = (row < col).astype(BF16)
        dk_acc[...] = jnp.zeros_like(dk_acc)
        dv_acc[...] = jnp.zeros_like(dv_acc)

        def rows_of(i, b):
            return pl.ds(pl.multiple_of(jnp.minimum(b, i) * QB, QB), QB)

        def scores(i, b, g, q, do):
            ks = rows_of(i, b)
            return _dot_nt(q, k_ref[ks, _lanes(g)]) * scale, _dot_nt(do, v_ref[ks, _lanes(g)])

        def log_terms(i, b, z, da):
            valid = col < row + (i - jnp.minimum(b, i)) * QB
            lf = jnp.where(valid, _log_fail(z), 0.0)
            pre = jnp.where(valid, lf + z + _dot_split(lf, u_after), MASKED)
            return (pre, da, jnp.exp(lf), jnp.where(valid, jnp.exp(lf + z), 0.0),
                    jnp.sum(lf, axis=1, keepdims=True))

        def d_log_a(tot_q, seen, terms):
            pre, da, fail, beta, rs = terms
            seen = seen + rs
            a = jnp.exp(pre + (tot_q - seen))
            dlog = a * da
            return seen, (dlog, _dot_split(dlog, u_before), a.astype(BF16), fail, beta)

        def q_block(i, _):
            qs = pl.ds(pl.multiple_of(i * QB, QB), QB)
            qg = [q_ref[qs, _lanes(g)] for g in range(G)]
            dog = [do_ref[qs, _lanes(g)] for g in range(G)]
            totg = [tot_ref[g, qs, :] for g in range(G)]

            def put(g, first, tiles):
                for n, tile in enumerate(tiles):
                    stage_ref[g, first + n] = tile.astype(F32)

            def get(g, first, count):
                return [stage_ref[g, first + n] for n in range(count)]

            def step(t, carry):
                out = []
                for g in range(G):
                    dq, seen, gsum, rs = carry[g]
                    dlog, left, a, fail, beta = get(g, 6, 5)
                    terms = get(g, 2, 4) + [rs]
                    z, da = get(g, 0, 2)
                    ks = rows_of(i, t - 3)
                    dz = (dlog * fail - (gsum + left) * beta) * scale
                    dzb = dz.astype(BF16)
                    dk_acc[ks, _lanes(g)] += _dot_tn(dzb, qg[g])
                    dv_acc[ks, _lanes(g)] += _dot_tn(a.astype(BF16), dog[g])
                    dq = dq + _dot_nn(dzb, k_ref[ks, _lanes(g)])
                    gsum = gsum + jnp.sum(dlog, axis=1, keepdims=True)
                    seen, grads = d_log_a(totg[g], seen, terms)
                    terms = log_terms(i, t - 1, z, da)
                    put(g, 6, grads)
                    put(g, 2, terms[:4])
                    put(g, 0, scores(i, t, g, qg[g], dog[g]))
                    out.append((dq, seen, gsum, terms[4]))
                return tuple(out)

            zero = jnp.zeros((QB, 1), F32)
            init = []
            for g in range(G):
                terms0 = log_terms(i, 0, *scores(i, 0, g, qg[g], dog[g]))
                terms1 = log_terms(i, 1, *scores(i, 1, g, qg[g], dog[g]))
                seen, grads0 = d_log_a(totg[g], zero, terms0)
                put(g, 6, grads0)
                put(g, 2, terms1[:4])
                put(g, 0, scores(i, 2, g, qg[g], dog[g]))
                init.append((jnp.zeros((QB, HEAD_DIM), F32), seen, zero, terms1[4]))
            res = lax.fori_loop(3, i + 4, step, tuple(init))
            for g in range(G):
                dq_ref[qs, _lanes(g)] = res[g][0].astype(BF16)
            return 0

        lax.fori_loop(0, nb, q_block, 0)
        dk_ref[...] = dk_acc[...].astype(BF16)
        dv_ref[...] = dv_acc[...].astype(BF16)

    def hs(off):
        return pl.BlockSpec((S, G * HEAD_DIM), lambda h: (0, off // (G * HEAD_DIM) + h))

    head = pl.BlockSpec((S, G * HEAD_DIM), lambda h: (0, h))
    act = jax.ShapeDtypeStruct((S, DA), BF16)
    return pl.pallas_call(
        body, name=name, grid=(H // G,),
        in_specs=[hs(offs["q"]), hs(offs["k"]), hs(offs["v"]), pl.BlockSpec((G, S, 1), lambda h: (h, 0, 0)), head],
        out_specs=[head, head, head], out_shape=[act, act, act],
        scratch_shapes=[pltpu.VMEM((S, G * HEAD_DIM), F32), pltpu.VMEM((S, G * HEAD_DIM), F32),
                        pltpu.VMEM((G, 11, QB, QB), F32)],
        compiler_params=_params("parallel"),
    )(proj, proj, proj, tot, dout)


def _merge_tiles(S, D, goff):
    tn = LANE
    for t in range(LANE, 513, LANE):
        if D % t == 0 and goff % t == 0:
            tn = t
    return _tile(S, 512, 8), tn


def _merge_fwd(ga, attn, u2, pa, pb, pc, proj, goff, name):
    S = ga.shape[0]
    D = pa.shape[-1]
    tm, tn = _merge_tiles(S, D, goff)

    def body(ga_ref, at_ref, u2_ref, pa_ref, pb_ref, pc_ref, la_ref, lb_ref, lc_ref, ya_ref, yb_ref, yc_ref, m_ref):
        ya = _dot_nn(ga_ref[...], pa_ref[...])
        yb = _dot_nn(at_ref[...], pb_ref[...])
        yc = _dot_nn(u2_ref[...], pc_ref[...])
        ya_ref[...] = ya.astype(BF16)
        yb_ref[...] = yb.astype(BF16)
        yc_ref[...] = yc.astype(BF16)
        m_ref[...] = (_sigmoid(la_ref[...].astype(F32)) * ya + _sigmoid(lb_ref[...].astype(F32)) * yb
                      + _sigmoid(lc_ref[...].astype(F32)) * yc).astype(BF16)

    def lhs(a):
        return pl.BlockSpec((tm, a.shape[1]), lambda i, j: (i, 0))

    def rhs(p):
        return pl.BlockSpec((p.shape[0], tn), lambda i, j: (0, j))

    def gate(r):
        return pl.BlockSpec((tm, tn), lambda i, j: (i, (goff + r * D) // tn + j))

    out = pl.BlockSpec((tm, tn), lambda i, j: (i, j))
    act = jax.ShapeDtypeStruct((S, D), BF16)
    return pl.pallas_call(
        body, name=name, grid=(S // tm, D // tn),
        in_specs=[lhs(ga), lhs(attn), lhs(u2), rhs(pa), rhs(pb), rhs(pc), gate(0), gate(1), gate(2)],
        out_specs=[out, out, out, out], out_shape=[act, act, act, act],
        compiler_params=_params("parallel", "parallel"),
    )(ga, attn, u2, pa, pb, pc, proj, proj, proj)


def _merge_bwd(dm, ya, yb, yc, proj, goff, name):
    S, D = dm.shape
    tm, tn = _merge_tiles(S, D, goff)

    def body(dm_ref, ya_ref, yb_ref, yc_ref, la_ref, lb_ref, lc_ref, *o_refs):
        dmv = dm_ref[...]
        for y_ref, l_ref, dy_ref, dl_ref in zip((ya_ref, yb_ref, yc_ref), (la_ref, lb_ref, lc_ref),
                                                o_refs[:3], o_refs[3:]):
            sg = _sigmoid(l_ref[...].astype(F32))
            dy_ref[...] = (dmv * sg).astype(BF16)
            dl_ref[...] = (dmv * y_ref[...].astype(F32) * sg * (1.0 - sg)).astype(BF16)

    def gate(r):
        return pl.BlockSpec((tm, tn), lambda i, j: (i, (goff + r * D) // tn + j))

    blk = pl.BlockSpec((tm, tn), lambda i, j: (i, j))
    act = jax.ShapeDtypeStruct((S, D), BF16)
    return pl.pallas_call(
        body, name=name, grid=(S // tm, D // tn),
        in_specs=[blk, blk, blk, blk, gate(0), gate(1), gate(2)],
        out_specs=[blk] * 6, out_shape=[act] * 6,
        compiler_params=_params("parallel", "parallel"),
    )(dm, ya, yb, yc, proj, proj, proj)


def _ew_tiles(rows, cols):
    tc = cols if cols <= 4096 else _tile(cols, 2048)
    tr = _tile(rows, max(8, (1 << 19) // tc), 8)
    return tr, tc


def _half_rows_tile(Rh, Cs):
    return _tile(Rh, max(16, (1 << 20) // Cs), 16)


def _place_shard(w, l, kind, place, name):
    _, Rs, Cs = w.shape
    tr = _half_rows_tile(Rs, Cs)

    def body(pr_ref, w_ref, o_ref):
        o_ref[...] = w_ref[...].astype(BF16)

    if kind == "col":
        shape = (Rs, 4 * Cs)
        o_spec = pl.BlockSpec((tr, Cs), lambda i, pr: (i, pr[1]))
    else:
        shape = (4, Rs, Cs)
        o_spec = pl.BlockSpec((None, tr, Cs), lambda i, pr: (pr[1], i, 0))
    out = pl.pallas_call(
        body, name=name,
        grid_spec=pltpu.PrefetchScalarGridSpec(
            num_scalar_prefetch=1, grid=(Rs // tr,),
            in_specs=[pl.BlockSpec((None, tr, Cs), lambda i, pr: (l, i, 0))], out_specs=o_spec),
        out_shape=jax.ShapeDtypeStruct(shape, BF16), compiler_params=_params("parallel"),
    )(place, w)
    return out if kind == "col" else out.reshape(4 * Rs, Cs)


def _pair_sum(g, got, kind, place, name):
    _, Rh, Cs = got.shape
    tr = _half_rows_tile(Rh, Cs)
    if kind == "col":
        gv = g.reshape(2, Rh, 4 * Cs)
        g_spec = pl.BlockSpec((None, tr, Cs), lambda p, i, pr: (pr[0], i, p))
    else:
        gv = g.reshape(4, 2, Rh, Cs)
        g_spec = pl.BlockSpec((None, None, tr, Cs), lambda p, i, pr: (p, pr[0], i, 0))
    blk = pl.BlockSpec((None, tr, Cs), lambda p, i, pr: (p, i, 0))

    def body(pr_ref, g_ref, r_ref, o_ref):
        o_ref[...] = (g_ref[...].astype(F32) + r_ref[...].astype(F32)).astype(BF16)

    return pl.pallas_call(
        body, name=name,
        grid_spec=pltpu.PrefetchScalarGridSpec(num_scalar_prefetch=1, grid=(4, Rh // tr),
                                               in_specs=[g_spec, blk], out_specs=blk),
        out_shape=jax.ShapeDtypeStruct(got.shape, BF16), compiler_params=_params("parallel", "parallel"),
    )(place, gv, got)


def _chip_sum(part, got, place, stack, name):
    n, Rh, Cs = got.shape
    l, L, buf = stack
    tr = _half_rows_tile(Rh, Cs)
    prev = () if buf is None else (buf,)

    def body(pr_ref, p_ref, r_ref, *rest):
        acc = p_ref[...].astype(F32)
        for s in range(n):
            acc = acc + r_ref[s].astype(F32)
        rest[-1][...] = acc

    return pl.pallas_call(
        body, name=name,
        grid_spec=pltpu.PrefetchScalarGridSpec(
            num_scalar_prefetch=1, grid=(Rh // tr,),
            in_specs=[pl.BlockSpec((None, tr, Cs), lambda i, pr: (pr[1], i, 0)),
                      pl.BlockSpec((n, tr, Cs), lambda i, pr: (0, i, 0))] + [ANY] * len(prev),
            out_specs=pl.BlockSpec((None, None, tr, Cs), lambda i, pr: (l, pr[0], i, 0))),
        out_shape=jax.ShapeDtypeStruct((L, 2, Rh, Cs), F32), input_output_aliases={3: 0} if prev else {},
        compiler_params=_params("parallel"),
    )(place, part, got, *prev)


def _adamw(w, g, m, v, name):
    shape = w.shape
    args = [t.reshape(-1, shape[-1]) for t in (w, g, m, v)]
    rows, cols = args[0].shape
    tr, tc = _ew_tiles(rows, cols)
    c1 = 1.0 - ADAM_B1 ** ADAM_STEP
    c2 = 1.0 - ADAM_B2 ** ADAM_STEP

    def body(w_ref, g_ref, m_ref, v_ref, d_ref, nm_ref, nv_ref, go_ref):
        gv = g_ref[...]
        nm = ADAM_B1 * m_ref[...] + (1.0 - ADAM_B1) * gv
        nv = ADAM_B2 * v_ref[...] + (1.0 - ADAM_B2) * (gv * gv)
        nm_ref[...] = nm
        nv_ref[...] = nv
        go_ref[...] = gv
        d_ref[...] = -ADAM_LR * ((nm / c1) / (jnp.sqrt(nv / c2) + ADAM_EPS) + ADAM_WD * w_ref[...])

    blk = pl.BlockSpec((tr, tc), lambda i, j: (i, j))
    shp = jax.ShapeDtypeStruct((rows, cols), F32)
    outs = pl.pallas_call(
        body, name=name, grid=(rows // tr, cols // tc), in_specs=[blk] * 4, out_specs=[blk] * 4,
        out_shape=[shp] * 4, compiler_params=_params("parallel", "parallel"),
    )(*args)
    return [o.reshape(shape) for o in outs]


def _place():
    x, y, c = lax.axis_index("x"), lax.axis_index("y"), lax.axis_index("c")
    chips = [(1 - x, y), (x, 1 - y), (1 - x, 1 - y)]
    return x, y, c, chips


def _al(v, unit):
    return pl.multiple_of(v, unit) if unit % LANE == 0 else v


def _half_of_full(ref, kind, p, half, Rs, Cs):
    Rh = (ref.shape[-2] // 2) if kind == "col" else Rs // 2
    lead = (slice(None),) * (len(ref.shape) - 2)
    if kind == "col":
        return ref.at[lead + (pl.ds(_al(half * Rh, Rh), Rh), pl.ds(_al(p * Cs, Cs), Cs))]
    return ref.at[lead + (pl.ds(_al(p * Rs + half * Rh, Rh), Rh), slice(None))]


HBM = pl.BlockSpec(memory_space=pltpu.HBM)
SEM = pl.BlockSpec(memory_space=pltpu.SEMAPHORE)
EFFECT = pltpu.SideEffectType.DATAFLOW_SIDE_EFFECTING
FIRST_GATHER_ID = 0
GATHER_ID = 2
REDUCE_ID = 1


def _in_hbm(v):
    return pltpu.with_memory_space_constraint(v, pltpu.HBM)


def _ici_handshake(chips, c):
    barrier = pltpu.get_barrier_semaphore()
    for px, py in chips:
        pl.semaphore_signal(barrier, inc=1, device_id=(px, py, c), device_id_type=MESH)
    pl.semaphore_wait(barrier, len(chips))


def _shard_dims(ref, kind):
    R, C = ref.shape[-2:]
    return (R, C // 4) if kind == "col" else (R // 4, C)


def _gather_start(groups, after, collective_id, name):
    bufs = [b for grp in groups for b, _ in grp]
    kinds = [k for grp in groups for _, k in grp]
    m, ng = len(bufs), len(groups)

    def body(*refs):
        ins = refs[:m]
        sems = refs[m + 1:m + 1 + 2 * ng]
        token = refs[-1]
        x, y, c, chips = _place()
        _ici_handshake(chips, c)
        me = 2 * x + y
        at = 0
        for gi, grp in enumerate(groups):
            n = len(grp)
            for j, chip in enumerate(chips):
                for w in range(n):
                    ref, kind = ins[at + w], kinds[at + w]
                    mine = _half_of_full(ref, kind, me, c, *_shard_dims(ref, kind))
                    pltpu.make_async_remote_copy(mine, mine, sems[2 * gi].at[j * n + w], sems[2 * gi + 1].at[j * n + w],
                                                 device_id=(*chip, c), device_id_type=MESH).start()
            at += n
        token[...] = jnp.zeros_like(token)

    sem_shapes = [pltpu.SemaphoreType.DMA((3 * len(grp),)) for grp in groups for _ in range(2)]
    outs = pl.pallas_call(
        body, name=name, in_specs=[HBM] * m + [ANY],
        out_specs=[SEM] * (2 * ng) + [HBM] * m + [pl.BlockSpec(memory_space=pltpu.VMEM)],
        out_shape=sem_shapes + [pltpu.HBM(b.shape, b.dtype) for b in bufs] + [jax.ShapeDtypeStruct((8, LANE), F32)],
        input_output_aliases={i: 2 * ng + i for i in range(m)},
        compiler_params=pltpu.CompilerParams(has_side_effects=EFFECT, collective_id=collective_id),
    )(*[_in_hbm(b) for b in bufs], after)
    res, at = [], 2 * ng
    for gi, grp in enumerate(groups):
        res.append((outs[2 * gi], outs[2 * gi + 1], outs[at:at + len(grp)]))
        at += len(grp)
    return res, outs[-1]


def _gather_wait(bufs, kinds, send_sem, recv_sem, after, name):
    n = len(bufs)

    def body(*refs):
        ins = refs[:n]
        send, recv = refs[n], refs[n + 1]
        x, y, c, chips = _place()
        me = 2 * x + y
        for j, (px, py) in enumerate(chips):
            for w in range(n):
                dims = _shard_dims(ins[w], kinds[w])
                mine = _half_of_full(ins[w], kinds[w], me, c, *dims)
                theirs = _half_of_full(ins[w], kinds[w], 2 * px + py, c, *dims)
                cp = pltpu.make_async_remote_copy(mine, theirs, send.at[j * n + w], recv.at[j * n + w],
                                                  device_id=(px, py, c), device_id_type=MESH)
                cp.wait_send()
                cp.wait_recv()

    return pl.pallas_call(
        body, name=name, in_specs=[HBM] * n + [SEM, SEM, ANY], out_specs=[HBM] * n,
        out_shape=[pltpu.HBM(b.shape, b.dtype) for b in bufs],
        input_output_aliases={i: i for i in range(n)},
        compiler_params=pltpu.CompilerParams(has_side_effects=EFFECT),
    )(*bufs, send_sem, recv_sem, after)


def _ride_forward(bufs, kinds):
    n = len(bufs)

    def copies(ins, outs, send_sem, recv_sem):
        x, y, c, chips = _place()
        made = []
        for j, (px, py) in enumerate(chips):
            for w in range(n):
                got = _half_of_full(outs[w], kinds[w], 2 * px + py, c, *_shard_dims(outs[w], kinds[w]))
                made.append(pltpu.make_async_remote_copy(got, got, send_sem.at[j * n + w], recv_sem.at[j * n + w],
                                                         device_id=(x, y, 1 - c), device_id_type=MESH))
        return made

    return dict(arrays=list(bufs), out_shape=[jax.ShapeDtypeStruct(b.shape, b.dtype) for b in bufs], alias=True,
                n_sems=3 * n, copies=copies)


def _ride_halves(grads, kinds):
    n = len(grads)
    shapes = []
    for g, kind in zip(grads, kinds):
        R, C = g.shape
        shapes.append((4, R // 2, C // 4) if kind == "col" else (4, R // 8, C))

    def copies(ins, outs, send_sem, recv_sem):
        x, y, c, _ = _place()
        made = []
        for w in range(n):
            _, Rh, Cs = shapes[w]
            for p in range(4):
                made.append(pltpu.make_async_remote_copy(
                    _half_of_full(ins[w], kinds[w], p, 1 - c, 2 * Rh, Cs), outs[w].at[p], send_sem.at[4 * w + p],
                    recv_sem.at[4 * w + p], device_id=(x, y, 1 - c), device_id_type=MESH))
        return made

    return dict(arrays=list(grads), out_shape=[jax.ShapeDtypeStruct(s, BF16) for s in shapes], alias=False,
                n_sems=4 * n, copies=copies)


def _exchange_now(ride, name):
    n = len(ride["arrays"])

    def body(*refs):
        made = ride["copies"](refs[:n], refs[n:2 * n], refs[2 * n], refs[2 * n + 1])
        for cp in made:
            cp.start()
        for cp in made:
            cp.wait()

    return pl.pallas_call(
        body, name=name, in_specs=[ANY] * n, out_specs=[ANY] * n, out_shape=ride["out_shape"],
        input_output_aliases={w: w for w in range(n)} if ride["alias"] else {},
        scratch_shapes=[pltpu.SemaphoreType.DMA((ride["n_sems"],))] * 2,
    )(*ride["arrays"])


def _reduce_start(parts, name):
    n = len(parts)
    zones = [lax.empty((3,) + p.shape[1:], p.dtype) for p in parts]

    def body(*refs):
        ins, lands = refs[:n], refs[n:2 * n]
        send, recv = refs[2 * n], refs[2 * n + 1]
        token = refs[-1]
        x, y, c, chips = _place()
        _ici_handshake(chips, c)
        for j, (px, py) in enumerate(chips):
            for w in range(n):
                pltpu.make_async_remote_copy(ins[w].at[2 * px + py], lands[w].at[j], send.at[j * n + w],
                                             recv.at[j * n + w], device_id=(px, py, c), device_id_type=MESH).start()
        token[...] = jnp.zeros_like(token)

    both = list(parts) + zones
    outs = pl.pallas_call(
        body, name=name, in_specs=[HBM] * (2 * n),
        out_specs=[SEM, SEM] + [HBM] * (2 * n) + [pl.BlockSpec(memory_space=pltpu.VMEM)],
        out_shape=([pltpu.SemaphoreType.DMA((3 * n,))] * 2 + [pltpu.HBM(b.shape, b.dtype) for b in both]
                   + [jax.ShapeDtypeStruct((8, LANE), F32)]),
        input_output_aliases={i: 2 + i for i in range(2 * n)},
        compiler_params=pltpu.CompilerParams(has_side_effects=EFFECT, collective_id=REDUCE_ID),
    )(*[_in_hbm(b) for b in both])
    return outs[0], outs[1], outs[2:2 + n], outs[2 + n:2 + 2 * n], outs[-1]


def _reduce_wait(parts, zones, send_sem, recv_sem, after, name):
    n = len(parts)

    def body(*refs):
        ins, lands = refs[:n], refs[n:2 * n]
        send, recv = refs[2 * n], refs[2 * n + 1]
        x, y, c, chips = _place()
        for j, (px, py) in enumerate(chips):
            for w in range(n):
                cp = pltpu.make_async_remote_copy(ins[w].at[2 * px + py], lands[w].at[j], send.at[j * n + w],
                                                  recv.at[j * n + w], device_id=(px, py, c), device_id_type=MESH)
                cp.wait_send()
                cp.wait_recv()

    both = list(parts) + list(zones)
    outs = pl.pallas_call(
        body, name=name, in_specs=[HBM] * (2 * n) + [SEM, SEM, ANY], out_specs=[HBM] * (2 * n),
        out_shape=[pltpu.HBM(b.shape, b.dtype) for b in both],
        input_output_aliases={i: i for i in range(2 * n)},
        compiler_params=pltpu.CompilerParams(has_side_effects=EFFECT),
    )(*both, send_sem, recv_sem, after)
    return outs[:n], outs[n:]


def _share_with_sibling(reduced, name):
    n = len(reduced)

    def body(*refs):
        outs = refs[n:2 * n]
        send_sem, recv_sem = refs[2 * n:]
        x, y, c, _ = _place()
        sib = (x, y, 1 - c)
        copies = []
        for w in range(n):
            mine = outs[w].at[:, c]
            cp = pltpu.make_async_remote_copy(mine, mine, send_sem.at[w], recv_sem.at[w], device_id=sib,
                                              device_id_type=MESH)
            cp.start()
            copies.append(cp)
        for cp in copies:
            cp.wait()

    return pl.pallas_call(
        body, name=name, in_specs=[ANY] * n, out_specs=[ANY] * n,
        out_shape=[jax.ShapeDtypeStruct(r.shape, r.dtype) for r in reduced],
        input_output_aliases={w: w for w in range(n)},
        scratch_shapes=[pltpu.SemaphoreType.DMA((n,))] * 2,
    )(*reduced)


def _all_gather_small(v):
    r = v.shape[0]

    def body(v_ref, o_ref, send_sem, recv_sem):
        x, y, c, _ = _place()
        me = 4 * x + 2 * y + c
        o_ref[me] = v_ref[...]
        copies = []
        for k in range(1, 8):
            peer = (x ^ (k >> 2), y ^ ((k >> 1) & 1), c ^ (k & 1))
            cp = pltpu.make_async_remote_copy(v_ref, o_ref.at[me], send_sem.at[k - 1], recv_sem.at[k - 1],
                                              device_id=peer, device_id_type=MESH)
            cp.start()
            copies.append(cp)
        for cp in copies:
            cp.wait()

    vmem = pl.BlockSpec(memory_space=pltpu.VMEM)
    return pl.pallas_call(
        body, name="all_gather_small", in_specs=[vmem], out_specs=vmem,
        out_shape=jax.ShapeDtypeStruct((8, r, LANE), F32),
        scratch_shapes=[pltpu.SemaphoreType.DMA((7,)), pltpu.SemaphoreType.DMA((7,))],
        compiler_params=pltpu.CompilerParams(vmem_limit_bytes=VMEM_LIMIT),
    )(v)


def _sum_slots(g):
    n, r, _ = g.shape

    def body(g_ref, o_ref):
        acc = g_ref[0]
        for s in range(1, n):
            acc = acc + g_ref[s]
        o_ref[...] = acc

    vmem = pl.BlockSpec(memory_space=pltpu.VMEM)
    return pl.pallas_call(
        body, name="sum_slots", in_specs=[vmem], out_specs=vmem, out_shape=jax.ShapeDtypeStruct((r, LANE), F32),
        compiler_params=pltpu.CompilerParams(vmem_limit_bytes=VMEM_LIMIT),
    )(g)


def _pack(arrays):
    flat = jnp.concatenate([a.reshape(-1) for a in arrays])
    pad = (-flat.shape[0]) % (8 * LANE)
    return jnp.pad(flat, (0, pad)).reshape(-1, LANE)


def _unpack(packed, like):
    flat = packed.reshape(-1)
    out, off = [], 0
    for a in like:
        out.append(flat[off:off + a.size].reshape(a.shape))
        off += a.size
    return out


def _offsets(D):
    DA, DS, DC = D // 2, D // 4, D // 4
    names = ["q", "k", "v", "sc_b", "sc_c", "sc_u", "cf_a", "cf_g", "gate"]
    sizes = [DA, DA, DA, DS, DS, DS, DC, DC, 3 * D]
    offs, o = {}, 0
    for nm, sz in zip(names, sizes):
        offs[nm] = o
        o += sz
    return offs, DA, DS, DC


def _relu2_epilogue(acc):
    r = jnp.maximum(acc, 0.0)
    return acc, r * r


def _drelu2_epilogue(acc, up):
    return (acc * (2.0 * jnp.maximum(up.astype(F32), 0.0)),)


def _local_step(x, target, gains, conv_a_w, conv_c_w, conv_c_b, norm_c_g, norm_c_b, landed, emit_grads):
    S, D = x.shape
    L = gains[0].shape[0]
    offs, DA, DS, DC = _offsets(D)
    goff = offs["gate"]
    g_mix_pre, g_mix_post, g_mlp_pre, g_mlp_post = gains
    cb3, ng3, nb3 = (t.reshape(L, 1, DC) for t in (conv_c_b, norm_c_g, norm_c_b))

    saved = []
    h = _rms_fwd(x, g_mix_pre, 0, "rms_first")
    xin = x
    def whole(l, part, after, carrier=None):
        names, halves, kinds = landed(l, part, after)
        ride = _ride_forward(halves, kinds)
        if carrier is None:
            return list(zip(names, _exchange_now(ride, f"gather_forward_{part}_{l}"))), None
        result, made = carrier(ride)
        return list(zip(names, made)), result

    next_in, _ = whole(0, "in", xin)
    for l in range(L):
        big = dict(next_in)
        proj = _mm(h, big["w_in"], out_dtypes=(BF16,), name=f"fwd_w_in_{l}", tn_cap=512)
        ga = _sc_fwd(proj, conv_a_w, l, offs, DS, f"sc_fwd_{l}")
        attn, attn_tot = _attn_fwd(proj, offs, DA, f"attn_fwd_{l}")
        u1 = _cf_conv_fwd(proj, conv_c_w, cb3, l, offs, DC, f"cf_conv_fwd_{l}")
        u2 = _cf_norm_fwd(u1, ng3, nb3, l, f"cf_norm_fwd_{l}")
        big.update(whole(l, "rest", attn)[0])
        ya, yb, yc, merged = _merge_fwd(ga, attn, u2, big["proj_a"], big["proj_b"], big["proj_c"], proj, goff,
                                        f"merge_fwd_{l}")
        mixed = _mm(merged, big["w_o"], name=f"fwd_w_o_{l}")
        x1, h2 = _post_res_fwd(xin, mixed, g_mix_post, l, g_mlp_pre, l, f"mix_residual_{l}")
        up, act = _mm(h2, big["w_up"], out_dtypes=(BF16, BF16), epilogue=_relu2_epilogue, name=f"fwd_w_up_{l}")
        if l + 1 < L:
            next_in, f = whole(l + 1, "in", act,
                               lambda ride: _mm(act, big["w_down"], name=f"fwd_w_down_{l}", ride=ride))
        else:
            f = _mm(act, big["w_down"], name=f"fwd_w_down_{l}")
        saved.append(dict(big=big, xin=xin, h=h, proj=proj, ga=ga, attn=attn, attn_tot=attn_tot, u1=u1, u2=u2, ya=ya,
                          yb=yb, yc=yc, merged=merged, mixed=mixed, x1=x1, h2=h2, up=up, act=act, f=f))
        if l + 1 < L:
            xin, h = _post_res_fwd(x1, f, g_mlp_post, l, g_mix_pre, l + 1, f"mlp_residual_{l}")
        else:
            dx, loss = _final_fwd_loss(x1, f, g_mlp_post, l, target, "loss_head")

    small = {k: [None] * L for k in ("mix_pre", "mix_post", "mlp_pre", "mlp_post", "conv_a_w", "conv_c_w",
                                       "conv_c_b", "norm_c_g", "norm_c_b")}

    def dw(key, a, b, l, **kw):
        return _mm(a, b, ta=True, out_dtypes=(BF16,), name=f"d{key}_{l}", **kw)

    def halves_of(g):
        names = [k for k in BIG if k in g]
        return names, [BIG_KIND[k] for k in names]

    for l in reversed(range(L)):
        s = saved[l]
        big = s["big"]
        g = {}
        df, small["mlp_post"][l] = _post_bwd(dx, s["f"], g_mlp_post, l, f"mlp_post_bwd_{l}")
        g["w_down"] = dw("w_down", s["act"], df, l)
        dup = _mm(df, big["w_down"], tb=True, out_dtypes=(BF16,), epilogue=_drelu2_epilogue, extras=(s["up"],),
                  name=f"d_up_{l}")
        g["w_up"] = dw("w_up", s["h2"], dup, l)
        names, kinds = halves_of(g)
        dh2, got = _mm(dup, big["w_up"], tb=True, name=f"d_h2_{l}", ride=_ride_halves([g[k] for k in names], kinds))
        dx1, small["mlp_pre"][l] = _pre_bwd(dx, dh2, s["x1"], g_mlp_pre, l, f"mlp_pre_bwd_{l}")
        g_mix_post = g_mix_post + emit_grads(l, "mlp", names, [g[k] for k in names], got)[0, 0]
        g = {}
        dmixed, small["mix_post"][l] = _post_bwd(dx1, s["mixed"], g_mix_post, l, f"mix_post_bwd_{l}")
        g["w_o"] = dw("w_o", s["merged"], dmixed, l)
        dmerged = _mm(dmixed, big["w_o"], tb=True, name=f"d_merged_{l}")
        dya, dyb, dyc, dla, dlb, dlc = _merge_bwd(dmerged, s["ya"], s["yb"], s["yc"], s["proj"], goff,
                                                  f"merge_bwd_{l}")
        g["proj_a"] = dw("proj_a", s["ga"], dya, l)
        g["proj_b"] = dw("proj_b", s["attn"], dyb, l)
        g["proj_c"] = dw("proj_c", s["u2"], dyc, l)
        dga = _mm(dya, big["proj_a"], tb=True, name=f"d_ga_{l}")
        dattn = _mm(dyb, big["proj_b"], tb=True, out_dtypes=(BF16,), name=f"d_attn_{l}")
        du2 = _mm(dyc, big["proj_c"], tb=True, name=f"d_u2_{l}")
        dsb, dsc, dsu, small["conv_a_w"][l] = _sc_bwd(dga, s["proj"], conv_a_w, l, offs, DS, f"sc_bwd_{l}")
        du1, small["norm_c_g"][l], small["norm_c_b"][l] = _cf_norm_bwd(du2, s["u1"], ng3, nb3, l, f"cf_norm_bwd_{l}")
        dca, dcg, small["conv_c_w"][l], small["conv_c_b"][l] = _cf_conv_bwd(du1, s["proj"], conv_c_w, l, offs, DC,
                                                                          f"cf_conv_bwd_{l}")
        dq, dk, dv = _attn_bwd(dattn, s["attn_tot"], s["proj"], offs, DA, f"attn_bwd_{l}")
        dproj = jnp.concatenate([dq, dk, dv, dsb, dsc, dsu, dca, dcg, dla, dlb, dlc], axis=1)
        g["w_in"] = dw("w_in", s["h"], dproj, l, tn_cap=512)
        names, kinds = halves_of(g)
        dh, got = _mm(dproj, big["w_in"], tb=True, name=f"d_h_{l}", tk_cap=3072,
                      ride=_ride_halves([g[k] for k in names], kinds))
        dx, small["mix_pre"][l] = _pre_bwd(dx1, dh, s["xin"], g_mix_pre, l, f"mix_pre_bwd_{l}")
        g_mlp_post = g_mlp_post + emit_grads(l, "mix", names, [g[k] for k in names], got)[0, 0]
    return loss, dx, small


BIG = ("w_in", "proj_a", "proj_b", "proj_c", "w_o", "w_up", "w_down")
BIG_KIND = {"w_in": "col", "proj_a": "col", "proj_b": "col", "proj_c": "col", "w_o": "row", "w_up": "col",
            "w_down": "row"}


def kernel(x, ln_mix_pre, ln_mix_post, ln_mlp_pre, ln_mlp_post, w_in, conv_a_w, proj_a, proj_b, conv_c_w, conv_c_b, norm_c_g, norm_c_b, proj_c, w_o, w_up, w_down, loss_target, m_ln_mix_pre, m_ln_mix_post, m_ln_mlp_pre, m_ln_mlp_post, m_w_in, m_conv_a_w, m_proj_a, m_proj_b, m_conv_c_w, m_conv_c_b, m_norm_c_g, m_norm_c_b, m_proj_c, m_w_o, m_w_up, m_w_down, v_ln_mix_pre, v_ln_mix_post, v_ln_mlp_pre, v_ln_mlp_post, v_w_in, v_conv_a_w, v_proj_a, v_proj_b, v_conv_c_w, v_conv_c_b, v_norm_c_g, v_norm_c_b, v_proj_c, v_w_o, v_w_up, v_w_down):
    weights = dict(ln_mix_pre=ln_mix_pre, ln_mix_post=ln_mix_post, ln_mlp_pre=ln_mlp_pre, ln_mlp_post=ln_mlp_post,
                   w_in=w_in, conv_a_w=conv_a_w, proj_a=proj_a, proj_b=proj_b, conv_c_w=conv_c_w, conv_c_b=conv_c_b,
                   norm_c_g=norm_c_g, norm_c_b=norm_c_b, proj_c=proj_c, w_o=w_o, w_up=w_up, w_down=w_down)
    m_in = dict(ln_mix_pre=m_ln_mix_pre, ln_mix_post=m_ln_mix_post, ln_mlp_pre=m_ln_mlp_pre, ln_mlp_post=m_ln_mlp_post,
                w_in=m_w_in, conv_a_w=m_conv_a_w, proj_a=m_proj_a, proj_b=m_proj_b, conv_c_w=m_conv_c_w,
                conv_c_b=m_conv_c_b, norm_c_g=m_norm_c_g, norm_c_b=m_norm_c_b, proj_c=m_proj_c, w_o=m_w_o,
                w_up=m_w_up, w_down=m_w_down)
    v_in = dict(ln_mix_pre=v_ln_mix_pre, ln_mix_post=v_ln_mix_post, ln_mlp_pre=v_ln_mlp_pre, ln_mlp_post=v_ln_mlp_post,
                w_in=v_w_in, conv_a_w=v_conv_a_w, proj_a=v_proj_a, proj_b=v_proj_b, conv_c_w=v_conv_c_w,
                conv_c_b=v_conv_c_b, norm_c_g=v_norm_c_g, norm_c_b=v_norm_c_b, proj_c=v_proj_c, w_o=v_w_o,
                w_up=v_w_up, w_down=v_w_down)
    order = list(weights)
    L, D = ln_mix_pre.shape
    chip = 2 * lax.axis_index("x") + lax.axis_index("y")
    place = jnp.stack([lax.axis_index("c"), chip]).astype(jnp.int32)

    conv_local = [conv_a_w, conv_c_w]
    slots = _all_gather_small(_pack(conv_local))
    gather_groups = {"in": ("w_in",), "rest": ("proj_a", "proj_b", "proj_c", "w_o", "w_up", "w_down")}
    def placed(layers, where):
        return [[(_place_shard(weights[k], l, BIG_KIND[k], where, f"place_{k}_{l}"), BIG_KIND[k]) for k in names]
                for l in layers for names in gather_groups.values()]

    first, token = _gather_start(placed([0], place), slots, FIRST_GATHER_ID, "gather_start_first")
    later, started_all = _gather_start(placed(range(1, L), place + token[0, 0].astype(jnp.int32)), token, GATHER_ID,
                                       "gather_start")
    started = first + later
    in_flight = {(l, part): started[l * len(gather_groups) + i]
                 for l in range(L) for i, part in enumerate(gather_groups)}

    def landed(l, part, after):
        names = gather_groups[part]
        kinds = [BIG_KIND[k] for k in names]
        send_sem, recv_sem, bufs = in_flight[l, part]
        return names, _gather_wait(bufs, kinds, send_sem, recv_sem, after, f"gather_wait_{part}_{l}"), kinds

    pending = []

    def emit_grads(l, part, names, glist, got):
        kinds = [BIG_KIND[k] for k in names]
        pair = [_pair_sum(gk, r, kd, place, f"pair_sum_{k}_{l}") for k, kd, gk, r in zip(names, kinds, glist, got)]
        send_sem, recv_sem, parts, zones, token = _reduce_start(pair, f"reduce_start_{part}_{l}")
        pending.append((l, part, names, parts, zones, send_sem, recv_sem))
        return token

    per_chip = [_unpack(slots[4 * px + 2 * py], conv_local) for px in range(2) for py in range(2)]
    conv_a_full = jnp.concatenate([pc[0] for pc in per_chip], axis=-1)
    conv_c_full = jnp.concatenate([pc[1] for pc in per_chip], axis=-1)

    gains = [weights[k].reshape(L, 1, D) for k in ("ln_mix_pre", "ln_mix_post", "ln_mlp_pre", "ln_mlp_post")]
    gains[0] = gains[0] + started_all[0, 0]
    loss, dx, small = _local_step(x[0], loss_target[0], gains, conv_a_full, conv_c_full, conv_c_b, norm_c_g,
                                  norm_c_b, landed, emit_grads)

    grads, delta, new_m, new_v = {}, {}, {}, {}

    def finish(part, after):
        reduced = {}
        for l, p, names, parts, zones, send_sem, recv_sem in pending:
            if p == part:
                parts, landed = _reduce_wait(parts, zones, send_sem, recv_sem, after, f"reduce_wait_{p}_{l}")
                for k, mine, theirs in zip(names, parts, landed):
                    reduced[k] = _chip_sum(mine, theirs, place, (l, L, reduced.get(k)), f"chip_sum_{k}_{l}")
        for k, r in zip(reduced, _share_with_sibling(list(reduced.values()), f"reduce_share_{part}")):
            whole = r.reshape(r.shape[0], r.shape[1] * r.shape[2], r.shape[3])
            delta[k], new_m[k], new_v[k], grads[k] = _adamw(weights[k], whole, m_in[k], v_in[k], f"adamw_{k}")

    finish("mlp", dx)
    finish("mix", delta["w_down"])

    small_names = ["ln_mix_pre", "ln_mix_post", "ln_mlp_pre", "ln_mlp_post", "conv_a_w", "conv_c_w", "conv_c_b",
                   "norm_c_g", "norm_c_b"]
    small_key = dict(ln_mix_pre="mix_pre", ln_mix_post="mix_post", ln_mlp_pre="mlp_pre", ln_mlp_post="mlp_post")
    small_local = []
    for k in small_names:
        per_layer = small[small_key.get(k, k)]
        stacked = jnp.stack(per_layer)
        small_local.append(stacked.reshape(L, -1) if stacked.shape[1] == 1 else stacked)
    small_sum = _unpack(_sum_slots(_all_gather_small(_pack(small_local))), small_local)
    for k, g in zip(small_names, small_sum):
        if k in ("conv_a_w", "conv_c_w"):
            width = weights[k].shape[-1]
            g = lax.dynamic_slice_in_dim(g, chip * width, width, axis=2)
        grads[k] = g

    packed = [_pack([t[k] for k in small_names]) for t in (weights, grads, m_in, v_in)]
    like = [weights[k] for k in small_names]
    for dst, res in zip((delta, new_m, new_v), _adamw(*packed, "adamw_small")):
        dst.update(zip(small_names, _unpack(res, like)))

    total = lax.psum(loss[0, 0], ("x", "y", "c"))
    return (total, dx[None], *[grads[k] for k in order], *[delta[k] for k in order],
            *[new_m[k] for k in order], *[new_v[k] for k in order])
```

```python
import functools

import jax
import jax.numpy as jnp
from jax import lax
from jax.experimental import pallas as pl
from jax.experimental.pallas import tpu as pltpu

F32 = jnp.float32
BF16 = jnp.bfloat16
MESH = pl.DeviceIdType.MESH

HEAD_DIM = 128
QB = 128
ATTN_FWD_HEADS = 4
ATTN_BWD_HEADS = 4
RMS_EPS = 1e-6
LN_EPS = 1e-5
ADAM_LR = 0.001
ADAM_B1 = 0.9
ADAM_B2 = 0.999
ADAM_EPS = 1e-08
ADAM_WD = 0.01
ADAM_STEP = 10
LANE = 128
VMEM_LIMIT = 56 * 1024 * 1024
CONV_PAD = 32
CONV_ROWS = 256
ANY = pl.BlockSpec(memory_space=pl.ANY)


def _tile(n, cap, mult=LANE):
    best = None
    t = mult
    while t <= min(n, cap):
        if n % t == 0:
            best = t
        t += mult
    return best if best is not None else n


def _params(*sem):
    return pltpu.CompilerParams(dimension_semantics=sem if sem else None, vmem_limit_bytes=VMEM_LIMIT)


def _sigmoid(x):
    return 1.0 / (1.0 + jnp.exp(-x))


def _mm(a, b, *, name, ta=False, tb=False, out_dtypes=(F32,), epilogue=None, extras=(),
        tm_cap=1024, tn_cap=1024, tk_cap=2048, ride=None, a_at=None, b_at=None):
    if ta:
        K, M = a.shape[-2:]
    else:
        M, K = a.shape[-2:]
    N = b.shape[-2] if tb else b.shape[-1]
    tm, tn, tk = _tile(M, tm_cap), _tile(N, tn_cap), _tile(K, tk_cap)
    gm, gn, nk = M // tm, N // tn, K // tk

    def spec(blk, idx, at):
        return pl.BlockSpec(blk, idx) if at is None else pl.BlockSpec((None,) + blk, lambda i, j, k: (at,) + idx(i, j, k))

    a_spec = spec((tk, tm), lambda i, j, k: (k, i), a_at) if ta else spec((tm, tk), lambda i, j, k: (i, k), a_at)
    b_spec = spec((tn, tk), lambda i, j, k: (j, k), b_at) if tb else spec((tk, tn), lambda i, j, k: (k, j), b_at)
    e_specs = [pl.BlockSpec((tm, tn), lambda i, j, k: (i, j)) for _ in extras]
    dims = (((0 if ta else 1,), (1 if tb else 0,)), ((), ()))
    n_e, n_o = len(extras), len(out_dtypes)
    n_r = len(ride["arrays"]) if ride else 0

    def body(a_ref, b_ref, *rest):
        e_refs, o_refs = rest[:n_e], rest[n_e + n_r:n_e + n_r + n_o]
        step = (pl.program_id(0) * gn + pl.program_id(1)) * nk + pl.program_id(2)

        def riding():
            at = n_e + n_r + n_o
            send_sem, recv_sem = rest[-2], rest[-1]
            return ride["copies"](rest[n_e:n_e + n_r], rest[at:at + n_r], send_sem, recv_sem)

        if ride:
            @pl.when(step == 0)
            def _():
                for cp in riding():
                    cp.start()

        part = lax.dot_general(a_ref[...].astype(BF16), b_ref[...].astype(BF16), dims, preferred_element_type=F32)

        def finish(acc):
            outs = (acc,) if epilogue is None else epilogue(acc, *[e[...] for e in e_refs])
            for o_ref, o in zip(o_refs, outs):
                o_ref[...] = o.astype(o_ref.dtype)

        if nk == 1:
            finish(part)
        else:
            acc_ref = rest[n_e + 2 * n_r + n_o]
            k = pl.program_id(2)

            @pl.when(k == 0)
            def _():
                acc_ref[...] = part

            @pl.when(k > 0)
            def _():
                acc_ref[...] += part

            @pl.when(k == nk - 1)
            def _():
                finish(acc_ref[...])

        if ride:
            @pl.when(step == gm * gn * nk - 1)
            def _():
                for cp in riding():
                    cp.wait()

    scratch = [pltpu.VMEM((tm, tn), F32)] if nk > 1 else []
    if ride:
        scratch += [pltpu.SemaphoreType.DMA((ride["n_sems"],))] * 2
    outs = pl.pallas_call(
        body, name=name, grid=(gm, gn, nk),
        in_specs=[a_spec, b_spec] + e_specs + [ANY] * n_r,
        out_specs=[pl.BlockSpec((tm, tn), lambda i, j, k: (i, j)) for _ in out_dtypes] + [ANY] * n_r,
        out_shape=[jax.ShapeDtypeStruct((M, N), dt) for dt in out_dtypes] + (ride["out_shape"] if ride else []),
        input_output_aliases={2 + n_e + w: n_o + w for w in range(n_r)} if ride and ride["alias"] else {},
        scratch_shapes=scratch,
        compiler_params=_params(*(("arbitrary",) * 3 if ride else ("parallel", "parallel", "arbitrary"))),
    )(a, b, *extras, *(ride["arrays"] if ride else ()))
    main = outs[0] if n_o == 1 else outs[:n_o]
    return (main, outs[n_o:]) if ride else main


def _row_tile(S):
    return _tile(S, 256, 8)


def _gain_spec(D, l):
    return pl.BlockSpec((None, 1, D), lambda i: (l, 0, 0))


def _rms(x, g):
    r = lax.rsqrt(jnp.mean(x * x, axis=-1, keepdims=True) + RMS_EPS)
    return x * r * g


def _rms_fwd(x, g3, l, name):
    S, D = x.shape
    tr = _row_tile(S)

    def body(x_ref, g_ref, h_ref):
        h_ref[...] = _rms(x_ref[...], g_ref[...]).astype(BF16)

    return pl.pallas_call(
        body, name=name, grid=(S // tr,),
        in_specs=[pl.BlockSpec((tr, D), lambda i: (i, 0)), _gain_spec(D, l)],
        out_specs=pl.BlockSpec((tr, D), lambda i: (i, 0)),
        out_shape=jax.ShapeDtypeStruct((S, D), BF16),
        compiler_params=_params("parallel"),
    )(x, g3)


def _post_res_fwd(x_in, f, gpost3, l, gnext3, lnext, name):
    S, D = x_in.shape
    tr = _row_tile(S)

    def body(x_ref, f_ref, gp_ref, gn_ref, xo_ref, h_ref):
        xo = x_ref[...] + _rms(f_ref[...], gp_ref[...])
        xo_ref[...] = xo
        h_ref[...] = _rms(xo, gn_ref[...]).astype(BF16)

    row = pl.BlockSpec((tr, D), lambda i: (i, 0))
    return pl.pallas_call(
        body, name=name, grid=(S // tr,),
        in_specs=[row, row, _gain_spec(D, l), _gain_spec(D, lnext)],
        out_specs=[row, row],
        out_shape=[jax.ShapeDtypeStruct((S, D), F32), jax.ShapeDtypeStruct((S, D), BF16)],
        compiler_params=_params("parallel"),
    )(x_in, f, gpost3, gnext3)


def _final_fwd_loss(x_in, f, gpost3, l, target, name):
    S, D = x_in.shape
    tr = _row_tile(S)

    def body(x_ref, f_ref, gp_ref, t_ref, dx_ref, loss_ref):
        @pl.when(pl.program_id(0) == 0)
        def _():
            loss_ref[...] = jnp.zeros_like(loss_ref)

        err = x_ref[...] + _rms(f_ref[...], gp_ref[...]) - t_ref[...]
        dx_ref[...] = err * (1.0 / D)
        loss_ref[...] += 0.5 * jnp.sum(jnp.mean(err * err, axis=-1, keepdims=True))

    row = pl.BlockSpec((tr, D), lambda i: (i, 0))
    return pl.pallas_call(
        body, name=name, grid=(S // tr,),
        in_specs=[row, row, _gain_spec(D, l), row],
        out_specs=[row, pl.BlockSpec((8, LANE), lambda i: (0, 0))],
        out_shape=[jax.ShapeDtypeStruct((S, D), F32), jax.ShapeDtypeStruct((8, LANE), F32)],
        compiler_params=_params("arbitrary"),
    )(x_in, f, gpost3, target)


def _rms_bwd_rows(dy, x, g):
    r = lax.rsqrt(jnp.mean(x * x, axis=-1, keepdims=True) + RMS_EPS)
    t = dy * g
    dx = r * t - x * (r * r * r) * jnp.mean(t * x, axis=-1, keepdims=True)
    return dx, dy * x * r


def _post_bwd(dxo, f, gpost3, l, name):
    S, D = f.shape
    tr = _row_tile(S)

    def body(d_ref, f_ref, g_ref, df_ref, dg_ref):
        @pl.when(pl.program_id(0) == 0)
        def _():
            dg_ref[...] = jnp.zeros_like(dg_ref)

        df, dg = _rms_bwd_rows(d_ref[...], f_ref[...], g_ref[...])
        df_ref[...] = df.astype(BF16)
        dg_ref[...] += jnp.sum(dg, axis=0, keepdims=True)

    row = pl.BlockSpec((tr, D), lambda i: (i, 0))
    return pl.pallas_call(
        body, name=name, grid=(S // tr,),
        in_specs=[row, row, _gain_spec(D, l)],
        out_specs=[row, pl.BlockSpec((1, D), lambda i: (0, 0))],
        out_shape=[jax.ShapeDtypeStruct((S, D), BF16), jax.ShapeDtypeStruct((1, D), F32)],
        compiler_params=_params("arbitrary"),
    )(dxo, f, gpost3)


def _pre_bwd(dxo, dh, x_in, gpre3, l, name):
    S, D = x_in.shape
    tr = _row_tile(S)

    def body(d_ref, dh_ref, x_ref, g_ref, dx_ref, dg_ref):
        @pl.when(pl.program_id(0) == 0)
        def _():
            dg_ref[...] = jnp.zeros_like(dg_ref)

        dx, dg = _rms_bwd_rows(dh_ref[...], x_ref[...], g_ref[...])
        dx_ref[...] = d_ref[...] + dx
        dg_ref[...] += jnp.sum(dg, axis=0, keepdims=True)

    row = pl.BlockSpec((tr, D), lambda i: (i, 0))
    return pl.pallas_call(
        body, name=name, grid=(S // tr,),
        in_specs=[row, row, row, _gain_spec(D, l)],
        out_specs=[row, pl.BlockSpec((1, D), lambda i: (0, 0))],
        out_shape=[jax.ShapeDtypeStruct((S, D), F32), jax.ShapeDtypeStruct((1, D), F32)],
        compiler_params=_params("arbitrary"),
    )(dxo, dh, x_in, gpre3)


def _zero_pads(pad_ref, S):
    z = jnp.zeros((CONV_PAD, pad_ref.shape[1]), F32)
    pad_ref[pl.ds(0, CONV_PAD), :] = z
    pad_ref[pl.ds(CONV_PAD + S, CONV_PAD), :] = z


def _conv_fwd_chunk(pad_ref, w_ref, K, r0, rows):
    acc = None
    for k in range(K):
        term = w_ref[pl.ds(k, 1), :] * pad_ref[pl.ds(CONV_PAD + r0 - (K - 1) + k, rows), :]
        acc = term if acc is None else acc + term
    return acc


def _conv_bwd_chunk(pad_ref, w_ref, K, r0, rows):
    acc = None
    for k in range(K):
        term = w_ref[pl.ds(k, 1), :] * pad_ref[pl.ds(CONV_PAD + r0 + (K - 1) - k, rows), :]
        acc = term if acc is None else acc + term
    return acc


def _conv_dw(upad_ref, dy_ref_or_pad, dy_off, K, S, dw_ref):
    rows = min(CONV_ROWS, S)
    for k in range(K):
        acc = None
        for r0 in range(0, S, rows):
            term = jnp.sum(dy_ref_or_pad[pl.ds(dy_off + r0, rows), :]
                           * upad_ref[pl.ds(CONV_PAD + r0 - (K - 1) + k, rows), :], axis=0, keepdims=True)
            acc = term if acc is None else acc + term
        dw_ref[pl.ds(k, 1), :] = acc


def _col_spec(S, off):
    return pl.BlockSpec((S, LANE), lambda j: (0, off // LANE + j))


def _sc_fwd(proj, conv_w, l, offs, DS, name):
    S = proj.shape[0]
    K = conv_w.shape[1]
    rows = min(CONV_ROWS, S)

    def body(b_ref, c_ref, u_ref, w_ref, o_ref, pad_ref):
        _zero_pads(pad_ref, S)
        pad_ref[pl.ds(CONV_PAD, S), :] = c_ref[...].astype(F32) * u_ref[...].astype(F32)
        for r0 in range(0, S, rows):
            cv = _conv_fwd_chunk(pad_ref, w_ref, K, r0, rows)
            o_ref[pl.ds(r0, rows), :] = (b_ref[pl.ds(r0, rows), :].astype(F32) * cv).astype(BF16)

    return pl.pallas_call(
        body, name=name, grid=(DS // LANE,),
        in_specs=[_col_spec(S, offs["sc_b"]), _col_spec(S, offs["sc_c"]), _col_spec(S, offs["sc_u"]),
                  pl.BlockSpec((None, K, LANE), lambda j: (l, 0, j))],
        out_specs=pl.BlockSpec((S, LANE), lambda j: (0, j)),
        out_shape=jax.ShapeDtypeStruct((S, DS), BF16),
        scratch_shapes=[pltpu.VMEM((S + 2 * CONV_PAD, LANE), F32)],
        compiler_params=_params("parallel"),
    )(proj, proj, proj, conv_w)


def _sc_bwd(dga, proj, conv_w, l, offs, DS, name):
    S = proj.shape[0]
    K = conv_w.shape[1]
    rows = min(CONV_ROWS, S)

    def body(d_ref, b_ref, c_ref, u_ref, w_ref, db_ref, dc_ref, du_ref, dw_ref, tpad_ref, gpad_ref):
        _zero_pads(tpad_ref, S)
        _zero_pads(gpad_ref, S)
        tpad_ref[pl.ds(CONV_PAD, S), :] = c_ref[...].astype(F32) * u_ref[...].astype(F32)
        for r0 in range(0, S, rows):
            sl = pl.ds(r0, rows)
            cv = _conv_fwd_chunk(tpad_ref, w_ref, K, r0, rows)
            d = d_ref[sl, :]
            db_ref[sl, :] = (d * cv).astype(BF16)
            gpad_ref[pl.ds(CONV_PAD + r0, rows), :] = d * b_ref[sl, :].astype(F32)
        for r0 in range(0, S, rows):
            sl = pl.ds(r0, rows)
            dt = _conv_bwd_chunk(gpad_ref, w_ref, K, r0, rows)
            dc_ref[sl, :] = (dt * u_ref[sl, :].astype(F32)).astype(BF16)
            du_ref[sl, :] = (dt * c_ref[sl, :].astype(F32)).astype(BF16)
        _conv_dw(tpad_ref, gpad_ref, CONV_PAD, K, S, dw_ref)

    blk = pl.BlockSpec((S, LANE), lambda j: (0, j))
    act = jax.ShapeDtypeStruct((S, DS), BF16)
    return pl.pallas_call(
        body, name=name, grid=(DS // LANE,),
        in_specs=[blk, _col_spec(S, offs["sc_b"]), _col_spec(S, offs["sc_c"]), _col_spec(S, offs["sc_u"]),
                  pl.BlockSpec((None, K, LANE), lambda j: (l, 0, j))],
        out_specs=[blk, blk, blk, pl.BlockSpec((K, LANE), lambda j: (0, j))],
        out_shape=[act, act, act, jax.ShapeDtypeStruct((K, DS), F32)],
        scratch_shapes=[pltpu.VMEM((S + 2 * CONV_PAD, LANE), F32), pltpu.VMEM((S + 2 * CONV_PAD, LANE), F32)],
        compiler_params=_params("parallel"),
    )(dga, proj, proj, proj, conv_w)


def _cf_conv_fwd(proj, conv_w, conv_b3, l, offs, DC, name):
    S = proj.shape[0]
    K = conv_w.shape[1]
    rows = min(CONV_ROWS, S)

    def body(a_ref, g_ref, w_ref, bias_ref, o_ref, pad_ref):
        _zero_pads(pad_ref, S)
        pad_ref[pl.ds(CONV_PAD, S), :] = a_ref[...].astype(F32) * _sigmoid(g_ref[...].astype(F32))
        for r0 in range(0, S, rows):
            o_ref[pl.ds(r0, rows), :] = _conv_fwd_chunk(pad_ref, w_ref, K, r0, rows) + bias_ref[...]

    return pl.pallas_call(
        body, name=name, grid=(DC // LANE,),
        in_specs=[_col_spec(S, offs["cf_a"]), _col_spec(S, offs["cf_g"]),
                  pl.BlockSpec((None, K, LANE), lambda j: (l, 0, j)),
                  pl.BlockSpec((None, 1, LANE), lambda j: (l, 0, j))],
        out_specs=pl.BlockSpec((S, LANE), lambda j: (0, j)),
        out_shape=jax.ShapeDtypeStruct((S, DC), F32),
        scratch_shapes=[pltpu.VMEM((S + 2 * CONV_PAD, LANE), F32)],
        compiler_params=_params("parallel"),
    )(proj, proj, conv_w, conv_b3)


def _layer_norm_hat(u):
    mu = jnp.mean(u, axis=-1, keepdims=True)
    xc = u - mu
    rstd = lax.rsqrt(jnp.mean(xc * xc, axis=-1, keepdims=True) + LN_EPS)
    return xc * rstd, rstd


def _cf_norm_fwd(u1, gam3, bet3, l, name):
    S, DC = u1.shape
    tr = _row_tile(S)

    def body(u_ref, g_ref, b_ref, o_ref):
        xhat, _ = _layer_norm_hat(u_ref[...])
        s = xhat * g_ref[...] + b_ref[...]
        o_ref[...] = (s * _sigmoid(s)).astype(BF16)

    row = pl.BlockSpec((tr, DC), lambda i: (i, 0))
    vec = pl.BlockSpec((None, 1, DC), lambda i: (l, 0, 0))
    return pl.pallas_call(
        body, name=name, grid=(S // tr,),
        in_specs=[row, vec, vec], out_specs=row,
        out_shape=jax.ShapeDtypeStruct((S, DC), BF16),
        compiler_params=_params("parallel"),
    )(u1, gam3, bet3)


def _cf_norm_bwd(du2, u1, gam3, bet3, l, name):
    S, DC = u1.shape
    tr = _row_tile(S)

    def body(d_ref, u_ref, g_ref, b_ref, du_ref, dg_ref, db_ref):
        @pl.when(pl.program_id(0) == 0)
        def _():
            dg_ref[...] = jnp.zeros_like(dg_ref)
            db_ref[...] = jnp.zeros_like(db_ref)

        xhat, rstd = _layer_norm_hat(u_ref[...])
        s = xhat * g_ref[...] + b_ref[...]
        sg = _sigmoid(s)
        ds = d_ref[...] * (sg * (1.0 + s * (1.0 - sg)))
        dg_ref[...] += jnp.sum(ds * xhat, axis=0, keepdims=True)
        db_ref[...] += jnp.sum(ds, axis=0, keepdims=True)
        dxh = ds * g_ref[...]
        du_ref[...] = rstd * (dxh - jnp.mean(dxh, axis=-1, keepdims=True)
                              - xhat * jnp.mean(dxh * xhat, axis=-1, keepdims=True))

    row = pl.BlockSpec((tr, DC), lambda i: (i, 0))
    vec = pl.BlockSpec((None, 1, DC), lambda i: (l, 0, 0))
    acc = pl.BlockSpec((1, DC), lambda i: (0, 0))
    return pl.pallas_call(
        body, name=name, grid=(S // tr,),
        in_specs=[row, row, vec, vec], out_specs=[row, acc, acc],
        out_shape=[jax.ShapeDtypeStruct((S, DC), F32), jax.ShapeDtypeStruct((1, DC), F32),
                   jax.ShapeDtypeStruct((1, DC), F32)],
        compiler_params=_params("arbitrary"),
    )(du2, u1, gam3, bet3)


def _cf_conv_bwd(du1, proj, conv_w, l, offs, DC, name):
    S = proj.shape[0]
    K = conv_w.shape[1]
    rows = min(CONV_ROWS, S)

    def body(d_ref, a_ref, g_ref, w_ref, da_ref, dgl_ref, dw_ref, dbias_ref, upad_ref, dpad_ref):
        _zero_pads(upad_ref, S)
        _zero_pads(dpad_ref, S)
        upad_ref[pl.ds(CONV_PAD, S), :] = a_ref[...].astype(F32) * _sigmoid(g_ref[...].astype(F32))
        dpad_ref[pl.ds(CONV_PAD, S), :] = d_ref[...]
        dbias_ref[...] = jnp.sum(d_ref[...], axis=0, keepdims=True)
        for r0 in range(0, S, rows):
            sl = pl.ds(r0, rows)
            du0 = _conv_bwd_chunk(dpad_ref, w_ref, K, r0, rows)
            a = a_ref[sl, :].astype(F32)
            sg = _sigmoid(g_ref[sl, :].astype(F32))
            da_ref[sl, :] = (du0 * sg).astype(BF16)
            dgl_ref[sl, :] = (du0 * a * sg * (1.0 - sg)).astype(BF16)
        _conv_dw(upad_ref, dpad_ref, CONV_PAD, K, S, dw_ref)

    blk = pl.BlockSpec((S, LANE), lambda j: (0, j))
    act = jax.ShapeDtypeStruct((S, DC), BF16)
    return pl.pallas_call(
        body, name=name, grid=(DC // LANE,),
        in_specs=[blk, _col_spec(S, offs["cf_a"]), _col_spec(S, offs["cf_g"]),
                  pl.BlockSpec((None, K, LANE), lambda j: (l, 0, j))],
        out_specs=[blk, blk, pl.BlockSpec((K, LANE), lambda j: (0, j)), pl.BlockSpec((1, LANE), lambda j: (0, j))],
        out_shape=[act, act, jax.ShapeDtypeStruct((K, DC), F32), jax.ShapeDtypeStruct((1, DC), F32)],
        scratch_shapes=[pltpu.VMEM((S + 2 * CONV_PAD, LANE), F32), pltpu.VMEM((S + 2 * CONV_PAD, LANE), F32)],
        compiler_params=_params("parallel"),
    )(du1, proj, proj, conv_w)


def _dot_nt(a, b):
    return lax.dot_general(a, b, (((1,), (1,)), ((), ())), preferred_element_type=F32)


def _dot_nn(a, b):
    return lax.dot_general(a, b, (((1,), (0,)), ((), ())), preferred_element_type=F32)


def _dot_tn(a, b):
    return lax.dot_general(a, b, (((0,), (0,)), ((), ())), preferred_element_type=F32)


def _dot_split(x, u):
    hi = x.astype(BF16)
    lo = (x - hi.astype(F32)).astype(BF16)
    return _dot_nn(hi, u) + _dot_nn(lo, u)


MASKED = -1e30


def _log_fail(z):
    return -(jnp.maximum(z, 0.0) + jnp.log(1.0 + jnp.exp(-jnp.abs(z))))


def _head_group(H, want):
    g = min(want, H)
    while H % g:
        g -= 1
    return g


def _lanes(g):
    return slice(g * HEAD_DIM, (g + 1) * HEAD_DIM)


def _attn_fwd(proj, offs, DA, name):
    S = proj.shape[0]
    H = DA // HEAD_DIM
    G = _head_group(H, ATTN_FWD_HEADS)
    nb = S // QB

    def body(q_ref, k_ref, v_ref, o_ref, tot_ref):
        row = lax.broadcasted_iota(jnp.int32, (QB, QB), 0)
        col = lax.broadcasted_iota(jnp.int32, (QB, QB), 1)
        u_after = (row > col).astype(BF16)

        def rows_of(i, t):
            return pl.ds(pl.multiple_of(jnp.maximum(i - t, 0) * QB, QB), QB)

        def scores(i, t, g, q):
            return _dot_nt(q, k_ref[rows_of(i, t), _lanes(g)]) * (HEAD_DIM ** -0.5)

        def log_terms(i, t, z):
            valid = col < row + jnp.minimum(t, i) * QB
            lf = jnp.where(valid, _log_fail(z), 0.0)
            return jnp.where(valid, lf + z + _dot_split(lf, u_after), MASKED), jnp.sum(lf, axis=1, keepdims=True)

        def q_block(i, _):
            qs = pl.ds(pl.multiple_of(i * QB, QB), QB)
            qg = [q_ref[qs, _lanes(g)] for g in range(G)]

            def step(t, carry):
                out = []
                for g in range(G):
                    acc, c, z, (pre, rs) = carry[g]
                    a = jnp.exp(pre + c)
                    acc = acc + _dot_nn(a.astype(BF16), v_ref[rows_of(i, t - 2), _lanes(g)])
                    out.append((acc, c + rs, scores(i, t, g, qg[g]), log_terms(i, t - 1, z)))
                return tuple(out)

            init = []
            for g in range(G):
                z0 = scores(i, 0, g, qg[g])
                init.append((jnp.zeros((QB, HEAD_DIM), F32), jnp.zeros((QB, 1), F32), scores(i, 1, g, qg[g]),
                             log_terms(i, 0, z0)))
            res = lax.fori_loop(2, i + 3, step, tuple(init))
            for g in range(G):
                o_ref[qs, _lanes(g)] = res[g][0].astype(BF16)
                tot_ref[g, qs, :] = res[g][1]
            return 0

        lax.fori_loop(0, nb, q_block, 0)

    def hs(off):
        return pl.BlockSpec((S, G * HEAD_DIM), lambda h: (0, off // (G * HEAD_DIM) + h))

    return pl.pallas_call(
        body, name=name, grid=(H // G,),
        in_specs=[hs(offs["q"]), hs(offs["k"]), hs(offs["v"])],
        out_specs=[pl.BlockSpec((S, G * HEAD_DIM), lambda h: (0, h)), pl.BlockSpec((G, S, 1), lambda h: (h, 0, 0))],
        out_shape=[jax.ShapeDtypeStruct((S, DA), BF16), jax.ShapeDtypeStruct((H, S, 1), F32)],
        compiler_params=_params("parallel"),
    )(proj, proj, proj)


def _put_columns(into, pieces, first_col, name):
    S, w = pieces[0].shape
    n = len(pieces)

    def body(*refs):
        out_ref = refs[n + 1]
        for r in range(n):
            @pl.when(pl.program_id(0) == r)
            def _(r=r):
                out_ref[...] = refs[r][...]

    whole = pl.BlockSpec((S, w), lambda r: (0, 0))
    return pl.pallas_call(
        body, name=name, grid=(n,), in_specs=[whole] * n + [ANY],
        out_specs=pl.BlockSpec((S, w), lambda r: (0, first_col // w + r)),
        out_shape=jax.ShapeDtypeStruct(into.shape, into.dtype), input_output_aliases={n: 0},
        compiler_params=_params("arbitrary"),
    )(*pieces, into)


def _attn_bwd(dout, tot, proj, offs, DA, into, name):
    S = proj.shape[0]
    H = DA // HEAD_DIM
    G = _head_group(H, ATTN_BWD_HEADS)
    nb = S // QB
    scale = HEAD_DIM ** -0.5

    def body(q_ref, k_ref, v_ref, tot_ref, do_ref, into_ref, out_ref, dk_acc, dv_acc, stage_ref, keep_ref):
        part = pl.program_id(1)

        @pl.when(part == 0)
        def _():
            work(q_ref, k_ref, v_ref, tot_ref, do_ref, out_ref, keep_ref.at[0], keep_ref.at[1], dk_acc, dv_acc, stage_ref)

        @pl.when(part > 0)
        def _():
            out_ref[...] = keep_ref[part - 1]

    def work(q_ref, k_ref, v_ref, tot_ref, do_ref, dq_ref, dk_ref, dv_ref, dk_acc, dv_acc, stage_ref):
        row = lax.broadcasted_iota(jnp.int32, (QB, QB), 0)
        col = lax.broadcasted_iota(jnp.int32, (QB, QB), 1)
        u_after = (row > col).astype(BF16)
        u_before = (row < col).astype(BF16)
        dk_acc[...] = jnp.zeros_like(dk_acc)
        dv_acc[...] = jnp.zeros_like(dv_acc)

        def rows_of(i, b):
            return pl.ds(pl.multiple_of(jnp.minimum(b, i) * QB, QB), QB)

        def scores(i, b, g, q, do):
            ks = rows_of(i, b)
            return _dot_nt(q, k_ref[ks, _lanes(g)]) * scale, _dot_nt(do, v_ref[ks, _lanes(g)])

        def log_terms(i, b, z, da):
            valid = col < row + (i - jnp.minimum(b, i)) * QB
            lf = jnp.where(valid, _log_fail(z), 0.0)
            pre = jnp.where(valid, lf + z + _dot_split(lf, u_after), MASKED)
            return (pre, da, jnp.exp(lf), jnp.where(valid, jnp.exp(lf + z), 0.0),
                    jnp.sum(lf, axis=1, keepdims=True))

        def d_log_a(tot_q, seen, terms):
            pre, da, fail, beta, rs = terms
            seen = seen + rs
            a = jnp.exp(pre + (tot_q - seen))
            dlog = a * da
            return seen, (dlog, _dot_split(dlog, u_before), a.astype(BF16), fail, beta)

        def q_block(i, _):
            qs = pl.ds(pl.multiple_of(i * QB, QB), QB)
            qg = [q_ref[qs, _lanes(g)] for g in range(G)]
            dog = [do_ref[qs, _lanes(g)] for g in range(G)]
            totg = [tot_ref[g, qs, :] for g in range(G)]

            def put(g, first, tiles):
                for n, tile in enumerate(tiles):
                    stage_ref[g, first + n] = tile.astype(F32)

            def get(g, first, count):
                return [stage_ref[g, first + n] for n in range(count)]

            def step(t, carry):
                out = []
                for g in range(G):
                    dq, seen, gsum, rs = carry[g]
                    dlog, left, a, fail, beta = get(g, 6, 5)
                    terms = get(g, 2, 4) + [rs]
                    z, da = get(g, 0, 2)
                    ks = rows_of(i, t - 3)
                    dz = (dlog * fail - (gsum + left) * beta) * scale
                    dzb = dz.astype(BF16)
                    dk_acc[ks, _lanes(g)] += _dot_tn(dzb, qg[g])
                    dv_acc[ks, _lanes(g)] += _dot_tn(a.astype(BF16), dog[g])
                    dq = dq + _dot_nn(dzb, k_ref[ks, _lanes(g)])
                    gsum = gsum + jnp.sum(dlog, axis=1, keepdims=True)
                    seen, grads = d_log_a(totg[g], seen, terms)
                    terms = log_terms(i, t - 1, z, da)
                    put(g, 6, grads)
                    put(g, 2, terms[:4])
                    put(g, 0, scores(i, t, g, qg[g], dog[g]))
                    out.append((dq, seen, gsum, terms[4]))
                return tuple(out)

            zero = jnp.zeros((QB, 1), F32)
            init = []
            for g in range(G):
                terms0 = log_terms(i, 0, *scores(i, 0, g, qg[g], dog[g]))
                terms1 = log_terms(i, 1, *scores(i, 1, g, qg[g], dog[g]))
                seen, grads0 = d_log_a(totg[g], zero, terms0)
                put(g, 6, grads0)
                put(g, 2, terms1[:4])
                put(g, 0, scores(i, 2, g, qg[g], dog[g]))
                init.append((jnp.zeros((QB, HEAD_DIM), F32), seen, zero, terms1[4]))
            res = lax.fori_loop(3, i + 4, step, tuple(init))
            for g in range(G):
                dq_ref[qs, _lanes(g)] = res[g][0].astype(BF16)
            return 0

        lax.fori_loop(0, nb, q_block, 0)
        dk_ref[...] = dk_acc[...].astype(BF16)
        dv_ref[...] = dv_acc[...].astype(BF16)

    wide = G * HEAD_DIM

    def hs(off):
        return pl.BlockSpec((S, wide), lambda h, part: (0, off // wide + h))

    return pl.pallas_call(
        body, name=name, grid=(H // G, 3),
        in_specs=[hs(offs["q"]), hs(offs["k"]), hs(offs["v"]), pl.BlockSpec((G, S, 1), lambda h, part: (h, 0, 0)),
                  pl.BlockSpec((S, wide), lambda h, part: (0, h)), ANY],
        out_specs=pl.BlockSpec((S, wide), lambda h, part: (0, offs["q"] // wide + part * (DA // wide) + h)),
        out_shape=jax.ShapeDtypeStruct(into.shape, into.dtype), input_output_aliases={5: 0},
        scratch_shapes=[pltpu.VMEM((S, wide), F32), pltpu.VMEM((S, wide), F32), pltpu.VMEM((G, 11, QB, QB), F32),
                        pltpu.VMEM((2, S, wide), BF16)],
        compiler_params=_params("parallel", "arbitrary"),
    )(proj, proj, proj, tot, dout, into)


def _merge_tiles(S, D, goff):
    tn = LANE
    for t in range(LANE, 513, LANE):
        if D % t == 0 and goff % t == 0:
            tn = t
    return _tile(S, 512, 8), tn


def _merge_fwd(ga, attn, u2, pa, pb, pc, proj, goff, name):
    S = ga.shape[0]
    D = pa.shape[-1]
    tm, tn = _merge_tiles(S, D, goff)

    def body(ga_ref, at_ref, u2_ref, pa_ref, pb_ref, pc_ref, la_ref, lb_ref, lc_ref, y_ref, m_ref):
        ya = _dot_nn(ga_ref[...], pa_ref[...])
        yb = _dot_nn(at_ref[...], pb_ref[...])
        yc = _dot_nn(u2_ref[...], pc_ref[...])
        y_ref[0] = ya.astype(BF16)
        y_ref[1] = yb.astype(BF16)
        y_ref[2] = yc.astype(BF16)
        m_ref[...] = (_sigmoid(la_ref[...].astype(F32)) * ya + _sigmoid(lb_ref[...].astype(F32)) * yb
                      + _sigmoid(lc_ref[...].astype(F32)) * yc).astype(BF16)

    def lhs(a):
        return pl.BlockSpec((tm, a.shape[1]), lambda i, j: (i, 0))

    def rhs(p):
        return pl.BlockSpec((p.shape[0], tn), lambda i, j: (0, j))

    def gate(r):
        return pl.BlockSpec((tm, tn), lambda i, j: (i, (goff + r * D) // tn + j))

    return pl.pallas_call(
        body, name=name, grid=(S // tm, D // tn),
        in_specs=[lhs(ga), lhs(attn), lhs(u2), rhs(pa), rhs(pb), rhs(pc), gate(0), gate(1), gate(2)],
        out_specs=[pl.BlockSpec((3, tm, tn), lambda i, j: (0, i, j)), pl.BlockSpec((tm, tn), lambda i, j: (i, j))],
        out_shape=[jax.ShapeDtypeStruct((3, S, D), BF16), jax.ShapeDtypeStruct((S, D), BF16)],
        compiler_params=_params("parallel", "parallel"),
    )(ga, attn, u2, pa, pb, pc, proj, proj, proj)


def _merge_bwd(dm, y3, proj, goff, name):
    S, D = dm.shape
    tm, tn = _merge_tiles(S, D, goff)

    def body(dm_ref, y_ref, l_ref, dy_ref, dl_ref):
        dmv = dm_ref[...]
        sg = _sigmoid(l_ref[...].astype(F32))
        dy_ref[...] = (dmv * sg).astype(BF16)
        dl_ref[...] = (dmv * y_ref[...].astype(F32) * sg * (1.0 - sg)).astype(BF16)

    gate = pl.BlockSpec((tm, tn), lambda i, j, r: (i, goff // tn + r * (D // tn) + j))
    branch = pl.BlockSpec((None, tm, tn), lambda i, j, r: (r, i, j))
    return pl.pallas_call(
        body, name=name, grid=(S // tm, D // tn, 3),
        in_specs=[pl.BlockSpec((tm, tn), lambda i, j, r: (i, j)), branch, gate],
        out_specs=[branch, gate],
        out_shape=[jax.ShapeDtypeStruct((3, S, D), BF16), jax.ShapeDtypeStruct(proj.shape, BF16)],
        compiler_params=_params("parallel", "parallel", "parallel"),
    )(dm, y3, proj)


def _ew_tiles(rows, cols):
    tc = cols if cols <= 4096 else _tile(cols, 2048)
    tr = _tile(rows, max(8, (1 << 19) // tc), 8)
    return tr, tc


def _half_rows_tile(Rh, Cs):
    return _tile(Rh, max(16, (1 << 20) // Cs), 16)


def _place_shard(w, l, kind, place, name):
    _, Rs, Cs = w.shape
    tr = _half_rows_tile(Rs, Cs)

    def body(pr_ref, w_ref, o_ref):
        o_ref[...] = w_ref[...].astype(BF16)

    if kind == "col":
        shape = (Rs, 4 * Cs)
        o_spec = pl.BlockSpec((tr, Cs), lambda i, pr: (i, pr[1]))
    else:
        shape = (4, Rs, Cs)
        o_spec = pl.BlockSpec((None, tr, Cs), lambda i, pr: (pr[1], i, 0))
    out = pl.pallas_call(
        body, name=name,
        grid_spec=pltpu.PrefetchScalarGridSpec(
            num_scalar_prefetch=1, grid=(Rs // tr,),
            in_specs=[pl.BlockSpec((None, tr, Cs), lambda i, pr: (l, i, 0))], out_specs=o_spec),
        out_shape=jax.ShapeDtypeStruct(shape, BF16), compiler_params=_params("parallel"),
    )(place, w)
    return out if kind == "col" else out.reshape(4 * Rs, Cs)


def _pair_sum(g, got, kind, place, name):
    _, Rh, Cs = got.shape
    tr = _half_rows_tile(Rh, Cs)
    if kind == "col":
        gv = g.reshape(2, Rh, 4 * Cs)
        g_spec = pl.BlockSpec((None, tr, Cs), lambda p, i, pr: (pr[0], i, p))
    else:
        gv = g.reshape(4, 2, Rh, Cs)
        g_spec = pl.BlockSpec((None, None, tr, Cs), lambda p, i, pr: (p, pr[0], i, 0))
    blk = pl.BlockSpec((None, tr, Cs), lambda p, i, pr: (p, i, 0))

    def body(pr_ref, g_ref, r_ref, o_ref):
        o_ref[...] = (g_ref[...].astype(F32) + r_ref[...].astype(F32)).astype(BF16)

    return pl.pallas_call(
        body, name=name,
        grid_spec=pltpu.PrefetchScalarGridSpec(num_scalar_prefetch=1, grid=(4, Rh // tr),
                                               in_specs=[g_spec, blk], out_specs=blk),
        out_shape=jax.ShapeDtypeStruct(got.shape, BF16), compiler_params=_params("parallel", "parallel"),
    )(place, gv, got)


def _chip_sum(part, got, place, stack, name):
    n, Rh, Cs = got.shape
    l, L, buf = stack
    tr = _half_rows_tile(Rh, Cs)
    prev = () if buf is None else (buf,)

    def body(pr_ref, p_ref, r_ref, *rest):
        acc = p_ref[...].astype(F32)
        for s in range(n):
            acc = acc + r_ref[s].astype(F32)
        rest[-1][...] = acc

    return pl.pallas_call(
        body, name=name,
        grid_spec=pltpu.PrefetchScalarGridSpec(
            num_scalar_prefetch=1, grid=(Rh // tr,),
            in_specs=[pl.BlockSpec((None, tr, Cs), lambda i, pr: (pr[1], i, 0)),
                      pl.BlockSpec((n, tr, Cs), lambda i, pr: (0, i, 0))] + [ANY] * len(prev),
            out_specs=pl.BlockSpec((None, None, tr, Cs), lambda i, pr: (l, pr[0], i, 0))),
        out_shape=jax.ShapeDtypeStruct((L, 2, Rh, Cs), F32), input_output_aliases={3: 0} if prev else {},
        compiler_params=_params("parallel"),
    )(place, part, got, *prev)


def _adamw(w, g, m, v, name):
    shape = w.shape
    args = [t.reshape(-1, shape[-1]) for t in (w, g, m, v)]
    rows, cols = args[0].shape
    tr, tc = _ew_tiles(rows, cols)
    c1 = 1.0 - ADAM_B1 ** ADAM_STEP
    c2 = 1.0 - ADAM_B2 ** ADAM_STEP

    def body(w_ref, g_ref, m_ref, v_ref, d_ref, nm_ref, nv_ref, go_ref):
        gv = g_ref[...]
        nm = ADAM_B1 * m_ref[...] + (1.0 - ADAM_B1) * gv
        nv = ADAM_B2 * v_ref[...] + (1.0 - ADAM_B2) * (gv * gv)
        nm_ref[...] = nm
        nv_ref[...] = nv
        go_ref[...] = gv
        d_ref[...] = -ADAM_LR * ((nm / c1) / (jnp.sqrt(nv / c2) + ADAM_EPS) + ADAM_WD * w_ref[...])

    blk = pl.BlockSpec((tr, tc), lambda i, j: (i, j))
    shp = jax.ShapeDtypeStruct((rows, cols), F32)
    outs = pl.pallas_call(
        body, name=name, grid=(rows // tr, cols // tc), in_specs=[blk] * 4, out_specs=[blk] * 4,
        out_shape=[shp] * 4, compiler_params=_params("parallel", "parallel"),
    )(*args)
    return [o.reshape(shape) for o in outs]


def _place():
    x, y, c = lax.axis_index("x"), lax.axis_index("y"), lax.axis_index("c")
    chips = [(1 - x, y), (x, 1 - y), (1 - x, 1 - y)]
    return x, y, c, chips


def _al(v, unit):
    return pl.multiple_of(v, unit) if unit % LANE == 0 else v


def _half_of_full(ref, kind, p, half, Rs, Cs):
    Rh = (ref.shape[-2] // 2) if kind == "col" else Rs // 2
    lead = (slice(None),) * (len(ref.shape) - 2)
    if kind == "col":
        return ref.at[lead + (pl.ds(_al(half * Rh, Rh), Rh), pl.ds(_al(p * Cs, Cs), Cs))]
    return ref.at[lead + (pl.ds(_al(p * Rs + half * Rh, Rh), Rh), slice(None))]


HBM = pl.BlockSpec(memory_space=pltpu.HBM)
SEM = pl.BlockSpec(memory_space=pltpu.SEMAPHORE)
EFFECT = pltpu.SideEffectType.DATAFLOW_SIDE_EFFECTING
FIRST_GATHER_ID = 0
GATHER_ID = 2
REDUCE_ID = 1


def _in_hbm(v):
    return pltpu.with_memory_space_constraint(v, pltpu.HBM)


def _ici_handshake(chips, c):
    barrier = pltpu.get_barrier_semaphore()
    for px, py in chips:
        pl.semaphore_signal(barrier, inc=1, device_id=(px, py, c), device_id_type=MESH)
    pl.semaphore_wait(barrier, len(chips))


def _shard_dims(ref, kind):
    R, C = ref.shape[-2:]
    return (R, C // 4) if kind == "col" else (R // 4, C)


def _gather_start(groups, after, collective_id, name):
    bufs = [b for grp in groups for b, _ in grp]
    kinds = [k for grp in groups for _, k in grp]
    m, ng = len(bufs), len(groups)

    def body(*refs):
        ins = refs[:m]
        sems = refs[m + 1:m + 1 + 2 * ng]
        token = refs[-1]
        x, y, c, chips = _place()
        _ici_handshake(chips, c)
        me = 2 * x + y
        at = 0
        for gi, grp in enumerate(groups):
            n = len(grp)
            for j, chip in enumerate(chips):
                for w in range(n):
                    ref, kind = ins[at + w], kinds[at + w]
                    mine = _half_of_full(ref, kind, me, c, *_shard_dims(ref, kind))
                    pltpu.make_async_remote_copy(mine, mine, sems[2 * gi].at[j * n + w], sems[2 * gi + 1].at[j * n + w],
                                                 device_id=(*chip, c), device_id_type=MESH).start()
            at += n
        token[...] = jnp.zeros_like(token)

    sem_shapes = [pltpu.SemaphoreType.DMA((3 * len(grp),)) for grp in groups for _ in range(2)]
    outs = pl.pallas_call(
        body, name=name, in_specs=[HBM] * m + [ANY],
        out_specs=[SEM] * (2 * ng) + [HBM] * m + [pl.BlockSpec(memory_space=pltpu.VMEM)],
        out_shape=sem_shapes + [pltpu.HBM(b.shape, b.dtype) for b in bufs] + [jax.ShapeDtypeStruct((8, LANE), F32)],
        input_output_aliases={i: 2 * ng + i for i in range(m)},
        compiler_params=pltpu.CompilerParams(has_side_effects=EFFECT, collective_id=collective_id),
    )(*[_in_hbm(b) for b in bufs], after)
    res, at = [], 2 * ng
    for gi, grp in enumerate(groups):
        res.append((outs[2 * gi], outs[2 * gi + 1], outs[at:at + len(grp)]))
        at += len(grp)
    return res, outs[-1]


def _gather_wait(bufs, kinds, send_sem, recv_sem, after, name):
    n = len(bufs)

    def body(*refs):
        ins = refs[:n]
        send, recv = refs[n], refs[n + 1]
        x, y, c, chips = _place()
        me = 2 * x + y
        for j, (px, py) in enumerate(chips):
            for w in range(n):
                dims = _shard_dims(ins[w], kinds[w])
                mine = _half_of_full(ins[w], kinds[w], me, c, *dims)
                theirs = _half_of_full(ins[w], kinds[w], 2 * px + py, c, *dims)
                cp = pltpu.make_async_remote_copy(mine, theirs, send.at[j * n + w], recv.at[j * n + w],
                                                  device_id=(px, py, c), device_id_type=MESH)
                cp.wait_send()
                cp.wait_recv()

    return pl.pallas_call(
        body, name=name, in_specs=[HBM] * n + [SEM, SEM, ANY], out_specs=[HBM] * n,
        out_shape=[pltpu.HBM(b.shape, b.dtype) for b in bufs],
        input_output_aliases={i: i for i in range(n)},
        compiler_params=pltpu.CompilerParams(has_side_effects=EFFECT),
    )(*bufs, send_sem, recv_sem, after)


def _ride_forward(bufs, kinds):
    n = len(bufs)

    def copies(ins, outs, send_sem, recv_sem):
        x, y, c, chips = _place()
        made = []
        for j, (px, py) in enumerate(chips):
            for w in range(n):
                got = _half_of_full(outs[w], kinds[w], 2 * px + py, c, *_shard_dims(outs[w], kinds[w]))
                made.append(pltpu.make_async_remote_copy(got, got, send_sem.at[j * n + w], recv_sem.at[j * n + w],
                                                         device_id=(x, y, 1 - c), device_id_type=MESH))
        return made

    return dict(arrays=list(bufs), out_shape=[jax.ShapeDtypeStruct(b.shape, b.dtype) for b in bufs], alias=True,
                n_sems=3 * n, copies=copies)


def _ride_halves(grads, kinds):
    n = len(grads)
    shapes = []
    for g, kind in zip(grads, kinds):
        R, C = g.shape
        shapes.append((4, R // 2, C // 4) if kind == "col" else (4, R // 8, C))

    def copies(ins, outs, send_sem, recv_sem):
        x, y, c, _ = _place()
        made = []
        for w in range(n):
            _, Rh, Cs = shapes[w]
            for p in range(4):
                made.append(pltpu.make_async_remote_copy(
                    _half_of_full(ins[w], kinds[w], p, 1 - c, 2 * Rh, Cs), outs[w].at[p], send_sem.at[4 * w + p],
                    recv_sem.at[4 * w + p], device_id=(x, y, 1 - c), device_id_type=MESH))
        return made

    return dict(arrays=list(grads), out_shape=[jax.ShapeDtypeStruct(s, BF16) for s in shapes], alias=False,
                n_sems=4 * n, copies=copies)


def _exchange_now(ride, name):
    n = len(ride["arrays"])

    def body(*refs):
        made = ride["copies"](refs[:n], refs[n:2 * n], refs[2 * n], refs[2 * n + 1])
        for cp in made:
            cp.start()
        for cp in made:
            cp.wait()

    return pl.pallas_call(
        body, name=name, in_specs=[ANY] * n, out_specs=[ANY] * n, out_shape=ride["out_shape"],
        input_output_aliases={w: w for w in range(n)} if ride["alias"] else {},
        scratch_shapes=[pltpu.SemaphoreType.DMA((ride["n_sems"],))] * 2,
    )(*ride["arrays"])


def _reduce_start(parts, name):
    n = len(parts)
    zones = [lax.empty((3,) + p.shape[1:], p.dtype) for p in parts]

    def body(*refs):
        ins, lands = refs[:n], refs[n:2 * n]
        send, recv = refs[2 * n], refs[2 * n + 1]
        token = refs[-1]
        x, y, c, chips = _place()
        _ici_handshake(chips, c)
        for j, (px, py) in enumerate(chips):
            for w in range(n):
                pltpu.make_async_remote_copy(ins[w].at[2 * px + py], lands[w].at[j], send.at[j * n + w],
                                             recv.at[j * n + w], device_id=(px, py, c), device_id_type=MESH).start()
        token[...] = jnp.zeros_like(token)

    both = list(parts) + zones
    outs = pl.pallas_call(
        body, name=name, in_specs=[HBM] * (2 * n),
        out_specs=[SEM, SEM] + [HBM] * (2 * n) + [pl.BlockSpec(memory_space=pltpu.VMEM)],
        out_shape=([pltpu.SemaphoreType.DMA((3 * n,))] * 2 + [pltpu.HBM(b.shape, b.dtype) for b in both]
                   + [jax.ShapeDtypeStruct((8, LANE), F32)]),
        input_output_aliases={i: 2 + i for i in range(2 * n)},
        compiler_params=pltpu.CompilerParams(has_side_effects=EFFECT, collective_id=REDUCE_ID),
    )(*[_in_hbm(b) for b in both])
    return outs[0], outs[1], outs[2:2 + n], outs[2 + n:2 + 2 * n], outs[-1]


def _reduce_wait(parts, zones, send_sem, recv_sem, after, name):
    n = len(parts)

    def body(*refs):
        ins, lands = refs[:n], refs[n:2 * n]
        send, recv = refs[2 * n], refs[2 * n + 1]
        x, y, c, chips = _place()
        for j, (px, py) in enumerate(chips):
            for w in range(n):
                cp = pltpu.make_async_remote_copy(ins[w].at[2 * px + py], lands[w].at[j], send.at[j * n + w],
                                                  recv.at[j * n + w], device_id=(px, py, c), device_id_type=MESH)
                cp.wait_send()
                cp.wait_recv()

    both = list(parts) + list(zones)
    outs = pl.pallas_call(
        body, name=name, in_specs=[HBM] * (2 * n) + [SEM, SEM, ANY], out_specs=[HBM] * (2 * n),
        out_shape=[pltpu.HBM(b.shape, b.dtype) for b in both],
        input_output_aliases={i: i for i in range(2 * n)},
        compiler_params=pltpu.CompilerParams(has_side_effects=EFFECT),
    )(*both, send_sem, recv_sem, after)
    return outs[:n], outs[n:]


def _share_with_sibling(reduced, name):
    n = len(reduced)

    def body(*refs):
        outs = refs[n:2 * n]
        send_sem, recv_sem = refs[2 * n:]
        x, y, c, _ = _place()
        sib = (x, y, 1 - c)
        copies = []
        for w in range(n):
            mine = outs[w].at[:, c]
            cp = pltpu.make_async_remote_copy(mine, mine, send_sem.at[w], recv_sem.at[w], device_id=sib,
                                              device_id_type=MESH)
            cp.start()
            copies.append(cp)
        for cp in copies:
            cp.wait()

    return pl.pallas_call(
        body, name=name, in_specs=[ANY] * n, out_specs=[ANY] * n,
        out_shape=[jax.ShapeDtypeStruct(r.shape, r.dtype) for r in reduced],
        input_output_aliases={w: w for w in range(n)},
        scratch_shapes=[pltpu.SemaphoreType.DMA((n,))] * 2,
    )(*reduced)


def _all_gather_small(v):
    r = v.shape[0]

    def body(v_ref, o_ref, send_sem, recv_sem):
        x, y, c, _ = _place()
        me = 4 * x + 2 * y + c
        o_ref[me] = v_ref[...]
        copies = []
        for k in range(1, 8):
            peer = (x ^ (k >> 2), y ^ ((k >> 1) & 1), c ^ (k & 1))
            cp = pltpu.make_async_remote_copy(v_ref, o_ref.at[me], send_sem.at[k - 1], recv_sem.at[k - 1],
                                              device_id=peer, device_id_type=MESH)
            cp.start()
            copies.append(cp)
        for cp in copies:
            cp.wait()

    vmem = pl.BlockSpec(memory_space=pltpu.VMEM)
    return pl.pallas_call(
        body, name="all_gather_small", in_specs=[vmem], out_specs=vmem,
        out_shape=jax.ShapeDtypeStruct((8, r, LANE), F32),
        scratch_shapes=[pltpu.SemaphoreType.DMA((7,)), pltpu.SemaphoreType.DMA((7,))],
        compiler_params=pltpu.CompilerParams(vmem_limit_bytes=VMEM_LIMIT),
    )(v)


def _sum_slots(g):
    n, r, _ = g.shape

    def body(g_ref, o_ref):
        acc = g_ref[0]
        for s in range(1, n):
            acc = acc + g_ref[s]
        o_ref[...] = acc

    vmem = pl.BlockSpec(memory_space=pltpu.VMEM)
    return pl.pallas_call(
        body, name="sum_slots", in_specs=[vmem], out_specs=vmem, out_shape=jax.ShapeDtypeStruct((r, LANE), F32),
        compiler_params=pltpu.CompilerParams(vmem_limit_bytes=VMEM_LIMIT),
    )(g)


def _pack(arrays):
    flat = jnp.concatenate([a.reshape(-1) for a in arrays])
    pad = (-flat.shape[0]) % (8 * LANE)
    return jnp.pad(flat, (0, pad)).reshape(-1, LANE)


def _unpack(packed, like):
    flat = packed.reshape(-1)
    out, off = [], 0
    for a in like:
        out.append(flat[off:off + a.size].reshape(a.shape))
        off += a.size
    return out


def _offsets(D):
    DA, DS, DC = D // 2, D // 4, D // 4
    names = ["q", "k", "v", "sc_b", "sc_c", "sc_u", "cf_a", "cf_g", "gate"]
    sizes = [DA, DA, DA, DS, DS, DS, DC, DC, 3 * D]
    offs, o = {}, 0
    for nm, sz in zip(names, sizes):
        offs[nm] = o
        o += sz
    return offs, DA, DS, DC


def _relu2_epilogue(acc):
    r = jnp.maximum(acc, 0.0)
    return acc, r * r


def _drelu2_epilogue(acc, up):
    return (acc * (2.0 * jnp.maximum(up.astype(F32), 0.0)),)


def _local_step(x, target, gains, conv_a_w, conv_c_w, conv_c_b, norm_c_g, norm_c_b, landed, emit_grads):
    S, D = x.shape
    L = gains[0].shape[0]
    offs, DA, DS, DC = _offsets(D)
    goff = offs["gate"]
    g_mix_pre, g_mix_post, g_mlp_pre, g_mlp_post = gains
    cb3, ng3, nb3 = (t.reshape(L, 1, DC) for t in (conv_c_b, norm_c_g, norm_c_b))

    saved = []
    h = _rms_fwd(x, g_mix_pre, 0, "rms_first")
    xin = x
    def whole(l, part, after, carrier=None):
        names, halves, kinds = landed(l, part, after)
        ride = _ride_forward(halves, kinds)
        if carrier is None:
            return list(zip(names, _exchange_now(ride, f"gather_forward_{part}_{l}"))), None
        result, made = carrier(ride)
        return list(zip(names, made)), result

    next_in, _ = whole(0, "in", xin)
    for l in range(L):
        big = dict(next_in)
        proj = _mm(h, big["w_in"], out_dtypes=(BF16,), name=f"fwd_w_in_{l}", tn_cap=512)
        ga = _sc_fwd(proj, conv_a_w, l, offs, DS, f"sc_fwd_{l}")
        attn, attn_tot = _attn_fwd(proj, offs, DA, f"attn_fwd_{l}")
        u1 = _cf_conv_fwd(proj, conv_c_w, cb3, l, offs, DC, f"cf_conv_fwd_{l}")
        u2 = _cf_norm_fwd(u1, ng3, nb3, l, f"cf_norm_fwd_{l}")
        big.update(whole(l, "rest", attn)[0])
        y3, merged = _merge_fwd(ga, attn, u2, big["proj_a"], big["proj_b"], big["proj_c"], proj, goff,
                                f"merge_fwd_{l}")
        mixed = _mm(merged, big["w_o"], name=f"fwd_w_o_{l}")
        x1, h2 = _post_res_fwd(xin, mixed, g_mix_post, l, g_mlp_pre, l, f"mix_residual_{l}")
        up, act = _mm(h2, big["w_up"], out_dtypes=(BF16, BF16), epilogue=_relu2_epilogue, name=f"fwd_w_up_{l}")
        if l + 1 < L:
            next_in, f = whole(l + 1, "in", act,
                               lambda ride: _mm(act, big["w_down"], name=f"fwd_w_down_{l}", ride=ride))
        else:
            f = _mm(act, big["w_down"], name=f"fwd_w_down_{l}")
        saved.append(dict(big=big, xin=xin, h=h, proj=proj, ga=ga, attn=attn, attn_tot=attn_tot, u1=u1, u2=u2, y3=y3,
                          merged=merged, mixed=mixed, x1=x1, h2=h2, up=up, act=act, f=f))
        if l + 1 < L:
            xin, h = _post_res_fwd(x1, f, g_mlp_post, l, g_mix_pre, l + 1, f"mlp_residual_{l}")
        else:
            dx, loss = _final_fwd_loss(x1, f, g_mlp_post, l, target, "loss_head")

    small = {k: [None] * L for k in ("mix_pre", "mix_post", "mlp_pre", "mlp_post", "conv_a_w", "conv_c_w",
                                       "conv_c_b", "norm_c_g", "norm_c_b")}

    def dw(key, a, b, l, **kw):
        return _mm(a, b, ta=True, out_dtypes=(BF16,), name=f"d{key}_{l}", **kw)

    def halves_of(g):
        names = [k for k in BIG if k in g]
        return names, [BIG_KIND[k] for k in names]

    for l in reversed(range(L)):
        s = saved[l]
        big = s["big"]
        g = {}
        df, small["mlp_post"][l] = _post_bwd(dx, s["f"], g_mlp_post, l, f"mlp_post_bwd_{l}")
        g["w_down"] = dw("w_down", s["act"], df, l)
        dup = _mm(df, big["w_down"], tb=True, out_dtypes=(BF16,), epilogue=_drelu2_epilogue, extras=(s["up"],),
                  name=f"d_up_{l}")
        g["w_up"] = dw("w_up", s["h2"], dup, l)
        names, kinds = halves_of(g)
        dh2, got = _mm(dup, big["w_up"], tb=True, name=f"d_h2_{l}", ride=_ride_halves([g[k] for k in names], kinds))
        dx1, small["mlp_pre"][l] = _pre_bwd(dx, dh2, s["x1"], g_mlp_pre, l, f"mlp_pre_bwd_{l}")
        g_mix_post = g_mix_post + emit_grads(l, "mlp", names, [g[k] for k in names], got)[0, 0]
        g = {}
        dmixed, small["mix_post"][l] = _post_bwd(dx1, s["mixed"], g_mix_post, l, f"mix_post_bwd_{l}")
        g["w_o"] = dw("w_o", s["merged"], dmixed, l)
        dmerged = _mm(dmixed, big["w_o"], tb=True, name=f"d_merged_{l}")
        dy3, dproj = _merge_bwd(dmerged, s["y3"], s["proj"], goff, f"merge_bwd_{l}")
        g["proj_a"] = dw("proj_a", s["ga"], dy3, l, b_at=0)
        g["proj_b"] = dw("proj_b", s["attn"], dy3, l, b_at=1)
        g["proj_c"] = dw("proj_c", s["u2"], dy3, l, b_at=2)
        dga = _mm(dy3, big["proj_a"], tb=True, name=f"d_ga_{l}", a_at=0)
        dattn = _mm(dy3, big["proj_b"], tb=True, out_dtypes=(BF16,), name=f"d_attn_{l}", a_at=1)
        du2 = _mm(dy3, big["proj_c"], tb=True, name=f"d_u2_{l}", a_at=2)
        dsb, dsc, dsu, small["conv_a_w"][l] = _sc_bwd(dga, s["proj"], conv_a_w, l, offs, DS, f"sc_bwd_{l}")
        du1, small["norm_c_g"][l], small["norm_c_b"][l] = _cf_norm_bwd(du2, s["u1"], ng3, nb3, l, f"cf_norm_bwd_{l}")
        dca, dcg, small["conv_c_w"][l], small["conv_c_b"][l] = _cf_conv_bwd(du1, s["proj"], conv_c_w, l, offs, DC,
                                                                          f"cf_conv_bwd_{l}")
        dproj = _put_columns(dproj, [dsb, dsc, dsu, dca, dcg], offs["sc_b"], f"conv_grads_{l}")
        dproj = _attn_bwd(dattn, s["attn_tot"], s["proj"], offs, DA, dproj, f"attn_bwd_{l}")
        g["w_in"] = dw("w_in", s["h"], dproj, l, tn_cap=512)
        names, kinds = halves_of(g)
        dh, got = _mm(dproj, big["w_in"], tb=True, name=f"d_h_{l}", tk_cap=3072,
                      ride=_ride_halves([g[k] for k in names], kinds))
        dx, small["mix_pre"][l] = _pre_bwd(dx1, dh, s["xin"], g_mix_pre, l, f"mix_pre_bwd_{l}")
        g_mlp_post = g_mlp_post + emit_grads(l, "mix", names, [g[k] for k in names], got)[0, 0]
    return loss, dx, small


BIG = ("w_in", "proj_a", "proj_b", "proj_c", "w_o", "w_up", "w_down")
BIG_KIND = {"w_in": "col", "proj_a": "col", "proj_b": "col", "proj_c": "col", "w_o": "row", "w_up": "col",
            "w_down": "row"}


def kernel(x, ln_mix_pre, ln_mix_post, ln_mlp_pre, ln_mlp_post, w_in, conv_a_w, proj_a, proj_b, conv_c_w, conv_c_b, norm_c_g, norm_c_b, proj_c, w_o, w_up, w_down, loss_target, m_ln_mix_pre, m_ln_mix_post, m_ln_mlp_pre, m_ln_mlp_post, m_w_in, m_conv_a_w, m_proj_a, m_proj_b, m_conv_c_w, m_conv_c_b, m_norm_c_g, m_norm_c_b, m_proj_c, m_w_o, m_w_up, m_w_down, v_ln_mix_pre, v_ln_mix_post, v_ln_mlp_pre, v_ln_mlp_post, v_w_in, v_conv_a_w, v_proj_a, v_proj_b, v_conv_c_w, v_conv_c_b, v_norm_c_g, v_norm_c_b, v_proj_c, v_w_o, v_w_up, v_w_down):
    weights = dict(ln_mix_pre=ln_mix_pre, ln_mix_post=ln_mix_post, ln_mlp_pre=ln_mlp_pre, ln_mlp_post=ln_mlp_post,
                   w_in=w_in, conv_a_w=conv_a_w, proj_a=proj_a, proj_b=proj_b, conv_c_w=conv_c_w, conv_c_b=conv_c_b,
                   norm_c_g=norm_c_g, norm_c_b=norm_c_b, proj_c=proj_c, w_o=w_o, w_up=w_up, w_down=w_down)
    m_in = dict(ln_mix_pre=m_ln_mix_pre, ln_mix_post=m_ln_mix_post, ln_mlp_pre=m_ln_mlp_pre, ln_mlp_post=m_ln_mlp_post,
                w_in=m_w_in, conv_a_w=m_conv_a_w, proj_a=m_proj_a, proj_b=m_proj_b, conv_c_w=m_conv_c_w,
                conv_c_b=m_conv_c_b, norm_c_g=m_norm_c_g, norm_c_b=m_norm_c_b, proj_c=m_proj_c, w_o=m_w_o,
                w_up=m_w_up, w_down=m_w_down)
    v_in = dict(ln_mix_pre=v_ln_mix_pre, ln_mix_post=v_ln_mix_post, ln_mlp_pre=v_ln_mlp_pre, ln_mlp_post=v_ln_mlp_post,
                w_in=v_w_in, conv_a_w=v_conv_a_w, proj_a=v_proj_a, proj_b=v_proj_b, conv_c_w=v_conv_c_w,
                conv_c_b=v_conv_c_b, norm_c_g=v_norm_c_g, norm_c_b=v_norm_c_b, proj_c=v_proj_c, w_o=v_w_o,
                w_up=v_w_up, w_down=v_w_down)
    order = list(weights)
    L, D = ln_mix_pre.shape
    chip = 2 * lax.axis_index("x") + lax.axis_index("y")
    place = jnp.stack([lax.axis_index("c"), chip]).astype(jnp.int32)

    conv_local = [conv_a_w, conv_c_w]
    slots = _all_gather_small(_pack(conv_local))
    gather_groups = {"in": ("w_in",), "rest": ("proj_a", "proj_b", "proj_c", "w_o", "w_up", "w_down")}
    def placed(layers, where):
        return [[(_place_shard(weights[k], l, BIG_KIND[k], where, f"place_{k}_{l}"), BIG_KIND[k]) for k in names]
                for l in layers for names in gather_groups.values()]

    first, token = _gather_start(placed([0], place), slots, FIRST_GATHER_ID, "gather_start_first")
    later, started_all = _gather_start(placed(range(1, L), place + token[0, 0].astype(jnp.int32)), token, GATHER_ID,
                                       "gather_start")
    started = first + later
    in_flight = {(l, part): started[l * len(gather_groups) + i]
                 for l in range(L) for i, part in enumerate(gather_groups)}

    def landed(l, part, after):
        names = gather_groups[part]
        kinds = [BIG_KIND[k] for k in names]
        send_sem, recv_sem, bufs = in_flight[l, part]
        return names, _gather_wait(bufs, kinds, send_sem, recv_sem, after, f"gather_wait_{part}_{l}"), kinds

    pending = []

    def emit_grads(l, part, names, glist, got):
        kinds = [BIG_KIND[k] for k in names]
        pair = [_pair_sum(gk, r, kd, place, f"pair_sum_{k}_{l}") for k, kd, gk, r in zip(names, kinds, glist, got)]
        send_sem, recv_sem, parts, zones, token = _reduce_start(pair, f"reduce_start_{part}_{l}")
        pending.append((l, part, names, parts, zones, send_sem, recv_sem))
        return token

    per_chip = [_unpack(slots[4 * px + 2 * py], conv_local) for px in range(2) for py in range(2)]
    conv_a_full = jnp.concatenate([pc[0] for pc in per_chip], axis=-1)
    conv_c_full = jnp.concatenate([pc[1] for pc in per_chip], axis=-1)

    gains = [weights[k].reshape(L, 1, D) for k in ("ln_mix_pre", "ln_mix_post", "ln_mlp_pre", "ln_mlp_post")]
    gains[0] = gains[0] + started_all[0, 0]
    loss, dx, small = _local_step(x[0], loss_target[0], gains, conv_a_full, conv_c_full, conv_c_b, norm_c_g,
                                  norm_c_b, landed, emit_grads)

    grads, delta, new_m, new_v = {}, {}, {}, {}

    def finish(part, after):
        reduced = {}
        for l, p, names, parts, zones, send_sem, recv_sem in pending:
            if p == part:
                parts, landed = _reduce_wait(parts, zones, send_sem, recv_sem, after, f"reduce_wait_{p}_{l}")
                for k, mine, theirs in zip(names, parts, landed):
                    reduced[k] = _chip_sum(mine, theirs, place, (l, L, reduced.get(k)), f"chip_sum_{k}_{l}")
        for k, r in zip(reduced, _share_with_sibling(list(reduced.values()), f"reduce_share_{part}")):
            whole = r.reshape(r.shape[0], r.shape[1] * r.shape[2], r.shape[3])
            delta[k], new_m[k], new_v[k], grads[k] = _adamw(weights[k], whole, m_in[k], v_in[k], f"adamw_{k}")

    finish("mlp", dx)
    finish("mix", delta["w_down"])

    small_names = ["ln_mix_pre", "ln_mix_post", "ln_mlp_pre", "ln_mlp_post", "conv_a_w", "conv_c_w", "conv_c_b",
                   "norm_c_g", "norm_c_b"]
    small_key = dict(ln_mix_pre="mix_pre", ln_mix_post="mix_post", ln_mlp_pre="mlp_pre", ln_mlp_post="mlp_post")
    small_local = []
    for k in small_names:
        per_layer = small[small_key.get(k, k)]
        stacked = jnp.stack(per_layer)
        small_local.append(stacked.reshape(L, -1) if stacked.shape[1] == 1 else stacked)
    small_sum = _unpack(_sum_slots(_all_gather_small(_pack(small_local))), small_local)
    for k, g in zip(small_names, small_sum):
        if k in ("conv_a_w", "conv_c_w"):
            width = weights[k].shape[-1]
            g = lax.dynamic_slice_in_dim(g, chip * width, width, axis=2)
        grads[k] = g

    packed = [_pack([t[k] for k in small_names]) for t in (weights, grads, m_in, v_in)]
    like = [weights[k] for k in small_names]
    for dst, res in zip((delta, new_m, new_v), _adamw(*packed, "adamw_small")):
        dst.update(zip(small_names, _unpack(res, like)))

    total = lax.psum(loss[0, 0], ("x", "y", "c"))
    return (total, dx[None], *[grads[k] for k in order], *[delta[k] for k in order],
            *[new_m[k] for k in order], *[new_v[k] for k in order])
```

```python
import functools

import jax
import jax.numpy as jnp
from jax import lax
from jax.experimental import pallas as pl
from jax.experimental.pallas import tpu as pltpu

F32 = jnp.float32
BF16 = jnp.bfloat16
MESH = pl.DeviceIdType.MESH

HEAD_DIM = 128
QB = 128
ATTN_FWD_HEADS = 4
ATTN_BWD_HEADS = 4
RMS_EPS = 1e-6
LN_EPS = 1e-5
ADAM_LR = 0.001
ADAM_B1 = 0.9
ADAM_B2 = 0.999
ADAM_EPS = 1e-08
ADAM_WD = 0.01
ADAM_STEP = 10
LANE = 128
VMEM_LIMIT = 56 * 1024 * 1024
CONV_PAD = 32
CONV_ROWS = 256
ANY = pl.BlockSpec(memory_space=pl.ANY)


def _tile(n, cap, mult=LANE):
    best = None
    t = mult
    while t <= min(n, cap):
        if n % t == 0:
            best = t
        t += mult
    return best if best is not None else n


def _params(*sem):
    return pltpu.CompilerParams(dimension_semantics=sem if sem else None, vmem_limit_bytes=VMEM_LIMIT)


def _sigmoid(x):
    return 1.0 / (1.0 + jnp.exp(-x))


def _mm(a, b, *, name, ta=False, tb=False, out_dtypes=(F32,), epilogue=None, extras=(),
        tm_cap=1024, tn_cap=1024, tk_cap=2048, ride=None, a_at=None, b_at=None):
    if ta:
        K, M = a.shape[-2:]
    else:
        M, K = a.shape[-2:]
    N = b.shape[-2] if tb else b.shape[-1]
    tm, tn, tk = _tile(M, tm_cap), _tile(N, tn_cap), _tile(K, tk_cap)
    gm, gn, nk = M // tm, N // tn, K // tk

    def spec(blk, idx, at):
        return pl.BlockSpec(blk, idx) if at is None else pl.BlockSpec((None,) + blk, lambda i, j, k: (at,) + idx(i, j, k))

    a_spec = spec((tk, tm), lambda i, j, k: (k, i), a_at) if ta else spec((tm, tk), lambda i, j, k: (i, k), a_at)
    b_spec = spec((tn, tk), lambda i, j, k: (j, k), b_at) if tb else spec((tk, tn), lambda i, j, k: (k, j), b_at)
    e_specs = [pl.BlockSpec((tm, tn), lambda i, j, k: (i, j)) for _ in extras]
    dims = (((0 if ta else 1,), (1 if tb else 0,)), ((), ()))
    n_e, n_o = len(extras), len(out_dtypes)
    n_r = len(ride["arrays"]) if ride else 0

    def body(a_ref, b_ref, *rest):
        e_refs, o_refs = rest[:n_e], rest[n_e + n_r:n_e + n_r + n_o]
        step = (pl.program_id(0) * gn + pl.program_id(1)) * nk + pl.program_id(2)

        def riding():
            at = n_e + n_r + n_o
            send_sem, recv_sem = rest[-2], rest[-1]
            return ride["copies"](rest[n_e:n_e + n_r], rest[at:at + n_r], send_sem, recv_sem)

        if ride:
            @pl.when(step == 0)
            def _():
                for cp in riding():
                    cp.start()

        part = lax.dot_general(a_ref[...].astype(BF16), b_ref[...].astype(BF16), dims, preferred_element_type=F32)

        def finish(acc):
            outs = (acc,) if epilogue is None else epilogue(acc, *[e[...] for e in e_refs])
            for o_ref, o in zip(o_refs, outs):
                o_ref[...] = o.astype(o_ref.dtype)

        if nk == 1:
            finish(part)
        else:
            acc_ref = rest[n_e + 2 * n_r + n_o]
            k = pl.program_id(2)

            @pl.when(k == 0)
            def _():
                acc_ref[...] = part

            @pl.when(k > 0)
            def _():
                acc_ref[...] += part

            @pl.when(k == nk - 1)
            def _():
                finish(acc_ref[...])

        if ride:
            @pl.when(step == gm * gn * nk - 1)
            def _():
                for cp in riding():
                    cp.wait()

    scratch = [pltpu.VMEM((tm, tn), F32)] if nk > 1 else []
    if ride:
        scratch += [pltpu.SemaphoreType.DMA((ride["n_sems"],))] * 2
    outs = pl.pallas_call(
        body, name=name, grid=(gm, gn, nk),
        in_specs=[a_spec, b_spec] + e_specs + [ANY] * n_r,
        out_specs=[pl.BlockSpec((tm, tn), lambda i, j, k: (i, j)) for _ in out_dtypes] + [ANY] * n_r,
        out_shape=[jax.ShapeDtypeStruct((M, N), dt) for dt in out_dtypes] + (ride["out_shape"] if ride else []),
        input_output_aliases={2 + n_e + w: n_o + w for w in range(n_r)} if ride and ride["alias"] else {},
        scratch_shapes=scratch,
        compiler_params=_params(*(("arbitrary",) * 3 if ride else ("parallel", "parallel", "arbitrary"))),
    )(a, b, *extras, *(ride["arrays"] if ride else ()))
    main = outs[0] if n_o == 1 else outs[:n_o]
    return (main, outs[n_o:]) if ride else main


def _row_tile(S):
    return _tile(S, 256, 8)


def _gain_spec(D, l):
    return pl.BlockSpec((None, 1, D), lambda i: (l, 0, 0))


def _rms(x, g):
    r = lax.rsqrt(jnp.mean(x * x, axis=-1, keepdims=True) + RMS_EPS)
    return x * r * g


def _rms_fwd(x, g3, l, name):
    S, D = x.shape
    tr = _row_tile(S)

    def body(x_ref, g_ref, h_ref):
        h_ref[...] = _rms(x_ref[...], g_ref[...]).astype(BF16)

    return pl.pallas_call(
        body, name=name, grid=(S // tr,),
        in_specs=[pl.BlockSpec((tr, D), lambda i: (i, 0)), _gain_spec(D, l)],
        out_specs=pl.BlockSpec((tr, D), lambda i: (i, 0)),
        out_shape=jax.ShapeDtypeStruct((S, D), BF16),
        compiler_params=_params("parallel"),
    )(x, g3)


def _post_res_fwd(x_in, f, gpost3, l, gnext3, lnext, name):
    S, D = x_in.shape
    tr = _row_tile(S)

    def body(x_ref, f_ref, gp_ref, gn_ref, xo_ref, h_ref):
        xo = x_ref[...] + _rms(f_ref[...], gp_ref[...])
        xo_ref[...] = xo
        h_ref[...] = _rms(xo, gn_ref[...]).astype(BF16)

    row = pl.BlockSpec((tr, D), lambda i: (i, 0))
    return pl.pallas_call(
        body, name=name, grid=(S // tr,),
        in_specs=[row, row, _gain_spec(D, l), _gain_spec(D, lnext)],
        out_specs=[row, row],
        out_shape=[jax.ShapeDtypeStruct((S, D), F32), jax.ShapeDtypeStruct((S, D), BF16)],
        compiler_params=_params("parallel"),
    )(x_in, f, gpost3, gnext3)


def _final_fwd_loss(x_in, f, gpost3, l, target, name):
    S, D = x_in.shape
    tr = _row_tile(S)

    def body(x_ref, f_ref, gp_ref, t_ref, dx_ref, loss_ref):
        @pl.when(pl.program_id(0) == 0)
        def _():
            loss_ref[...] = jnp.zeros_like(loss_ref)

        err = x_ref[...] + _rms(f_ref[...], gp_ref[...]) - t_ref[...]
        dx_ref[...] = err * (1.0 / D)
        loss_ref[...] += 0.5 * jnp.sum(jnp.mean(err * err, axis=-1, keepdims=True))

    row = pl.BlockSpec((tr, D), lambda i: (i, 0))
    return pl.pallas_call(
        body, name=name, grid=(S // tr,),
        in_specs=[row, row, _gain_spec(D, l), row],
        out_specs=[row, pl.BlockSpec((8, LANE), lambda i: (0, 0))],
        out_shape=[jax.ShapeDtypeStruct((S, D), F32), jax.ShapeDtypeStruct((8, LANE), F32)],
        compiler_params=_params("arbitrary"),
    )(x_in, f, gpost3, target)


def _rms_bwd_rows(dy, x, g):
    r = lax.rsqrt(jnp.mean(x * x, axis=-1, keepdims=True) + RMS_EPS)
    t = dy * g
    dx = r * t - x * (r * r * r) * jnp.mean(t * x, axis=-1, keepdims=True)
    return dx, dy * x * r


def _post_bwd(dxo, f, gpost3, l, name):
    S, D = f.shape
    tr = _row_tile(S)

    def body(d_ref, f_ref, g_ref, df_ref, dg_ref):
        @pl.when(pl.program_id(0) == 0)
        def _():
            dg_ref[...] = jnp.zeros_like(dg_ref)

        df, dg = _rms_bwd_rows(d_ref[...], f_ref[...], g_ref[...])
        df_ref[...] = df.astype(BF16)
        dg_ref[...] += jnp.sum(dg, axis=0, keepdims=True)

    row = pl.BlockSpec((tr, D), lambda i: (i, 0))
    return pl.pallas_call(
        body, name=name, grid=(S // tr,),
        in_specs=[row, row, _gain_spec(D, l)],
        out_specs=[row, pl.BlockSpec((1, D), lambda i: (0, 0))],
        out_shape=[jax.ShapeDtypeStruct((S, D), BF16), jax.ShapeDtypeStruct((1, D), F32)],
        compiler_params=_params("arbitrary"),
    )(dxo, f, gpost3)


def _pre_bwd(dxo, dh, x_in, gpre3, l, name):
    S, D = x_in.shape
    tr = _row_tile(S)

    def body(d_ref, dh_ref, x_ref, g_ref, dx_ref, dg_ref):
        @pl.when(pl.program_id(0) == 0)
        def _():
            dg_ref[...] = jnp.zeros_like(dg_ref)

        dx, dg = _rms_bwd_rows(dh_ref[...], x_ref[...], g_ref[...])
        dx_ref[...] = d_ref[...] + dx
        dg_ref[...] += jnp.sum(dg, axis=0, keepdims=True)

    row = pl.BlockSpec((tr, D), lambda i: (i, 0))
    return pl.pallas_call(
        body, name=name, grid=(S // tr,),
        in_specs=[row, row, row, _gain_spec(D, l)],
        out_specs=[row, pl.BlockSpec((1, D), lambda i: (0, 0))],
        out_shape=[jax.ShapeDtypeStruct((S, D), F32), jax.ShapeDtypeStruct((1, D), F32)],
        compiler_params=_params("arbitrary"),
    )(dxo, dh, x_in, gpre3)


def _zero_pads(pad_ref, S):
    z = jnp.zeros((CONV_PAD, pad_ref.shape[1]), F32)
    pad_ref[pl.ds(0, CONV_PAD), :] = z
    pad_ref[pl.ds(CONV_PAD + S, CONV_PAD), :] = z


def _conv_fwd_chunk(pad_ref, w_ref, K, r0, rows):
    acc = None
    for k in range(K):
        term = w_ref[pl.ds(k, 1), :] * pad_ref[pl.ds(CONV_PAD + r0 - (K - 1) + k, rows), :]
        acc = term if acc is None else acc + term
    return acc


def _conv_bwd_chunk(pad_ref, w_ref, K, r0, rows):
    acc = None
    for k in range(K):
        term = w_ref[pl.ds(k, 1), :] * pad_ref[pl.ds(CONV_PAD + r0 + (K - 1) - k, rows), :]
        acc = term if acc is None else acc + term
    return acc


def _conv_dw(upad_ref, dy_ref_or_pad, dy_off, K, S, dw_ref):
    rows = min(CONV_ROWS, S)
    for k in range(K):
        acc = None
        for r0 in range(0, S, rows):
            term = jnp.sum(dy_ref_or_pad[pl.ds(dy_off + r0, rows), :]
                           * upad_ref[pl.ds(CONV_PAD + r0 - (K - 1) + k, rows), :], axis=0, keepdims=True)
            acc = term if acc is None else acc + term
        dw_ref[pl.ds(k, 1), :] = acc


def _col_spec(S, off):
    return pl.BlockSpec((S, LANE), lambda j: (0, off // LANE + j))


def _sc_fwd(proj, conv_w, l, offs, DS, name):
    S = proj.shape[0]
    K = conv_w.shape[1]
    rows = min(CONV_ROWS, S)

    def body(b_ref, c_ref, u_ref, w_ref, o_ref, pad_ref):
        _zero_pads(pad_ref, S)
        pad_ref[pl.ds(CONV_PAD, S), :] = c_ref[...].astype(F32) * u_ref[...].astype(F32)
        for r0 in range(0, S, rows):
            cv = _conv_fwd_chunk(pad_ref, w_ref, K, r0, rows)
            o_ref[pl.ds(r0, rows), :] = (b_ref[pl.ds(r0, rows), :].astype(F32) * cv).astype(BF16)

    return pl.pallas_call(
        body, name=name, grid=(DS // LANE,),
        in_specs=[_col_spec(S, offs["sc_b"]), _col_spec(S, offs["sc_c"]), _col_spec(S, offs["sc_u"]),
                  pl.BlockSpec((None, K, LANE), lambda j: (l, 0, j))],
        out_specs=pl.BlockSpec((S, LANE), lambda j: (0, j)),
        out_shape=jax.ShapeDtypeStruct((S, DS), BF16),
        scratch_shapes=[pltpu.VMEM((S + 2 * CONV_PAD, LANE), F32)],
        compiler_params=_params("parallel"),
    )(proj, proj, proj, conv_w)


def _sc_bwd(dga, proj, conv_w, l, offs, DS, name):
    S = proj.shape[0]
    K = conv_w.shape[1]
    rows = min(CONV_ROWS, S)

    def body(d_ref, b_ref, c_ref, u_ref, w_ref, db_ref, dc_ref, du_ref, dw_ref, tpad_ref, gpad_ref):
        _zero_pads(tpad_ref, S)
        _zero_pads(gpad_ref, S)
        tpad_ref[pl.ds(CONV_PAD, S), :] = c_ref[...].astype(F32) * u_ref[...].astype(F32)
        for r0 in range(0, S, rows):
            sl = pl.ds(r0, rows)
            cv = _conv_fwd_chunk(tpad_ref, w_ref, K, r0, rows)
            d = d_ref[sl, :]
            db_ref[sl, :] = (d * cv).astype(BF16)
            gpad_ref[pl.ds(CONV_PAD + r0, rows), :] = d * b_ref[sl, :].astype(F32)
        for r0 in range(0, S, rows):
            sl = pl.ds(r0, rows)
            dt = _conv_bwd_chunk(gpad_ref, w_ref, K, r0, rows)
            dc_ref[sl, :] = (dt * u_ref[sl, :].astype(F32)).astype(BF16)
            du_ref[sl, :] = (dt * c_ref[sl, :].astype(F32)).astype(BF16)
        _conv_dw(tpad_ref, gpad_ref, CONV_PAD, K, S, dw_ref)

    blk = pl.BlockSpec((S, LANE), lambda j: (0, j))
    act = jax.ShapeDtypeStruct((S, DS), BF16)
    return pl.pallas_call(
        body, name=name, grid=(DS // LANE,),
        in_specs=[blk, _col_spec(S, offs["sc_b"]), _col_spec(S, offs["sc_c"]), _col_spec(S, offs["sc_u"]),
                  pl.BlockSpec((None, K, LANE), lambda j: (l, 0, j))],
        out_specs=[blk, blk, blk, pl.BlockSpec((K, LANE), lambda j: (0, j))],
        out_shape=[act, act, act, jax.ShapeDtypeStruct((K, DS), F32)],
        scratch_shapes=[pltpu.VMEM((S + 2 * CONV_PAD, LANE), F32), pltpu.VMEM((S + 2 * CONV_PAD, LANE), F32)],
        compiler_params=_params("parallel"),
    )(dga, proj, proj, proj, conv_w)


def _cf_conv_fwd(proj, conv_w, conv_b3, l, offs, DC, name):
    S = proj.shape[0]
    K = conv_w.shape[1]
    rows = min(CONV_ROWS, S)

    def body(a_ref, g_ref, w_ref, bias_ref, o_ref, pad_ref):
        _zero_pads(pad_ref, S)
        pad_ref[pl.ds(CONV_PAD, S), :] = a_ref[...].astype(F32) * _sigmoid(g_ref[...].astype(F32))
        for r0 in range(0, S, rows):
            o_ref[pl.ds(r0, rows), :] = _conv_fwd_chunk(pad_ref, w_ref, K, r0, rows) + bias_ref[...]

    return pl.pallas_call(
        body, name=name, grid=(DC // LANE,),
        in_specs=[_col_spec(S, offs["cf_a"]), _col_spec(S, offs["cf_g"]),
                  pl.BlockSpec((None, K, LANE), lambda j: (l, 0, j)),
                  pl.BlockSpec((None, 1, LANE), lambda j: (l, 0, j))],
        out_specs=pl.BlockSpec((S, LANE), lambda j: (0, j)),
        out_shape=jax.ShapeDtypeStruct((S, DC), F32),
        scratch_shapes=[pltpu.VMEM((S + 2 * CONV_PAD, LANE), F32)],
        compiler_params=_params("parallel"),
    )(proj, proj, conv_w, conv_b3)


def _layer_norm_hat(u):
    mu = jnp.mean(u, axis=-1, keepdims=True)
    xc = u - mu
    rstd = lax.rsqrt(jnp.mean(xc * xc, axis=-1, keepdims=True) + LN_EPS)
    return xc * rstd, rstd


def _cf_norm_fwd(u1, gam3, bet3, l, name):
    S, DC = u1.shape
    tr = _row_tile(S)

    def body(u_ref, g_ref, b_ref, o_ref):
        xhat, _ = _layer_norm_hat(u_ref[...])
        s = xhat * g_ref[...] + b_ref[...]
        o_ref[...] = (s * _sigmoid(s)).astype(BF16)

    row = pl.BlockSpec((tr, DC), lambda i: (i, 0))
    vec = pl.BlockSpec((None, 1, DC), lambda i: (l, 0, 0))
    return pl.pallas_call(
        body, name=name, grid=(S // tr,),
        in_specs=[row, vec, vec], out_specs=row,
        out_shape=jax.ShapeDtypeStruct((S, DC), BF16),
        compiler_params=_params("parallel"),
    )(u1, gam3, bet3)


def _cf_norm_bwd(du2, u1, gam3, bet3, l, name):
    S, DC = u1.shape
    tr = _row_tile(S)

    def body(d_ref, u_ref, g_ref, b_ref, du_ref, dg_ref, db_ref):
        @pl.when(pl.program_id(0) == 0)
        def _():
            dg_ref[...] = jnp.zeros_like(dg_ref)
            db_ref[...] = jnp.zeros_like(db_ref)

        xhat, rstd = _layer_norm_hat(u_ref[...])
        s = xhat * g_ref[...] + b_ref[...]
        sg = _sigmoid(s)
        ds = d_ref[...] * (sg * (1.0 + s * (1.0 - sg)))
        dg_ref[...] += jnp.sum(ds * xhat, axis=0, keepdims=True)
        db_ref[...] += jnp.sum(ds, axis=0, keepdims=True)
        dxh = ds * g_ref[...]
        du_ref[...] = rstd * (dxh - jnp.mean(dxh, axis=-1, keepdims=True)
                              - xhat * jnp.mean(dxh * xhat, axis=-1, keepdims=True))

    row = pl.BlockSpec((tr, DC), lambda i: (i, 0))
    vec = pl.BlockSpec((None, 1, DC), lambda i: (l, 0, 0))
    acc = pl.BlockSpec((1, DC), lambda i: (0, 0))
    return pl.pallas_call(
        body, name=name, grid=(S // tr,),
        in_specs=[row, row, vec, vec], out_specs=[row, acc, acc],
        out_shape=[jax.ShapeDtypeStruct((S, DC), F32), jax.ShapeDtypeStruct((1, DC), F32),
                   jax.ShapeDtypeStruct((1, DC), F32)],
        compiler_params=_params("arbitrary"),
    )(du2, u1, gam3, bet3)


def _cf_conv_bwd(du1, proj, conv_w, l, offs, DC, name):
    S = proj.shape[0]
    K = conv_w.shape[1]
    rows = min(CONV_ROWS, S)

    def body(d_ref, a_ref, g_ref, w_ref, da_ref, dgl_ref, dw_ref, dbias_ref, upad_ref, dpad_ref):
        _zero_pads(upad_ref, S)
        _zero_pads(dpad_ref, S)
        upad_ref[pl.ds(CONV_PAD, S), :] = a_ref[...].astype(F32) * _sigmoid(g_ref[...].astype(F32))
        dpad_ref[pl.ds(CONV_PAD, S), :] = d_ref[...]
        dbias_ref[...] = jnp.sum(d_ref[...], axis=0, keepdims=True)
        for r0 in range(0, S, rows):
            sl = pl.ds(r0, rows)
            du0 = _conv_bwd_chunk(dpad_ref, w_ref, K, r0, rows)
            a = a_ref[sl, :].astype(F32)
            sg = _sigmoid(g_ref[sl, :].astype(F32))
            da_ref[sl, :] = (du0 * sg).astype(BF16)
            dgl_ref[sl, :] = (du0 * a * sg * (1.0 - sg)).astype(BF16)
        _conv_dw(upad_ref, dpad_ref, CONV_PAD, K, S, dw_ref)

    blk = pl.BlockSpec((S, LANE), lambda j: (0, j))
    act = jax.ShapeDtypeStruct((S, DC), BF16)
    return pl.pallas_call(
        body, name=name, grid=(DC // LANE,),
        in_specs=[blk, _col_spec(S, offs["cf_a"]), _col_spec(S, offs["cf_g"]),
                  pl.BlockSpec((None, K, LANE), lambda j: (l, 0, j))],
        out_specs=[blk, blk, pl.BlockSpec((K, LANE), lambda j: (0, j)), pl.BlockSpec((1, LANE), lambda j: (0, j))],
        out_shape=[act, act, jax.ShapeDtypeStruct((K, DC), F32), jax.ShapeDtypeStruct((1, DC), F32)],
        scratch_shapes=[pltpu.VMEM((S + 2 * CONV_PAD, LANE), F32), pltpu.VMEM((S + 2 * CONV_PAD, LANE), F32)],
        compiler_params=_params("parallel"),
    )(du1, proj, proj, conv_w)


def _dot_nt(a, b):
    return lax.dot_general(a, b, (((1,), (1,)), ((), ())), preferred_element_type=F32)


def _dot_nn(a, b):
    return lax.dot_general(a, b, (((1,), (0,)), ((), ())), preferred_element_type=F32)


def _dot_tn(a, b):
    return lax.dot_general(a, b, (((0,), (0,)), ((), ())), preferred_element_type=F32)


def _dot_split(x, u):
    hi = x.astype(BF16)
    lo = (x - hi.astype(F32)).astype(BF16)
    return _dot_nn(hi, u) + _dot_nn(lo, u)


MASKED = -1e30


def _log_fail(z):
    return -(jnp.maximum(z, 0.0) + jnp.log(1.0 + jnp.exp(-jnp.abs(z))))


def _head_group(H, want):
    g = min(want, H)
    while H % g:
        g -= 1
    return g


def _lanes(g):
    return slice(g * HEAD_DIM, (g + 1) * HEAD_DIM)


def _attn_fwd(proj, offs, DA, name):
    S = proj.shape[0]
    H = DA // HEAD_DIM
    G = _head_group(H, ATTN_FWD_HEADS)
    nb = S // QB

    def body(q_ref, k_ref, v_ref, o_ref, tot_ref):
        row = lax.broadcasted_iota(jnp.int32, (QB, QB), 0)
        col = lax.broadcasted_iota(jnp.int32, (QB, QB), 1)
        u_after = (row > col).astype(BF16)

        def rows_of(i, t):
            return pl.ds(pl.multiple_of(jnp.maximum(i - t, 0) * QB, QB), QB)

        def scores(i, t, g, q):
            return _dot_nt(q, k_ref[rows_of(i, t), _lanes(g)]) * (HEAD_DIM ** -0.5)

        def log_terms(i, t, z):
            valid = col < row + jnp.minimum(t, i) * QB
            lf = jnp.where(valid, _log_fail(z), 0.0)
            return jnp.where(valid, lf + z + _dot_split(lf, u_after), MASKED), jnp.sum(lf, axis=1, keepdims=True)

        def q_block(i, _):
            qs = pl.ds(pl.multiple_of(i * QB, QB), QB)
            qg = [q_ref[qs, _lanes(g)] for g in range(G)]

            def step(t, carry):
                out = []
                for g in range(G):
                    acc, c, z, (pre, rs) = carry[g]
                    a = jnp.exp(pre + c)
                    acc = acc + _dot_nn(a.astype(BF16), v_ref[rows_of(i, t - 2), _lanes(g)])
                    out.append((acc, c + rs, scores(i, t, g, qg[g]), log_terms(i, t - 1, z)))
                return tuple(out)

            init = []
            for g in range(G):
                z0 = scores(i, 0, g, qg[g])
                init.append((jnp.zeros((QB, HEAD_DIM), F32), jnp.zeros((QB, 1), F32), scores(i, 1, g, qg[g]),
                             log_terms(i, 0, z0)))
            res = lax.fori_loop(2, i + 3, step, tuple(init))
            for g in range(G):
                o_ref[qs, _lanes(g)] = res[g][0].astype(BF16)
                tot_ref[g, qs, :] = res[g][1]
            return 0

        lax.fori_loop(0, nb, q_block, 0)

    def hs(off):
        return pl.BlockSpec((S, G * HEAD_DIM), lambda h: (0, off // (G * HEAD_DIM) + h))

    return pl.pallas_call(
        body, name=name, grid=(H // G,),
        in_specs=[hs(offs["q"]), hs(offs["k"]), hs(offs["v"])],
        out_specs=[pl.BlockSpec((S, G * HEAD_DIM), lambda h: (0, h)), pl.BlockSpec((G, S, 1), lambda h: (h, 0, 0))],
        out_shape=[jax.ShapeDtypeStruct((S, DA), BF16), jax.ShapeDtypeStruct((H, S, 1), F32)],
        compiler_params=_params("parallel"),
    )(proj, proj, proj)


def _put_columns(into, pieces, first_col, name):
    S, w = pieces[0].shape
    n = len(pieces)

    def body(*refs):
        out_ref = refs[n + 1]
        for r in range(n):
            @pl.when(pl.program_id(0) == r)
            def _(r=r):
                out_ref[...] = refs[r][...]

    whole = pl.BlockSpec((S, w), lambda r: (0, 0))
    return pl.pallas_call(
        body, name=name, grid=(n,), in_specs=[whole] * n + [ANY],
        out_specs=pl.BlockSpec((S, w), lambda r: (0, first_col // w + r)),
        out_shape=jax.ShapeDtypeStruct(into.shape, into.dtype), input_output_aliases={n: 0},
        compiler_params=_params("arbitrary"),
    )(*pieces, into)


def _attn_bwd(dout, tot, proj, offs, DA, into, name):
    S = proj.shape[0]
    H = DA // HEAD_DIM
    G = _head_group(H, ATTN_BWD_HEADS)
    nb = S // QB
    scale = HEAD_DIM ** -0.5

    def body(q_ref, k_ref, v_ref, tot_ref, do_ref, into_ref, out_ref, dk_acc, dv_acc, stage_ref, keep_ref):
        part = pl.program_id(1)

        @pl.when(part == 0)
        def _():
            work(q_ref, k_ref, v_ref, tot_ref, do_ref, out_ref, keep_ref.at[0], keep_ref.at[1], dk_acc, dv_acc, stage_ref)

        @pl.when(part > 0)
        def _():
            out_ref[...] = keep_ref[part - 1]

    def work(q_ref, k_ref, v_ref, tot_ref, do_ref, dq_ref, dk_ref, dv_ref, dk_acc, dv_acc, stage_ref):
        row = lax.broadcasted_iota(jnp.int32, (QB, QB), 0)
        col = lax.broadcasted_iota(jnp.int32, (QB, QB), 1)
        u_after = (row > col).astype(BF16)
        u_before = (row < col).astype(BF16)
        dk_acc[...] = jnp.zeros_like(dk_acc)
        dv_acc[...] = jnp.zeros_like(dv_acc)

        def rows_of(i, b):
            return pl.ds(pl.multiple_of(jnp.minimum(b, i) * QB, QB), QB)

        def scores(i, b, g, q, do):
            ks = rows_of(i, b)
            return _dot_nt(q, k_ref[ks, _lanes(g)]) * scale, _dot_nt(do, v_ref[ks, _lanes(g)])

        def log_terms(i, b, z, da):
            valid = col < row + (i - jnp.minimum(b, i)) * QB
            lf = jnp.where(valid, _log_fail(z), 0.0)
            pre = jnp.where(valid, lf + z + _dot_split(lf, u_after), MASKED)
            return (pre, da, jnp.exp(lf), jnp.where(valid, jnp.exp(lf + z), 0.0),
                    jnp.sum(lf, axis=1, keepdims=True))

        def d_log_a(tot_q, seen, terms):
            pre, da, fail, beta, rs = terms
            seen = seen + rs
            a = jnp.exp(pre + (tot_q - seen))
            dlog = a * da
            return seen, (dlog, _dot_split(dlog, u_before), a.astype(BF16), fail, beta)

        def q_block(i, _):
            qs = pl.ds(pl.multiple_of(i * QB, QB), QB)
            qg = [q_ref[qs, _lanes(g)] for g in range(G)]
            dog = [do_ref[qs, _lanes(g)] for g in range(G)]
            totg = [tot_ref[g, qs, :] for g in range(G)]

            def put(g, first, tiles):
                for n, tile in enumerate(tiles):
                    stage_ref[g, first + n] = tile.astype(F32)

            def get(g, first, count):
                return [stage_ref[g, first + n] for n in range(count)]

            def step(t, carry):
                out = []
                for g in range(G):
                    dq, seen, gsum, rs = carry[g]
                    dlog, left, a, fail, beta = get(g, 6, 5)
                    terms = get(g, 2, 4) + [rs]
                    z, da = get(g, 0, 2)
                    ks = rows_of(i, t - 3)
                    dz = (dlog * fail - (gsum + left) * beta) * scale
                    dzb = dz.astype(BF16)
                    dk_acc[ks, _lanes(g)] += _dot_tn(dzb, qg[g])
                    dv_acc[ks, _lanes(g)] += _dot_tn(a.astype(BF16), dog[g])
                    dq = dq + _dot_nn(dzb, k_ref[ks, _lanes(g)])
                    gsum = gsum + jnp.sum(dlog, axis=1, keepdims=True)
                    seen, grads = d_log_a(totg[g], seen, terms)
                    terms = log_terms(i, t - 1, z, da)
                    put(g, 6, grads)
                    put(g, 2, terms[:4])
                    put(g, 0, scores(i, t, g, qg[g], dog[g]))
                    out.append((dq, seen, gsum, terms[4]))
                return tuple(out)

            zero = jnp.zeros((QB, 1), F32)
            init = []
            for g in range(G):
                terms0 = log_terms(i, 0, *scores(i, 0, g, qg[g], dog[g]))
                terms1 = log_terms(i, 1, *scores(i, 1, g, qg[g], dog[g]))
                seen, grads0 = d_log_a(totg[g], zero, terms0)
                put(g, 6, grads0)
                put(g, 2, terms1[:4])
                put(g, 0, scores(i, 2, g, qg[g], dog[g]))
                init.append((jnp.zeros((QB, HEAD_DIM), F32), seen, zero, terms1[4]))
            res = lax.fori_loop(3, i + 4, step, tuple(init))
            for g in range(G):
                dq_ref[qs, _lanes(g)] = res[g][0].astype(BF16)
            return 0

        lax.fori_loop(0, nb, q_block, 0)
        dk_ref[...] = dk_acc[...].astype(BF16)
        dv_ref[...] = dv_acc[...].astype(BF16)

    wide = G * HEAD_DIM

    def hs(off):
        return pl.BlockSpec((S, wide), lambda h, part: (0, off // wide + h))

    return pl.pallas_call(
        body, name=name, grid=(H // G, 3),
        in_specs=[hs(offs["q"]), hs(offs["k"]), hs(offs["v"]), pl.BlockSpec((G, S, 1), lambda h, part: (h, 0, 0)),
                  pl.BlockSpec((S, wide), lambda h, part: (0, h)), ANY],
        out_specs=pl.BlockSpec((S, wide), lambda h, part: (0, offs["q"] // wide + part * (DA // wide) + h)),
        out_shape=jax.ShapeDtypeStruct(into.shape, into.dtype), input_output_aliases={5: 0},
        scratch_shapes=[pltpu.VMEM((S, wide), F32), pltpu.VMEM((S, wide), F32), pltpu.VMEM((G, 11, QB, QB), F32),
                        pltpu.VMEM((2, S, wide), BF16)],
        compiler_params=_params("parallel", "arbitrary"),
    )(proj, proj, proj, tot, dout, into)


def _merge_tiles(S, D, goff):
    tn = LANE
    for t in range(LANE, 513, LANE):
        if D % t == 0 and goff % t == 0:
            tn = t
    return _tile(S, 1024, 8), tn


def _merge_fwd(ga, attn, u2, pa, pb, pc, proj, goff, name):
    S = ga.shape[0]
    D = pa.shape[-1]
    tm, tn = _merge_tiles(S, D, goff)

    def body(ga_ref, at_ref, u2_ref, pa_ref, pb_ref, pc_ref, la_ref, lb_ref, lc_ref, y_ref, m_ref):
        ya = _dot_nn(ga_ref[...], pa_ref[...])
        yb = _dot_nn(at_ref[...], pb_ref[...])
        yc = _dot_nn(u2_ref[...], pc_ref[...])
        y_ref[0] = ya.astype(BF16)
        y_ref[1] = yb.astype(BF16)
        y_ref[2] = yc.astype(BF16)
        m_ref[...] = (_sigmoid(la_ref[...].astype(F32)) * ya + _sigmoid(lb_ref[...].astype(F32)) * yb
                      + _sigmoid(lc_ref[...].astype(F32)) * yc).astype(BF16)

    def lhs(a):
        return pl.BlockSpec((tm, a.shape[1]), lambda i, j: (i, 0))

    def rhs(p):
        return pl.BlockSpec((p.shape[0], tn), lambda i, j: (0, j))

    def gate(r):
        return pl.BlockSpec((tm, tn), lambda i, j: (i, (goff + r * D) // tn + j))

    return pl.pallas_call(
        body, name=name, grid=(S // tm, D // tn),
        in_specs=[lhs(ga), lhs(attn), lhs(u2), rhs(pa), rhs(pb), rhs(pc), gate(0), gate(1), gate(2)],
        out_specs=[pl.BlockSpec((3, tm, tn), lambda i, j: (0, i, j)), pl.BlockSpec((tm, tn), lambda i, j: (i, j))],
        out_shape=[jax.ShapeDtypeStruct((3, S, D), BF16), jax.ShapeDtypeStruct((S, D), BF16)],
        compiler_params=_params("parallel", "parallel"),
    )(ga, attn, u2, pa, pb, pc, proj, proj, proj)


def _merge_bwd(dm, y3, proj, goff, name):
    S, D = dm.shape
    tm, tn = _merge_tiles(S, D, goff)

    def body(dm_ref, y_ref, l_ref, dy_ref, dl_ref):
        dmv = dm_ref[...]
        sg = _sigmoid(l_ref[...].astype(F32))
        dy_ref[...] = (dmv * sg).astype(BF16)
        dl_ref[...] = (dmv * y_ref[...].astype(F32) * sg * (1.0 - sg)).astype(BF16)

    gate = pl.BlockSpec((tm, tn), lambda i, j, r: (i, goff // tn + r * (D // tn) + j))
    branch = pl.BlockSpec((None, tm, tn), lambda i, j, r: (r, i, j))
    return pl.pallas_call(
        body, name=name, grid=(S // tm, D // tn, 3),
        in_specs=[pl.BlockSpec((tm, tn), lambda i, j, r: (i, j)), branch, gate],
        out_specs=[branch, gate],
        out_shape=[jax.ShapeDtypeStruct((3, S, D), BF16), jax.ShapeDtypeStruct(proj.shape, BF16)],
        compiler_params=_params("parallel", "parallel", "parallel"),
    )(dm, y3, proj)


def _ew_tiles(rows, cols):
    tc = cols if cols <= 4096 else _tile(cols, 2048)
    tr = _tile(rows, max(8, (1 << 19) // tc), 8)
    return tr, tc


def _half_rows_tile(Rh, Cs):
    return _tile(Rh, max(16, (1 << 20) // Cs), 16)


def _place_shard(w, l, kind, place, name):
    _, Rs, Cs = w.shape
    tr = _half_rows_tile(Rs, Cs)

    def body(pr_ref, w_ref, o_ref):
        o_ref[...] = w_ref[...].astype(BF16)

    if kind == "col":
        shape = (Rs, 4 * Cs)
        o_spec = pl.BlockSpec((tr, Cs), lambda i, pr: (i, pr[1]))
    else:
        shape = (4, Rs, Cs)
        o_spec = pl.BlockSpec((None, tr, Cs), lambda i, pr: (pr[1], i, 0))
    out = pl.pallas_call(
        body, name=name,
        grid_spec=pltpu.PrefetchScalarGridSpec(
            num_scalar_prefetch=1, grid=(Rs // tr,),
            in_specs=[pl.BlockSpec((None, tr, Cs), lambda i, pr: (l, i, 0))], out_specs=o_spec),
        out_shape=jax.ShapeDtypeStruct(shape, BF16), compiler_params=_params("parallel"),
    )(place, w)
    return out if kind == "col" else out.reshape(4 * Rs, Cs)


def _pair_sum(g, got, kind, place, name):
    _, Rh, Cs = got.shape
    tr = _half_rows_tile(Rh, Cs)
    if kind == "col":
        gv = g.reshape(2, Rh, 4 * Cs)
        g_spec = pl.BlockSpec((None, tr, Cs), lambda p, i, pr: (pr[0], i, p))
    else:
        gv = g.reshape(4, 2, Rh, Cs)
        g_spec = pl.BlockSpec((None, None, tr, Cs), lambda p, i, pr: (p, pr[0], i, 0))
    blk = pl.BlockSpec((None, tr, Cs), lambda p, i, pr: (p, i, 0))

    def body(pr_ref, g_ref, r_ref, o_ref):
        o_ref[...] = (g_ref[...].astype(F32) + r_ref[...].astype(F32)).astype(BF16)

    return pl.pallas_call(
        body, name=name,
        grid_spec=pltpu.PrefetchScalarGridSpec(num_scalar_prefetch=1, grid=(4, Rh // tr),
                                               in_specs=[g_spec, blk], out_specs=blk),
        out_shape=jax.ShapeDtypeStruct(got.shape, BF16), compiler_params=_params("parallel", "parallel"),
    )(place, gv, got)


def _chip_sum(part, got, place, stack, name):
    n, Rh, Cs = got.shape
    l, L, buf = stack
    tr = _half_rows_tile(Rh, Cs)
    prev = () if buf is None else (buf,)

    def body(pr_ref, p_ref, r_ref, *rest):
        acc = p_ref[...].astype(F32)
        for s in range(n):
            acc = acc + r_ref[s].astype(F32)
        rest[-1][...] = acc

    return pl.pallas_call(
        body, name=name,
        grid_spec=pltpu.PrefetchScalarGridSpec(
            num_scalar_prefetch=1, grid=(Rh // tr,),
            in_specs=[pl.BlockSpec((None, tr, Cs), lambda i, pr: (pr[1], i, 0)),
                      pl.BlockSpec((n, tr, Cs), lambda i, pr: (0, i, 0))] + [ANY] * len(prev),
            out_specs=pl.BlockSpec((None, None, tr, Cs), lambda i, pr: (l, pr[0], i, 0))),
        out_shape=jax.ShapeDtypeStruct((L, 2, Rh, Cs), F32), input_output_aliases={3: 0} if prev else {},
        compiler_params=_params("parallel"),
    )(place, part, got, *prev)


def _adamw(w, g, m, v, name):
    shape = w.shape
    args = [t.reshape(-1, shape[-1]) for t in (w, g, m, v)]
    rows, cols = args[0].shape
    tr, tc = _ew_tiles(rows, cols)
    c1 = 1.0 - ADAM_B1 ** ADAM_STEP
    c2 = 1.0 - ADAM_B2 ** ADAM_STEP

    def body(w_ref, g_ref, m_ref, v_ref, d_ref, nm_ref, nv_ref, go_ref):
        gv = g_ref[...]
        nm = ADAM_B1 * m_ref[...] + (1.0 - ADAM_B1) * gv
        nv = ADAM_B2 * v_ref[...] + (1.0 - ADAM_B2) * (gv * gv)
        nm_ref[...] = nm
        nv_ref[...] = nv
        go_ref[...] = gv
        d_ref[...] = -ADAM_LR * ((nm / c1) / (jnp.sqrt(nv / c2) + ADAM_EPS) + ADAM_WD * w_ref[...])

    blk = pl.BlockSpec((tr, tc), lambda i, j: (i, j))
    shp = jax.ShapeDtypeStruct((rows, cols), F32)
    outs = pl.pallas_call(
        body, name=name, grid=(rows // tr, cols // tc), in_specs=[blk] * 4, out_specs=[blk] * 4,
        out_shape=[shp] * 4, compiler_params=_params("parallel", "parallel"),
    )(*args)
    return [o.reshape(shape) for o in outs]


def _place():
    x, y, c = lax.axis_index("x"), lax.axis_index("y"), lax.axis_index("c")
    chips = [(1 - x, y), (x, 1 - y), (1 - x, 1 - y)]
    return x, y, c, chips


def _al(v, unit):
    return pl.multiple_of(v, unit) if unit % LANE == 0 else v


def _half_of_full(ref, kind, p, half, Rs, Cs):
    Rh = (ref.shape[-2] // 2) if kind == "col" else Rs // 2
    lead = (slice(None),) * (len(ref.shape) - 2)
    if kind == "col":
        return ref.at[lead + (pl.ds(_al(half * Rh, Rh), Rh), pl.ds(_al(p * Cs, Cs), Cs))]
    return ref.at[lead + (pl.ds(_al(p * Rs + half * Rh, Rh), Rh), slice(None))]


HBM = pl.BlockSpec(memory_space=pltpu.HBM)
SEM = pl.BlockSpec(memory_space=pltpu.SEMAPHORE)
EFFECT = pltpu.SideEffectType.DATAFLOW_SIDE_EFFECTING
FIRST_GATHER_ID = 0
GATHER_ID = 2
REDUCE_ID = 1


def _in_hbm(v):
    return pltpu.with_memory_space_constraint(v, pltpu.HBM)


def _ici_handshake(chips, c):
    barrier = pltpu.get_barrier_semaphore()
    for px, py in chips:
        pl.semaphore_signal(barrier, inc=1, device_id=(px, py, c), device_id_type=MESH)
    pl.semaphore_wait(barrier, len(chips))


def _shard_dims(ref, kind):
    R, C = ref.shape[-2:]
    return (R, C // 4) if kind == "col" else (R // 4, C)


def _gather_start(groups, after, collective_id, name):
    bufs = [b for grp in groups for b, _ in grp]
    kinds = [k for grp in groups for _, k in grp]
    m, ng = len(bufs), len(groups)

    def body(*refs):
        ins = refs[:m]
        sems = refs[m + 1:m + 1 + 2 * ng]
        token = refs[-1]
        x, y, c, chips = _place()
        _ici_handshake(chips, c)
        me = 2 * x + y
        at = 0
        for gi, grp in enumerate(groups):
            n = len(grp)
            for j, chip in enumerate(chips):
                for w in range(n):
                    ref, kind = ins[at + w], kinds[at + w]
                    mine = _half_of_full(ref, kind, me, c, *_shard_dims(ref, kind))
                    pltpu.make_async_remote_copy(mine, mine, sems[2 * gi].at[j * n + w], sems[2 * gi + 1].at[j * n + w],
                                                 device_id=(*chip, c), device_id_type=MESH).start()
            at += n
        token[...] = jnp.zeros_like(token)

    sem_shapes = [pltpu.SemaphoreType.DMA((3 * len(grp),)) for grp in groups for _ in range(2)]
    outs = pl.pallas_call(
        body, name=name, in_specs=[HBM] * m + [ANY],
        out_specs=[SEM] * (2 * ng) + [HBM] * m + [pl.BlockSpec(memory_space=pltpu.VMEM)],
        out_shape=sem_shapes + [pltpu.HBM(b.shape, b.dtype) for b in bufs] + [jax.ShapeDtypeStruct((8, LANE), F32)],
        input_output_aliases={i: 2 * ng + i for i in range(m)},
        compiler_params=pltpu.CompilerParams(has_side_effects=EFFECT, collective_id=collective_id),
    )(*[_in_hbm(b) for b in bufs], after)
    res, at = [], 2 * ng
    for gi, grp in enumerate(groups):
        res.append((outs[2 * gi], outs[2 * gi + 1], outs[at:at + len(grp)]))
        at += len(grp)
    return res, outs[-1]


def _gather_wait(bufs, kinds, send_sem, recv_sem, after, name):
    n = len(bufs)

    def body(*refs):
        ins = refs[:n]
        send, recv = refs[n], refs[n + 1]
        x, y, c, chips = _place()
        me = 2 * x + y
        for j, (px, py) in enumerate(chips):
            for w in range(n):
                dims = _shard_dims(ins[w], kinds[w])
                mine = _half_of_full(ins[w], kinds[w], me, c, *dims)
                theirs = _half_of_full(ins[w], kinds[w], 2 * px + py, c, *dims)
                cp = pltpu.make_async_remote_copy(mine, theirs, send.at[j * n + w], recv.at[j * n + w],
                                                  device_id=(px, py, c), device_id_type=MESH)
                cp.wait_send()
                cp.wait_recv()

    return pl.pallas_call(
        body, name=name, in_specs=[HBM] * n + [SEM, SEM, ANY], out_specs=[HBM] * n,
        out_shape=[pltpu.HBM(b.shape, b.dtype) for b in bufs],
        input_output_aliases={i: i for i in range(n)},
        compiler_params=pltpu.CompilerParams(has_side_effects=EFFECT),
    )(*bufs, send_sem, recv_sem, after)


def _ride_forward(bufs, kinds):
    n = len(bufs)

    def copies(ins, outs, send_sem, recv_sem):
        x, y, c, chips = _place()
        made = []
        for j, (px, py) in enumerate(chips):
            for w in range(n):
                got = _half_of_full(outs[w], kinds[w], 2 * px + py, c, *_shard_dims(outs[w], kinds[w]))
                made.append(pltpu.make_async_remote_copy(got, got, send_sem.at[j * n + w], recv_sem.at[j * n + w],
                                                         device_id=(x, y, 1 - c), device_id_type=MESH))
        return made

    return dict(arrays=list(bufs), out_shape=[jax.ShapeDtypeStruct(b.shape, b.dtype) for b in bufs], alias=True,
                n_sems=3 * n, copies=copies)


def _ride_halves(grads, kinds):
    n = len(grads)
    shapes = []
    for g, kind in zip(grads, kinds):
        R, C = g.shape
        shapes.append((4, R // 2, C // 4) if kind == "col" else (4, R // 8, C))

    def copies(ins, outs, send_sem, recv_sem):
        x, y, c, _ = _place()
        made = []
        for w in range(n):
            _, Rh, Cs = shapes[w]
            for p in range(4):
                made.append(pltpu.make_async_remote_copy(
                    _half_of_full(ins[w], kinds[w], p, 1 - c, 2 * Rh, Cs), outs[w].at[p], send_sem.at[4 * w + p],
                    recv_sem.at[4 * w + p], device_id=(x, y, 1 - c), device_id_type=MESH))
        return made

    return dict(arrays=list(grads), out_shape=[jax.ShapeDtypeStruct(s, BF16) for s in shapes], alias=False,
                n_sems=4 * n, copies=copies)


def _exchange_now(ride, name):
    n = len(ride["arrays"])

    def body(*refs):
        made = ride["copies"](refs[:n], refs[n:2 * n], refs[2 * n], refs[2 * n + 1])
        for cp in made:
            cp.start()
        for cp in made:
            cp.wait()

    return pl.pallas_call(
        body, name=name, in_specs=[ANY] * n, out_specs=[ANY] * n, out_shape=ride["out_shape"],
        input_output_aliases={w: w for w in range(n)} if ride["alias"] else {},
        scratch_shapes=[pltpu.SemaphoreType.DMA((ride["n_sems"],))] * 2,
    )(*ride["arrays"])


def _reduce_start(parts, name):
    n = len(parts)
    zones = [lax.empty((3,) + p.shape[1:], p.dtype) for p in parts]

    def body(*refs):
        ins, lands = refs[:n], refs[n:2 * n]
        send, recv = refs[2 * n], refs[2 * n + 1]
        token = refs[-1]
        x, y, c, chips = _place()
        _ici_handshake(chips, c)
        for j, (px, py) in enumerate(chips):
            for w in range(n):
                pltpu.make_async_remote_copy(ins[w].at[2 * px + py], lands[w].at[j], send.at[j * n + w],
                                             recv.at[j * n + w], device_id=(px, py, c), device_id_type=MESH).start()
        token[...] = jnp.zeros_like(token)

    both = list(parts) + zones
    outs = pl.pallas_call(
        body, name=name, in_specs=[HBM] * (2 * n),
        out_specs=[SEM, SEM] + [HBM] * (2 * n) + [pl.BlockSpec(memory_space=pltpu.VMEM)],
        out_shape=([pltpu.SemaphoreType.DMA((3 * n,))] * 2 + [pltpu.HBM(b.shape, b.dtype) for b in both]
                   + [jax.ShapeDtypeStruct((8, LANE), F32)]),
        input_output_aliases={i: 2 + i for i in range(2 * n)},
        compiler_params=pltpu.CompilerParams(has_side_effects=EFFECT, collective_id=REDUCE_ID),
    )(*[_in_hbm(b) for b in both])
    return outs[0], outs[1], outs[2:2 + n], outs[2 + n:2 + 2 * n], outs[-1]


def _reduce_wait(parts, zones, send_sem, recv_sem, after, name):
    n = len(parts)

    def body(*refs):
        ins, lands = refs[:n], refs[n:2 * n]
        send, recv = refs[2 * n], refs[2 * n + 1]
        x, y, c, chips = _place()
        for j, (px, py) in enumerate(chips):
            for w in range(n):
                cp = pltpu.make_async_remote_copy(ins[w].at[2 * px + py], lands[w].at[j], send.at[j * n + w],
                                                  recv.at[j * n + w], device_id=(px, py, c), device_id_type=MESH)
                cp.wait_send()
                cp.wait_recv()

    both = list(parts) + list(zones)
    outs = pl.pallas_call(
        body, name=name, in_specs=[HBM] * (2 * n) + [SEM, SEM, ANY], out_specs=[HBM] * (2 * n),
        out_shape=[pltpu.HBM(b.shape, b.dtype) for b in both],
        input_output_aliases={i: i for i in range(2 * n)},
        compiler_params=pltpu.CompilerParams(has_side_effects=EFFECT),
    )(*both, send_sem, recv_sem, after)
    return outs[:n], outs[n:]


def _share_with_sibling(reduced, name):
    n = len(reduced)

    def body(*refs):
        outs = refs[n:2 * n]
        send_sem, recv_sem = refs[2 * n:]
        x, y, c, _ = _place()
        sib = (x, y, 1 - c)
        copies = []
        for w in range(n):
            mine = outs[w].at[:, c]
            cp = pltpu.make_async_remote_copy(mine, mine, send_sem.at[w], recv_sem.at[w], device_id=sib,
                                              device_id_type=MESH)
            cp.start()
            copies.append(cp)
        for cp in copies:
            cp.wait()

    return pl.pallas_call(
        body, name=name, in_specs=[ANY] * n, out_specs=[ANY] * n,
        out_shape=[jax.ShapeDtypeStruct(r.shape, r.dtype) for r in reduced],
        input_output_aliases={w: w for w in range(n)},
        scratch_shapes=[pltpu.SemaphoreType.DMA((n,))] * 2,
    )(*reduced)


def _all_gather_small(v):
    r = v.shape[0]

    def body(v_ref, o_ref, send_sem, recv_sem):
        x, y, c, _ = _place()
        me = 4 * x + 2 * y + c
        o_ref[me] = v_ref[...]
        copies = []
        for k in range(1, 8):
            peer = (x ^ (k >> 2), y ^ ((k >> 1) & 1), c ^ (k & 1))
            cp = pltpu.make_async_remote_copy(v_ref, o_ref.at[me], send_sem.at[k - 1], recv_sem.at[k - 1],
                                              device_id=peer, device_id_type=MESH)
            cp.start()
            copies.append(cp)
        for cp in copies:
            cp.wait()

    vmem = pl.BlockSpec(memory_space=pltpu.VMEM)
    return pl.pallas_call(
        body, name="all_gather_small", in_specs=[vmem], out_specs=vmem,
        out_shape=jax.ShapeDtypeStruct((8, r, LANE), F32),
        scratch_shapes=[pltpu.SemaphoreType.DMA((7,)), pltpu.SemaphoreType.DMA((7,))],
        compiler_params=pltpu.CompilerParams(vmem_limit_bytes=VMEM_LIMIT),
    )(v)


def _sum_slots(g):
    n, r, _ = g.shape

    def body(g_ref, o_ref):
        acc = g_ref[0]
        for s in range(1, n):
            acc = acc + g_ref[s]
        o_ref[...] = acc

    vmem = pl.BlockSpec(memory_space=pltpu.VMEM)
    return pl.pallas_call(
        body, name="sum_slots", in_specs=[vmem], out_specs=vmem, out_shape=jax.ShapeDtypeStruct((r, LANE), F32),
        compiler_params=pltpu.CompilerParams(vmem_limit_bytes=VMEM_LIMIT),
    )(g)


def _pack(arrays):
    flat = jnp.concatenate([a.reshape(-1) for a in arrays])
    pad = (-flat.shape[0]) % (8 * LANE)
    return jnp.pad(flat, (0, pad)).reshape(-1, LANE)


def _unpack(packed, like):
    flat = packed.reshape(-1)
    out, off = [], 0
    for a in like:
        out.append(flat[off:off + a.size].reshape(a.shape))
        off += a.size
    return out


def _offsets(D):
    DA, DS, DC = D // 2, D // 4, D // 4
    names = ["q", "k", "v", "sc_b", "sc_c", "sc_u", "cf_a", "cf_g", "gate"]
    sizes = [DA, DA, DA, DS, DS, DS, DC, DC, 3 * D]
    offs, o = {}, 0
    for nm, sz in zip(names, sizes):
        offs[nm] = o
        o += sz
    return offs, DA, DS, DC


def _relu2_epilogue(acc):
    r = jnp.maximum(acc, 0.0)
    return acc, r * r


def _drelu2_epilogue(acc, up):
    return (acc * (2.0 * jnp.maximum(up.astype(F32), 0.0)),)


def _local_step(x, target, gains, conv_a_w, conv_c_w, conv_c_b, norm_c_g, norm_c_b, landed, emit_grads):
    S, D = x.shape
    L = gains[0].shape[0]
    offs, DA, DS, DC = _offsets(D)
    goff = offs["gate"]
    g_mix_pre, g_mix_post, g_mlp_pre, g_mlp_post = gains
    cb3, ng3, nb3 = (t.reshape(L, 1, DC) for t in (conv_c_b, norm_c_g, norm_c_b))

    saved = []
    h = _rms_fwd(x, g_mix_pre, 0, "rms_first")
    xin = x
    def whole(l, part, after, carrier=None):
        names, halves, kinds = landed(l, part, after)
        ride = _ride_forward(halves, kinds)
        if carrier is None:
            return list(zip(names, _exchange_now(ride, f"gather_forward_{part}_{l}"))), None
        result, made = carrier(ride)
        return list(zip(names, made)), result

    next_in, _ = whole(0, "in", xin)
    for l in range(L):
        big = dict(next_in)
        proj = _mm(h, big["w_in"], out_dtypes=(BF16,), name=f"fwd_w_in_{l}", tn_cap=3072)
        ga = _sc_fwd(proj, conv_a_w, l, offs, DS, f"sc_fwd_{l}")
        attn, attn_tot = _attn_fwd(proj, offs, DA, f"attn_fwd_{l}")
        u1 = _cf_conv_fwd(proj, conv_c_w, cb3, l, offs, DC, f"cf_conv_fwd_{l}")
        u2 = _cf_norm_fwd(u1, ng3, nb3, l, f"cf_norm_fwd_{l}")
        big.update(whole(l, "rest", attn)[0])
        y3, merged = _merge_fwd(ga, attn, u2, big["proj_a"], big["proj_b"], big["proj_c"], proj, goff,
                                f"merge_fwd_{l}")
        mlp_weights, mixed = whole(l, "mlp", merged, lambda ride: _mm(merged, big["w_o"], name=f"fwd_w_o_{l}", ride=ride))
        big.update(mlp_weights)
        x1, h2 = _post_res_fwd(xin, mixed, g_mix_post, l, g_mlp_pre, l, f"mix_residual_{l}")
        up, act = _mm(h2, big["w_up"], out_dtypes=(BF16, BF16), epilogue=_relu2_epilogue, name=f"fwd_w_up_{l}")
        if l + 1 < L:
            next_in, f = whole(l + 1, "in", act,
                               lambda ride: _mm(act, big["w_down"], name=f"fwd_w_down_{l}", ride=ride))
        else:
            f = _mm(act, big["w_down"], name=f"fwd_w_down_{l}")
        saved.append(dict(big=big, xin=xin, h=h, proj=proj, ga=ga, attn=attn, attn_tot=attn_tot, u1=u1, u2=u2, y3=y3,
                          merged=merged, mixed=mixed, x1=x1, h2=h2, up=up, act=act, f=f))
        if l + 1 < L:
            xin, h = _post_res_fwd(x1, f, g_mlp_post, l, g_mix_pre, l + 1, f"mlp_residual_{l}")
        else:
            dx, loss = _final_fwd_loss(x1, f, g_mlp_post, l, target, "loss_head")

    small = {k: [None] * L for k in ("mix_pre", "mix_post", "mlp_pre", "mlp_post", "conv_a_w", "conv_c_w",
                                       "conv_c_b", "norm_c_g", "norm_c_b")}

    def dw(key, a, b, l, **kw):
        return _mm(a, b, ta=True, out_dtypes=(BF16,), name=f"d{key}_{l}", **kw)

    def halves_of(g):
        names = [k for k in BIG if k in g]
        return names, [BIG_KIND[k] for k in names]

    for l in reversed(range(L)):
        s = saved[l]
        big = s["big"]
        g = {}
        df, small["mlp_post"][l] = _post_bwd(dx, s["f"], g_mlp_post, l, f"mlp_post_bwd_{l}")
        g["w_down"] = dw("w_down", s["act"], df, l)
        dup = _mm(df, big["w_down"], tb=True, out_dtypes=(BF16,), epilogue=_drelu2_epilogue, extras=(s["up"],),
                  name=f"d_up_{l}")
        g["w_up"] = dw("w_up", s["h2"], dup, l)
        names, kinds = halves_of(g)
        dh2, got = _mm(dup, big["w_up"], tb=True, name=f"d_h2_{l}", ride=_ride_halves([g[k] for k in names], kinds))
        dx1, small["mlp_pre"][l] = _pre_bwd(dx, dh2, s["x1"], g_mlp_pre, l, f"mlp_pre_bwd_{l}")
        g_mix_post = g_mix_post + emit_grads(l, "mlp", names, [g[k] for k in names], got)[0, 0]
        g = {}
        dmixed, small["mix_post"][l] = _post_bwd(dx1, s["mixed"], g_mix_post, l, f"mix_post_bwd_{l}")
        g["w_o"] = dw("w_o", s["merged"], dmixed, l)
        dmerged = _mm(dmixed, big["w_o"], tb=True, name=f"d_merged_{l}")
        dy3, dproj = _merge_bwd(dmerged, s["y3"], s["proj"], goff, f"merge_bwd_{l}")
        g["proj_a"] = dw("proj_a", s["ga"], dy3, l, b_at=0)
        g["proj_b"] = dw("proj_b", s["attn"], dy3, l, b_at=1)
        g["proj_c"] = dw("proj_c", s["u2"], dy3, l, b_at=2)
        dga = _mm(dy3, big["proj_a"], tb=True, name=f"d_ga_{l}", a_at=0)
        dattn = _mm(dy3, big["proj_b"], tb=True, out_dtypes=(BF16,), name=f"d_attn_{l}", a_at=1)
        du2 = _mm(dy3, big["proj_c"], tb=True, name=f"d_u2_{l}", a_at=2)
        dsb, dsc, dsu, small["conv_a_w"][l] = _sc_bwd(dga, s["proj"], conv_a_w, l, offs, DS, f"sc_bwd_{l}")
        du1, small["norm_c_g"][l], small["norm_c_b"][l] = _cf_norm_bwd(du2, s["u1"], ng3, nb3, l, f"cf_norm_bwd_{l}")
        dca, dcg, small["conv_c_w"][l], small["conv_c_b"][l] = _cf_conv_bwd(du1, s["proj"], conv_c_w, l, offs, DC,
                                                                          f"cf_conv_bwd_{l}")
        dproj = _put_columns(dproj, [dsb, dsc, dsu, dca, dcg], offs["sc_b"], f"conv_grads_{l}")
        dproj = _attn_bwd(dattn, s["attn_tot"], s["proj"], offs, DA, dproj, f"attn_bwd_{l}")
        g["w_in"] = dw("w_in", s["h"], dproj, l, tn_cap=3072)
        names, kinds = halves_of(g)
        dh, got = _mm(dproj, big["w_in"], tb=True, name=f"d_h_{l}", tk_cap=3072,
                      ride=_ride_halves([g[k] for k in names], kinds))
        dx, small["mix_pre"][l] = _pre_bwd(dx1, dh, s["xin"], g_mix_pre, l, f"mix_pre_bwd_{l}")
        g_mlp_post = g_mlp_post + emit_grads(l, "mix", names, [g[k] for k in names], got)[0, 0]
    return loss, dx, small


BIG = ("w_in", "proj_a", "proj_b", "proj_c", "w_o", "w_up", "w_down")
BIG_KIND = {"w_in": "col", "proj_a": "col", "proj_b": "col", "proj_c": "col", "w_o": "row", "w_up": "col",
            "w_down": "row"}


def kernel(x, ln_mix_pre, ln_mix_post, ln_mlp_pre, ln_mlp_post, w_in, conv_a_w, proj_a, proj_b, conv_c_w, conv_c_b, norm_c_g, norm_c_b, proj_c, w_o, w_up, w_down, loss_target, m_ln_mix_pre, m_ln_mix_post, m_ln_mlp_pre, m_ln_mlp_post, m_w_in, m_conv_a_w, m_proj_a, m_proj_b, m_conv_c_w, m_conv_c_b, m_norm_c_g, m_norm_c_b, m_proj_c, m_w_o, m_w_up, m_w_down, v_ln_mix_pre, v_ln_mix_post, v_ln_mlp_pre, v_ln_mlp_post, v_w_in, v_conv_a_w, v_proj_a, v_proj_b, v_conv_c_w, v_conv_c_b, v_norm_c_g, v_norm_c_b, v_proj_c, v_w_o, v_w_up, v_w_down):
    weights = dict(ln_mix_pre=ln_mix_pre, ln_mix_post=ln_mix_post, ln_mlp_pre=ln_mlp_pre, ln_mlp_post=ln_mlp_post,
                   w_in=w_in, conv_a_w=conv_a_w, proj_a=proj_a, proj_b=proj_b, conv_c_w=conv_c_w, conv_c_b=conv_c_b,
                   norm_c_g=norm_c_g, norm_c_b=norm_c_b, proj_c=proj_c, w_o=w_o, w_up=w_up, w_down=w_down)
    m_in = dict(ln_mix_pre=m_ln_mix_pre, ln_mix_post=m_ln_mix_post, ln_mlp_pre=m_ln_mlp_pre, ln_mlp_post=m_ln_mlp_post,
                w_in=m_w_in, conv_a_w=m_conv_a_w, proj_a=m_proj_a, proj_b=m_proj_b, conv_c_w=m_conv_c_w,
                conv_c_b=m_conv_c_b, norm_c_g=m_norm_c_g, norm_c_b=m_norm_c_b, proj_c=m_proj_c, w_o=m_w_o,
                w_up=m_w_up, w_down=m_w_down)
    v_in = dict(ln_mix_pre=v_ln_mix_pre, ln_mix_post=v_ln_mix_post, ln_mlp_pre=v_ln_mlp_pre, ln_mlp_post=v_ln_mlp_post,
                w_in=v_w_in, conv_a_w=v_conv_a_w, proj_a=v_proj_a, proj_b=v_proj_b, conv_c_w=v_conv_c_w,
                conv_c_b=v_conv_c_b, norm_c_g=v_norm_c_g, norm_c_b=v_norm_c_b, proj_c=v_proj_c, w_o=v_w_o,
                w_up=v_w_up, w_down=v_w_down)
    order = list(weights)
    L, D = ln_mix_pre.shape
    chip = 2 * lax.axis_index("x") + lax.axis_index("y")
    place = jnp.stack([lax.axis_index("c"), chip]).astype(jnp.int32)

    conv_local = [conv_a_w, conv_c_w]
    slots = _all_gather_small(_pack(conv_local))
    gather_groups = {"in": ("w_in",), "rest": ("proj_a", "proj_b", "proj_c", "w_o"), "mlp": ("w_up", "w_down")}
    def placed(layers, where):
        return [[(_place_shard(weights[k], l, BIG_KIND[k], where, f"place_{k}_{l}"), BIG_KIND[k]) for k in names]
                for l in layers for names in gather_groups.values()]

    first, token = _gather_start(placed([0], place), slots, FIRST_GATHER_ID, "gather_start_first")
    later, started_all = _gather_start(placed(range(1, L), place + token[0, 0].astype(jnp.int32)), token, GATHER_ID,
                                       "gather_start")
    started = first + later
    in_flight = {(l, part): started[l * len(gather_groups) + i]
                 for l in range(L) for i, part in enumerate(gather_groups)}

    def landed(l, part, after):
        names = gather_groups[part]
        kinds = [BIG_KIND[k] for k in names]
        send_sem, recv_sem, bufs = in_flight[l, part]
        return names, _gather_wait(bufs, kinds, send_sem, recv_sem, after, f"gather_wait_{part}_{l}"), kinds

    pending = []

    def emit_grads(l, part, names, glist, got):
        kinds = [BIG_KIND[k] for k in names]
        pair = [_pair_sum(gk, r, kd, place, f"pair_sum_{k}_{l}") for k, kd, gk, r in zip(names, kinds, glist, got)]
        send_sem, recv_sem, parts, zones, token = _reduce_start(pair, f"reduce_start_{part}_{l}")
        pending.append((l, part, names, parts, zones, send_sem, recv_sem))
        return token

    per_chip = [_unpack(slots[4 * px + 2 * py], conv_local) for px in range(2) for py in range(2)]
    conv_a_full = jnp.concatenate([pc[0] for pc in per_chip], axis=-1)
    conv_c_full = jnp.concatenate([pc[1] for pc in per_chip], axis=-1)

    gains = [weights[k].reshape(L, 1, D) for k in ("ln_mix_pre", "ln_mix_post", "ln_mlp_pre", "ln_mlp_post")]
    gains[0] = gains[0] + started_all[0, 0]
    loss, dx, small = _local_step(x[0], loss_target[0], gains, conv_a_full, conv_c_full, conv_c_b, norm_c_g,
                                  norm_c_b, landed, emit_grads)

    grads, delta, new_m, new_v = {}, {}, {}, {}

    def finish(part, after):
        reduced = {}
        for l, p, names, parts, zones, send_sem, recv_sem in pending:
            if p == part:
                parts, landed = _reduce_wait(parts, zones, send_sem, recv_sem, after, f"reduce_wait_{p}_{l}")
                for k, mine, theirs in zip(names, parts, landed):
                    reduced[k] = _chip_sum(mine, theirs, place, (l, L, reduced.get(k)), f"chip_sum_{k}_{l}")
        for k, r in zip(reduced, _share_with_sibling(list(reduced.values()), f"reduce_share_{part}")):
            whole = r.reshape(r.shape[0], r.shape[1] * r.shape[2], r.shape[3])
            delta[k], new_m[k], new_v[k], grads[k] = _adamw(weights[k], whole, m_in[k], v_in[k], f"adamw_{k}")

    finish("mlp", dx)
    finish("mix", delta["w_down"])

    small_names = ["ln_mix_pre", "ln_mix_post", "ln_mlp_pre", "ln_mlp_post", "conv_a_w", "conv_c_w", "conv_c_b",
                   "norm_c_g", "norm_c_b"]
    small_key = dict(ln_mix_pre="mix_pre", ln_mix_post="mix_post", ln_mlp_pre="mlp_pre", ln_mlp_post="mlp_post")
    small_local = []
    for k in small_names:
        per_layer = small[small_key.get(k, k)]
        stacked = jnp.stack(per_layer)
        small_local.append(stacked.reshape(L, -1) if stacked.shape[1] == 1 else stacked)
    small_sum = _unpack(_sum_slots(_all_gather_small(_pack(small_local))), small_local)
    for k, g in zip(small_names, small_sum):
        if k in ("conv_a_w", "conv_c_w"):
            width = weights[k].shape[-1]
            g = lax.dynamic_slice_in_dim(g, chip * width, width, axis=2)
        grads[k] = g

    packed = [_pack([t[k] for k in small_names]) for t in (weights, grads, m_in, v_in)]
    like = [weights[k] for k in small_names]
    for dst, res in zip((delta, new_m, new_v), _adamw(*packed, "adamw_small")):
        dst.update(zip(small_names, _unpack(res, like)))

    total = lax.psum(loss[0, 0], ("x", "y", "c"))
    return (total, dx[None], *[grads[k] for k in order], *[delta[k] for k in order],
            *[new_m[k] for k in order], *[new_v[k] for k in order])
```
